```python
import math
import jax, jax.numpy as jnp
from jax import lax
import numpy as np

D_MODEL = 1024
BATCH = 8
SEQ = 2048
DEPTH = 1
DEC_BATCH = 128
DEC_SEQ = 4
PAST_LEN = 16384
PAGE_SIZE = 128

MIX_WIDTH = D_MODEL
A_WIDTH = MIX_WIDTH // 2
B_WIDTH = MIX_WIDTH - A_WIDTH
A_HEAD_DIM = 128
A_KEY_DIM = 128
A_HEADS = A_WIDTH // A_HEAD_DIM
A_QK = A_HEADS * A_KEY_DIM
CONV_WIDTH = 31
CHUNK = 64
EPS = 1e-6
IN_WIDTH = A_QK + A_QK + A_WIDTH + A_WIDTH + 2 * B_WIDTH + B_WIDTH

kernel_name = "hymba_hgrn2_conformer_conv_step"


def rmsnorm(x, g):
    xf = x.astype(jnp.float32)
    y = xf * lax.rsqrt(jnp.mean(xf * xf, axis=-1, keepdims=True) + EPS)
    return (y * g.astype(jnp.float32)).astype(x.dtype)


def layernorm(x, g, b):
    xf = x.astype(jnp.float32)
    mu = jnp.mean(xf, axis=-1, keepdims=True)
    var = jnp.mean(jnp.square(xf - mu), axis=-1, keepdims=True)
    y = (xf - mu) * lax.rsqrt(var + EPS)
    return (y * g.astype(jnp.float32) + b.astype(jnp.float32)).astype(x.dtype)


def hgrn2_chunked(q, k, v, logf, s0):
    n, t, h, dk = q.shape
    dv = v.shape[-1]
    c = math.gcd(t, CHUNK)
    nc = t // c

    def to_chunks(a):
        return a.astype(jnp.float32).reshape(n, nc, c, h, a.shape[-1]).transpose(1, 0, 3, 2, 4)

    qc, kc, vc, gc = to_chunks(q), to_chunks(k), to_chunks(v), to_chunks(logf)
    causal = jnp.tril(jnp.ones((c, c), dtype=bool))[:, :, None]

    def step(s, inp):
        qb, kb, vb, gb = inp
        g = jnp.cumsum(gb, axis=2)
        diff = g[:, :, :, None, :] - g[:, :, None, :, :]
        decay = jnp.exp(jnp.where(causal, diff, -jnp.inf))
        scores = jnp.einsum('nhtd,nhsd,nhtsd->nhts', qb, kb, decay)
        o = (jnp.einsum('nhts,nhsv->nhtv', scores, vb)
             + jnp.einsum('nhtd,nhdv->nhtv', qb * jnp.exp(g), s))
        g_last = g[:, :, -1:, :]
        s_new = (jnp.exp(g_last[:, :, 0, :])[..., None] * s
                 + jnp.einsum('nhsd,nhsv->nhdv', kb * jnp.exp(g_last - g), vb))
        return s_new, o

    s_fin, oc = lax.scan(step, s0.astype(jnp.float32), (qc, kc, vc, gc))
    o = oc.transpose(1, 0, 3, 2, 4).reshape(n, t, h, dv)
    return o, s_fin


def causal_dwconv(u, buf, w, b):
    full = jnp.concatenate([buf.astype(u.dtype), u], axis=1)
    y = lax.conv_general_dilated(full, w.astype(u.dtype)[:, None, :], (1,), 'VALID',
                                 dimension_numbers=('NWC', 'WIO', 'NWC'),
                                 feature_group_count=u.shape[-1])
    return y + b.astype(u.dtype), full[:, -(CONV_WIDTH - 1):]


def mixer_layer(x, s0, conv_buf, norm_g, w_in, lb, a_norm_g, b_glu, conv_w, conv_b, ln_g, ln_b, w_out):
    n, t, _ = x.shape
    h = rmsnorm(x, norm_g)
    proj = h @ w_in.astype(h.dtype)
    s1 = A_QK
    s2 = s1 + A_QK
    s3 = s2 + A_WIDTH
    s4 = s3 + A_WIDTH
    s5 = s4 + 2 * B_WIDTH
    q, fp, i, za, glu, zb = jnp.split(proj, [s1, s2, s3, s4, s5], axis=-1)
    q = jax.nn.silu(q)
    lbf = lb.astype(jnp.float32)
    f = lbf + (1.0 - lbf) * jax.nn.sigmoid(fp.astype(jnp.float32))
    k = 1.0 - f
    logf = jnp.log(f)
    heads = lambda a, d: a.reshape(n, t, A_HEADS, d)
    o_a, s_new = hgrn2_chunked(heads(q, A_KEY_DIM), heads(k, A_KEY_DIM),
                               heads(i, A_HEAD_DIM), heads(logf, A_KEY_DIM), s0)
    o_a = rmsnorm(o_a, a_norm_g).astype(x.dtype).reshape(n, t, A_WIDTH) * jax.nn.silu(za)
    glu = glu + b_glu.astype(glu.dtype)
    u = glu[..., :B_WIDTH] * jax.nn.sigmoid(glu[..., B_WIDTH:])
    cv, buf_new = causal_dwconv(u, conv_buf, conv_w, conv_b)
    o_b = jax.nn.silu(layernorm(cv, ln_g, ln_b)) * jax.nn.silu(zb)
    out = jnp.concatenate([o_a, o_b], axis=-1) @ w_out.astype(x.dtype)
    return x + out, s_new, buf_new


def setup_inputs(seed: int = 0) -> dict:
    key = jax.random.key(seed)
    ks = jax.random.split(key, 16)
    f32 = jnp.float32
    return {
        "x_prompt": jax.random.normal(ks[0], (BATCH, SEQ, D_MODEL), f32),
        "x_sample": jax.random.normal(ks[1], (DEC_BATCH, DEC_SEQ, D_MODEL), f32),
        "state_hgrn": 0.5 * jax.random.normal(ks[2], (DEPTH, DEC_BATCH, A_HEADS, A_KEY_DIM, A_HEAD_DIM), f32),
        "state_conv": 0.5 * jax.random.normal(ks[3], (DEPTH, DEC_BATCH, CONV_WIDTH - 1, B_WIDTH), f32),
        "norm_in_g": 1.0 + 0.05 * jax.random.normal(ks[4], (DEPTH, D_MODEL), f32),
        "w_in": jax.random.normal(ks[5], (DEPTH, D_MODEL, IN_WIDTH), f32) * D_MODEL ** -0.5,
        "lb_logits": 0.5 * jax.random.normal(ks[6], (DEPTH + 1, A_QK), f32),
        "hgrn_norm_g": 1.0 + 0.05 * jax.random.normal(ks[7], (DEPTH, A_HEAD_DIM), f32),
        "b_glu": 0.02 * jax.random.normal(ks[8], (DEPTH, 2 * B_WIDTH), f32),
        "conv_w": jax.random.normal(ks[9], (DEPTH, CONV_WIDTH, B_WIDTH), f32) * CONV_WIDTH ** -0.5,
        "conv_b": 0.02 * jax.random.normal(ks[10], (DEPTH, B_WIDTH), f32),
        "ln_g": 1.0 + 0.05 * jax.random.normal(ks[11], (DEPTH, B_WIDTH), f32),
        "ln_b": 0.02 * jax.random.normal(ks[12], (DEPTH, B_WIDTH), f32),
        "w_out": jax.random.normal(ks[13], (DEPTH, MIX_WIDTH, D_MODEL), f32) * MIX_WIDTH ** -0.5,
        "final_norm_g": 1.0 + 0.05 * jax.random.normal(ks[14], (D_MODEL,), f32),
    }


def reference(x_prompt, x_sample, state_hgrn, state_conv, norm_in_g, w_in, lb_logits, hgrn_norm_g,
              b_glu, conv_w, conv_b, ln_g, ln_b, w_out, final_norm_g):
    lb_all = jnp.cumsum(jax.nn.softmax(lb_logits.astype(jnp.float32), axis=0), axis=0)
    hp, hs = x_prompt, x_sample
    sp_list, cp_list, ss_list, cs_list = [], [], [], []
    for l in range(DEPTH):
        lw = (norm_in_g[l], w_in[l], lb_all[l], hgrn_norm_g[l], b_glu[l], conv_w[l], conv_b[l],
              ln_g[l], ln_b[l], w_out[l])
        s0p = jnp.zeros((hp.shape[0], A_HEADS, A_KEY_DIM, A_HEAD_DIM), jnp.float32)
        c0p = jnp.zeros((hp.shape[0], CONV_WIDTH - 1, B_WIDTH), hp.dtype)
        hp, sp, cp = mixer_layer(hp, s0p, c0p, *lw)
        hs, ss, cs = mixer_layer(hs, state_hgrn[l], state_conv[l], *lw)
        sp_list.append(sp)
        cp_list.append(cp)
        ss_list.append(ss)
        cs_list.append(cs)
    y_prompt = rmsnorm(hp, final_norm_g)
    y_sample = rmsnorm(hs, final_norm_g)
    new_state_hgrn_prompt = jnp.stack(sp_list)
    new_state_conv_prompt = jnp.stack(cp_list)
    new_state_hgrn_sample = jnp.stack(ss_list)
    new_state_conv_sample = jnp.stack(cs_list)
    return (y_prompt, y_sample, new_state_hgrn_prompt, new_state_conv_prompt, new_state_hgrn_sample, new_state_conv_sample)
```

```python
import functools

import numpy as np
import jax
import jax.numpy as jnp
from jax import lax
from jax.experimental import pallas as pl
from jax.experimental.pallas import tpu as pltpu

D_MODEL = 1024
A_HEADS = 4
HEAD_DIM = 128
A_WIDTH = A_HEADS * HEAD_DIM
B_WIDTH = 512
CONV_WIDTH = 31
EPS = 1e-6
IN_WIDTH = 4 * A_WIDTH + 3 * B_WIDTH

LANES = 128
SUBLANES = 8
CHUNK = SUBLANES * SUBLANES
HIST = 32
VMEM_LIMIT_BYTES = 48 * 1024 * 1024

_F32 = jnp.float32
_BF16 = jnp.bfloat16


def _sigmoid(x):
    return 1.0 / (1.0 + jnp.exp(-x))


def _silu(x):
    return x * _sigmoid(x)


def _dot(a, b):
    return jnp.dot(a, b, preferred_element_type=_F32)


def _dot_nt(a, b):
    return lax.dot_general(a, b, (((1,), (1,)), ((), ())), preferred_element_type=_F32)


def _dot_tn(a, b):
    return lax.dot_general(a, b, (((0,), (0,)), ((), ())), preferred_element_type=_F32)


def _level_masks(levels, n_res, n_rows, same_group=None):
    r = np.arange(n_res * n_rows)
    t = n_res * (r % n_rows) + r // n_rows
    tt, ss = t[:, None], t[None, :]
    masks = [tt == ss]
    for b in levels:
        masks.append(((tt // b) == (ss // b) + 1) & ((ss // b) % 2 == 0))
    masks = np.stack(masks)
    want = (ss <= tt) if same_group is None else ((ss <= tt) & (tt // same_group == ss // same_group))
    assert (masks.sum(0) == want).all()
    return masks.astype(np.float32)


def _rows(x, r):
    return jnp.broadcast_to(x[r:r + 1, :], x.shape)


def _hgrn_chunk(q, f, v, masks_ref, st, coarse):
    n_res = len(q)
    k = [1.0 - fi for fi in f]
    lf = [jnp.log(fi) for fi in f]
    g_in = [lf[0]]
    for i in range(1, n_res):
        g_in.append(g_in[-1] + lf[i])
    tot = g_in[-1]
    if coarse:
        row = lax.broadcasted_iota(jnp.int32, tot.shape, 0)
        pref = tot
        for sh in (1, 2, 4):
            pref = pref + jnp.where(row >= sh, pltpu.roll(pref, sh, 0), 0.0)
        before = pref - tot
        gc = [before + gi for gi in g_in]
        end_all = _rows(pref, SUBLANES - 1)
    else:
        before = jnp.zeros_like(tot)
        gc = g_in
        pref = tot
        end_all = tot

    def cat(parts):
        return jnp.concatenate(parts, axis=0).astype(_BF16)

    q_all, k_all, v_all = cat(q), cat(k), cat(v)
    s_tot = masks_ref[0] * _dot_nt(q_all, k_all)
    level = 1
    b = 1
    while b < n_res:
        qt, kt = [], []
        for i in range(n_res):
            bs = (i // b) * b
            g_start = gc[bs - 1] if bs >= 1 else before
            g_end = gc[bs + b - 1]
            qt.append(q[i] * jnp.exp(gc[i] - g_start))
            kt.append(k[i] * jnp.exp(g_end - gc[i]))
        s_tot = s_tot + masks_ref[level] * _dot_nt(cat(qt), cat(kt))
        level += 1
        b *= 2
    if coarse:
        for gsz in (1, 2, 4):
            if gsz == 1:
                g_start, g_end = before, pref
            elif gsz == 2:
                g_start = jnp.where(row % 2 == 0, before, pltpu.roll(before, 1, 0))
                g_end = jnp.where(row % 2 == 1, pref, pltpu.roll(pref, SUBLANES - 1, 0))
            else:
                g_start = jnp.where(row < 4, _rows(before, 0), _rows(before, 4))
                g_end = jnp.where(row < 4, _rows(pref, 3), _rows(pref, 7))
            qt = [q[i] * jnp.exp(gc[i] - g_start) for i in range(n_res)]
            kt = [k[i] * jnp.exp(g_end - gc[i]) for i in range(n_res)]
            s_tot = s_tot + masks_ref[level] * _dot_nt(cat(qt), cat(kt))
            level += 1
    o = _dot(s_tot.astype(_BF16), v_all)
    qc = cat([q[i] * jnp.exp(gc[i]) for i in range(n_res)])
    kc = cat([k[i] * jnp.exp(end_all - gc[i]) for i in range(n_res)])
    if coarse:
        o = o + _dot_nt(qc, st.astype(_BF16))
        decay = jnp.exp(pref[SUBLANES - 1:SUBLANES, :])
    else:
        decay = jnp.exp(tot)
    return o, qc, kc, v_all, decay


def _head_out(o, ag, gate):
    ms = jnp.mean(o * o, axis=-1, keepdims=True)
    return o * lax.rsqrt(ms + EPS) * ag * gate


def _project_in(h, win_ref, lb_ref, bglu_ref, pa_ref, n_rows):
    two = 2 * LANES
    for cg in range(2 * A_HEADS):
        p = _dot(h, win_ref[:, cg * two:(cg + 1) * two])
        arr = cg // 2
        if arr == 0 or arr == 3:
            p = _silu(p)
        elif arr == 1:
            lb = lb_ref[:, (cg % 2) * two:(cg % 2 + 1) * two]
            p = lb + (1.0 - lb) * _sigmoid(p)
        pa_ref[2 * cg] = p[:, :LANES]
        pa_ref[2 * cg + 1] = p[:, LANES:]
    off = 4 * A_WIDTH
    glu_a = _dot(h, win_ref[:, off:off + B_WIDTH]) + bglu_ref[:, :B_WIDTH]
    glu_b = _dot(h, win_ref[:, off + B_WIDTH:off + 2 * B_WIDTH]) + bglu_ref[:, B_WIDTH:]
    u = glu_a * _sigmoid(glu_b)
    zb = _dot(h, win_ref[:, off + 2 * B_WIDTH:off + 3 * B_WIDTH])
    return u, _silu(zb)


def _rmsnorm(x, g):
    return x * lax.rsqrt(jnp.mean(x * x, axis=-1, keepdims=True) + EPS) * g


def _group_b_out(cv, lng_ref, lnb_ref, gate_b):
    mu = jnp.mean(cv, axis=-1, keepdims=True)
    d = cv - mu
    var = jnp.mean(d * d, axis=-1, keepdims=True)
    ln = d * lax.rsqrt(var + EPS) * lng_ref[...] + lnb_ref[...]
    return _silu(ln) * gate_b


def _prompt_kernel(x_ref, normg_ref, win_ref, lb_ref, ag_ref, bglu_ref, convw_ref, convb_ref,
                   lng_ref, lnb_ref, wout_ref, fg_ref, masks_ref,
                   y_ref, st_out_ref, cs_out_ref,
                   pa_ref, uh_ref, oa_ref, st_ref, *, tb):
    t = pl.program_id(1)
    nt = pl.num_programs(1)

    @pl.when(t == 0)
    def _():
        st_ref[...] = jnp.zeros_like(st_ref)
        uh_ref[0:HIST, :] = jnp.zeros((HIST, B_WIDTH), _F32)

    x = x_ref[...]
    h = _rmsnorm(x, normg_ref[...]).astype(_BF16)
    u, gate_b = _project_in(h, win_ref, lb_ref, bglu_ref, pa_ref, tb)

    uh_ref[HIST:HIST + tb, :] = u
    first = HIST - (CONV_WIDTH - 1)
    cv = jnp.broadcast_to(convb_ref[...], (tb, B_WIDTH))
    for j in range(CONV_WIDTH):
        cv = cv + convw_ref[j:j + 1, :] * uh_ref[first + j:first + j + tb, :]
    o_b = _group_b_out(cv, lng_ref, lnb_ref, gate_b)

    @pl.when(t == nt - 1)
    def _():
        cs_out_ref[...] = uh_ref[HIST + tb - (CONV_WIDTH - 1):HIST + tb, :]

    uh_ref[0:HIST, :] = uh_ref[tb:tb + HIST, :]

    def chunk_body(c, carry):
        base = pl.multiple_of(c * CHUNK, CHUNK)
        for hd in range(A_HEADS):
            def slabs(arr):
                return [pa_ref[arr * A_HEADS + hd, pl.ds(base + i, SUBLANES, stride=SUBLANES), :]
                        for i in range(SUBLANES)]
            q, f, v, gate = slabs(0), slabs(1), slabs(2), slabs(3)
            st = st_ref[hd]
            o, _, kc, v_all, decay = _hgrn_chunk(q, f, v, masks_ref, st, True)
            st_ref[hd] = st * decay + _dot_tn(v_all, kc)
            on = _head_out(o, ag_ref[...], jnp.concatenate(gate, axis=0))
            for i in range(SUBLANES):
                oa_ref[hd, pl.ds(base + i, SUBLANES, stride=SUBLANES), :] = on[i * SUBLANES:(i + 1) * SUBLANES, :]
        return carry

    lax.fori_loop(0, tb // CHUNK, chunk_body, 0)

    @pl.when(t == nt - 1)
    def _():
        for hd in range(A_HEADS):
            st_out_ref[hd] = st_ref[hd].T

    cat = jnp.concatenate([oa_ref[hd] for hd in range(A_HEADS)] + [o_b], axis=-1).astype(_BF16)
    out = _dot(cat, wout_ref[...])
    y_ref[...] = _rmsnorm(x + out, fg_ref[...])


def _full(shape):
    nd = len(shape)
    return pl.BlockSpec(shape, lambda *_: (0,) * nd)


def _prompt_call(x, weights, tb):
    n, t, _ = x.shape
    assert t % tb == 0 and tb % CHUNK == 0
    masks = jnp.asarray(_level_masks((1, 2, 4, 8, 16, 32), SUBLANES, SUBLANES))
    kern = functools.partial(_prompt_kernel, tb=tb)
    w_specs = [_full(w.shape) for w in weights]
    return pl.pallas_call(
        kern,
        grid=(n, t // tb),
        in_specs=[pl.BlockSpec((None, tb, D_MODEL), lambda i, j: (i, j, 0))] + w_specs + [_full(masks.shape)],
        out_specs=[
            pl.BlockSpec((None, tb, D_MODEL), lambda i, j: (i, j, 0)),
            pl.BlockSpec((None, A_HEADS, HEAD_DIM, HEAD_DIM), lambda i, j: (i, 0, 0, 0)),
            pl.BlockSpec((None, CONV_WIDTH - 1, B_WIDTH), lambda i, j: (i, 0, 0)),
        ],
        out_shape=[
            jax.ShapeDtypeStruct((n, t, D_MODEL), _F32),
            jax.ShapeDtypeStruct((n, A_HEADS, HEAD_DIM, HEAD_DIM), _F32),
            jax.ShapeDtypeStruct((n, CONV_WIDTH - 1, B_WIDTH), _F32),
        ],
        scratch_shapes=[
            pltpu.VMEM((4 * A_HEADS, tb, LANES), _F32),
            pltpu.VMEM((HIST + tb, B_WIDTH), _F32),
            pltpu.VMEM((A_HEADS, tb, LANES), _F32),
            pltpu.VMEM((A_HEADS, HEAD_DIM, HEAD_DIM), _F32),
        ],
        compiler_params=pltpu.CompilerParams(
            dimension_semantics=("arbitrary", "arbitrary"),
            vmem_limit_bytes=VMEM_LIMIT_BYTES,
        ),
        name="hymba_prompt",
    )(x, *weights, masks)


def _decode_kernel(x_ref, s0_ref, c0_ref, normg_ref, win_ref, lb_ref, ag_ref, bglu_ref, convw_ref, convb_ref,
                   lng_ref, lnb_ref, wout_ref, fg_ref, masks_ref,
                   y_ref, s_out_ref, cs_out_ref,
                   pa_ref, oa_ref, *, t_dec):
    rows = SUBLANES * t_dec
    x = x_ref[...]
    h = _rmsnorm(x, normg_ref[...]).astype(_BF16)
    u, gate_b = _project_in(h, win_ref, lb_ref, bglu_ref, pa_ref, rows)
    pad = jnp.zeros((2 * SUBLANES - t_dec, LANES), _F32)

    cvs = []
    for s in range(SUBLANES):
        full = jnp.concatenate([c0_ref[s], u[s * t_dec:(s + 1) * t_dec, :]], axis=0)
        cs_out_ref[s] = full[t_dec:, :]
        cv = jnp.broadcast_to(convb_ref[...], (t_dec, B_WIDTH))
        for j in range(CONV_WIDTH):
            cv = cv + convw_ref[j:j + 1, :] * full[j:j + t_dec, :]
        cvs.append(cv)
    o_b = _group_b_out(jnp.concatenate(cvs, axis=0), lng_ref, lnb_ref, gate_b)

    for hd in range(A_HEADS):
        def slabs(arr):
            return [pa_ref[arr * A_HEADS + hd, pl.ds(i, SUBLANES, stride=t_dec), :] for i in range(t_dec)]
        q, f, v, gate = slabs(0), slabs(1), slabs(2), slabs(3)
        o, qc, kc, v_all, decay = _hgrn_chunk(q, f, v, masks_ref, None, False)
        qc = qc.astype(_F32)
        kc = kc.astype(_F32)
        vf = v_all.astype(_F32)
        o_rows = []
        for s in range(SUBLANES):
            s0 = s0_ref[s, hd]
            pick = lambda a: jnp.concatenate([a[i * SUBLANES + s:i * SUBLANES + s + 1, :] for i in range(t_dec)], axis=0)
            q_s, k_s, v_s = pick(qc), pick(kc), pick(vf)
            o_rows.append(_dot(jnp.concatenate([q_s, pad], axis=0).astype(_BF16), s0.astype(_BF16))[:t_dec, :])
            upd = _dot_tn(jnp.concatenate([k_s, pad], axis=0).astype(_BF16),
                          jnp.concatenate([v_s, pad], axis=0).astype(_BF16))
            dcol = jnp.transpose(jnp.broadcast_to(decay[s:s + 1, :], (SUBLANES, LANES)))[:, 0:1]
            s_out_ref[s, hd] = s0 * dcol + upd
        o_seq = jnp.concatenate(o_rows, axis=0)
        for i in range(t_dec):
            oa_ref[hd, pl.ds(i, SUBLANES, stride=t_dec), :] = o[i * SUBLANES:(i + 1) * SUBLANES, :]
        o_nat = oa_ref[hd] + o_seq
        gate_nat = pa_ref[3 * A_HEADS + hd]
        oa_ref[hd] = _head_out(o_nat, ag_ref[...], gate_nat)

    cat = jnp.concatenate([oa_ref[hd] for hd in range(A_HEADS)] + [o_b], axis=-1).astype(_BF16)
    out = _dot(cat, wout_ref[...])
    y_ref[...] = _rmsnorm(x + out, fg_ref[...])


def _decode_call(x, s0, c0, weights):
    n, t_dec, _ = x.shape
    assert n % SUBLANES == 0 and t_dec == 4
    x = x.reshape(n * t_dec, D_MODEL)
    masks = jnp.asarray(_level_masks((1, 2), t_dec, SUBLANES, same_group=t_dec))
    kern = functools.partial(_decode_kernel, t_dec=t_dec)
    w_specs = [_full(w.shape) for w in weights]
    rows = SUBLANES * t_dec
    y, s_new, c_new = pl.pallas_call(
        kern,
        grid=(n // SUBLANES,),
        in_specs=[
            pl.BlockSpec((rows, D_MODEL), lambda i: (i, 0)),
            pl.BlockSpec((SUBLANES, A_HEADS, HEAD_DIM, HEAD_DIM), lambda i: (i, 0, 0, 0)),
            pl.BlockSpec((SUBLANES, CONV_WIDTH - 1, B_WIDTH), lambda i: (i, 0, 0)),
        ] + w_specs + [_full(masks.shape)],
        out_specs=[
            pl.BlockSpec((rows, D_MODEL), lambda i: (i, 0)),
            pl.BlockSpec((SUBLANES, A_HEADS, HEAD_DIM, HEAD_DIM), lambda i: (i, 0, 0, 0)),
            pl.BlockSpec((SUBLANES, CONV_WIDTH - 1, B_WIDTH), lambda i: (i, 0, 0)),
        ],
        out_shape=[
            jax.ShapeDtypeStruct((n * t_dec, D_MODEL), _F32),
            jax.ShapeDtypeStruct((n, A_HEADS, HEAD_DIM, HEAD_DIM), _F32),
            jax.ShapeDtypeStruct((n, CONV_WIDTH - 1, B_WIDTH), _F32),
        ],
        scratch_shapes=[
            pltpu.VMEM((4 * A_HEADS, rows, LANES), _F32),
            pltpu.VMEM((A_HEADS, rows, LANES), _F32),
        ],
        compiler_params=pltpu.CompilerParams(
            dimension_semantics=("arbitrary",),
            vmem_limit_bytes=VMEM_LIMIT_BYTES,
        ),
        name="hymba_decode",
    )(x, s0, c0, *weights, masks)
    return y.reshape(n, t_dec, D_MODEL), s_new, c_new


def _prepare_weights(norm_in_g, w_in, lb_logits, hgrn_norm_g, b_glu, conv_w, conv_b, ln_g, ln_b, w_out, final_norm_g, layer):
    lb = jnp.cumsum(jax.nn.softmax(lb_logits.astype(_F32), axis=0), axis=0)[layer]
    row = lambda a: a.astype(_F32).reshape(1, -1)
    return (
        row(norm_in_g[layer]),
        w_in[layer].astype(_BF16),
        row(lb),
        row(hgrn_norm_g[layer]),
        row(b_glu[layer]),
        conv_w[layer].astype(_F32),
        row(conv_b[layer]),
        row(ln_g[layer]),
        row(ln_b[layer]),
        w_out[layer].astype(_BF16),
        row(final_norm_g),
    )


def kernel(x_prompt, x_sample, state_hgrn, state_conv, norm_in_g, w_in, lb_logits, hgrn_norm_g, b_glu, conv_w,
           conv_b, ln_g, ln_b, w_out, final_norm_g):
    depth = w_in.shape[0]
    assert depth == 1, "single mixer layer: the final norm is fused into the layer kernel"
    weights = _prepare_weights(norm_in_g, w_in, lb_logits, hgrn_norm_g, b_glu, conv_w, conv_b, ln_g, ln_b, w_out,
                               final_norm_g, 0)
    y_p, s_p, c_p = _prompt_call(x_prompt, weights, tb=256)
    y_s, s_s, c_s = _decode_call(x_sample, state_hgrn[0], state_conv[0], weights)
    return (y_p, y_s, s_p[None], c_p[None], s_s[None], c_s[None])
```

```python
import functools
import math

import numpy as np
import jax
import jax.numpy as jnp
from jax import lax
from jax.experimental import pallas as pl
from jax.experimental.pallas import tpu as pltpu

D_MODEL = 1024
A_HEADS = 4
HEAD_DIM = 128
A_WIDTH = A_HEADS * HEAD_DIM
B_WIDTH = 512
CONV_WIDTH = 31
EPS = 1e-6
IN_WIDTH = 4 * A_WIDTH + 3 * B_WIDTH
GLU_OFF = 4 * A_WIDTH
ZB_OFF = GLU_OFF + 2 * B_WIDTH

LANES = 128
SUBLANES = 8
MXU_N = 256
CHUNK = SUBLANES * SUBLANES
ROW_TILE = 128
UNROLL = 8
CHUNK_UNROLL = 4
CHUNK_LAG = 4
CONV_BUFS = 3
DEC_GROUPS = 2
HIST = 32
FIRST = HIST - (CONV_WIDTH - 1)
VMEM_LIMIT_BYTES = 56 * 1024 * 1024
LOG2E = math.log2(math.e)

_F32 = jnp.float32
_BF16 = jnp.bfloat16


def _sigmoid(x):
    return 1.0 / (1.0 + jnp.exp(-x))


def _silu(x):
    return x * _sigmoid(x)


def _dot(a, b):
    return jnp.dot(a, b, preferred_element_type=_F32)


def _dot_nt(a, b):
    return lax.dot_general(a, b, (((1,), (1,)), ((), ())), preferred_element_type=_F32)


def _dot_tn(a, b):
    return lax.dot_general(a, b, (((0,), (0,)), ((), ())), preferred_element_type=_F32)


def _level_masks(levels, n_res, n_rows, same_group=None):
    r = np.arange(n_res * n_rows)
    t = n_res * (r % n_rows) + r // n_rows
    tt, ss = t[:, None], t[None, :]
    masks = [tt == ss]
    for b in levels:
        masks.append(((tt // b) == (ss // b) + 1) & ((ss // b) % 2 == 0))
    masks = np.stack(masks)
    want = (ss <= tt) if same_group is None else ((ss <= tt) & (tt // same_group == ss // same_group))
    assert (masks.sum(0) == want).all()
    return masks.astype(np.float32)


def _rows(x, r):
    return jnp.broadcast_to(x[r:r + 1, :], x.shape)


def _hgrn_stages(q, f, v, masks_ref, get_st, coarse, done):
    n_res = len(q)
    every = range(n_res)
    k = [1.0 - fi for fi in f]
    lf = [jnp.log(fi) * LOG2E for fi in f]
    g_in = [lf[0]]
    for i in range(1, n_res):
        g_in.append(g_in[-1] + lf[i])
    tot = g_in[-1]
    zero = jnp.zeros_like(tot)

    def cat(parts):
        return jnp.concatenate(parts, axis=0).astype(_BF16)

    def slab_rows(x, i):
        return x[i * SUBLANES:(i + 1) * SUBLANES, :]

    v_all = cat(v)
    s_rows = [None] * n_res
    pending = []

    def fold():
        level, rows, s_l = pending.pop(0)
        for n, i in enumerate(rows):
            term = masks_ref[level, i * SUBLANES:(i + 1) * SUBLANES, :] * slab_rows(s_l, n)
            s_rows[i] = term if s_rows[i] is None else s_rows[i] + term

    def step(rows, qt, kt):
        pending.append((step.level, rows, _dot_nt(cat(qt), cat(kt))))
        step.level += 1
    step.level = 0

    step(every, q, k)
    yield
    b = 1
    while b < n_res:
        odd = [i for i in every if (i // b) % 2 == 1]
        qt, kt = [], []
        for i in every:
            bs = (i // b) * b
            if i in odd:
                qt.append(q[i] * jnp.exp2(g_in[i] - g_in[bs - 1]))
                kt.append(zero)
            else:
                be = bs + b - 1
                kt.append(k[i] * jnp.exp2(g_in[be] - g_in[i]) if i != be else k[i])
        step(odd, qt, kt)
        yield
        fold()
        b *= 2
    qe = [q[i] * jnp.exp2(g_in[i]) for i in every]
    ke = [k[i] * jnp.exp2(tot - g_in[i]) for i in range(n_res - 1)] + [k[-1]]
    if coarse:
        row = lax.broadcasted_iota(jnp.int32, tot.shape, 0)
        pref = tot
        for sh in (1, 2, 4):
            pref = pref + jnp.where(row >= sh, pltpu.roll(pref, sh, 0), 0.0)
        before = pref - tot
        end_all = _rows(pref, SUBLANES - 1)
        for gsz in (1, 2, 4):
            if gsz == 1:
                qt, kt = qe, ke
            else:
                if gsz == 2:
                    g_start = jnp.where(row % 2 == 0, before, pltpu.roll(before, 1, 0))
                    g_end = jnp.where(row % 2 == 1, pref, pltpu.roll(pref, SUBLANES - 1, 0))
                else:
                    g_start = jnp.where(row < 4, _rows(before, 0), _rows(before, 4))
                    g_end = jnp.where(row < 4, _rows(pref, 3), _rows(pref, 7))
                fq = jnp.exp2(before - g_start)
                fk = jnp.exp2(g_end - pref)
                qt = [x * fq for x in qe]
                kt = [x * fk for x in ke]
            step(every, qt, kt)
            yield
            fold()
        fq = jnp.exp2(before)
        fk = jnp.exp2(end_all - pref)
        qc = cat([x * fq for x in qe])
        kc = cat([x * fk for x in ke])
        o_st = _dot_nt(qc, get_st().astype(_BF16))
        decay = jnp.exp2(pref[SUBLANES - 1:SUBLANES, :])
        yield
    else:
        qc, kc = cat(qe), cat(ke)
        o_st = None
        decay = jnp.exp2(tot)
    while pending:
        fold()
    o = _dot(cat(s_rows), v_all)
    yield
    done(o if o_st is None else o + o_st, qc, kc, v_all, decay)


def _round_robin(starts):
    waiting = list(starts)
    live = []
    rnd = 0
    while waiting or live:
        if waiting and rnd % CHUNK_LAG == 0:
            live += waiting.pop(0)
        for g in list(live):
            try:
                next(g)
            except StopIteration:
                live.remove(g)
        rnd += 1


def _aligned(x, m):
    return x if isinstance(x, int) else pl.multiple_of(x, m)


def _loop(trips, body):
    if trips == 1:
        body(0, 0)
    else:
        lax.fori_loop(0, trips, body, 0)


def _head_out(o, ag, gate):
    ms = jnp.mean(o * o, axis=-1, keepdims=True)
    return o * lax.rsqrt(ms + EPS) * ag * gate


def _project_a(h, win_ref, lb_ref, store, cg):
    p = _dot(h(), win_ref[:, cg * MXU_N:(cg + 1) * MXU_N])
    arr = cg // 2
    if arr == 0 or arr == 3:
        p = _silu(p)
    elif arr == 1:
        lb = lb_ref[:, (cg % 2) * MXU_N:(cg % 2 + 1) * MXU_N]
        p = lb + (1.0 - lb) * _sigmoid(p)
    store(2 * cg, p[:, :LANES])
    store(2 * cg + 1, p[:, LANES:])


def _project_b(h, win_ref, bglu_ref, half):
    c0 = half * MXU_N
    glu_a = _dot(h(), win_ref[:, GLU_OFF + c0:GLU_OFF + c0 + MXU_N]) + bglu_ref[:, c0:c0 + MXU_N]
    glu_b = (_dot(h(), win_ref[:, GLU_OFF + B_WIDTH + c0:GLU_OFF + B_WIDTH + c0 + MXU_N])
             + bglu_ref[:, B_WIDTH + c0:B_WIDTH + c0 + MXU_N])
    zb = _dot(h(), win_ref[:, ZB_OFF + c0:ZB_OFF + c0 + MXU_N])
    return glu_a * _sigmoid(glu_b), _silu(zb)


def _rmsnorm(x, g):
    return x * lax.rsqrt(jnp.mean(x * x, axis=-1, keepdims=True) + EPS) * g


def _group_b_out(cv, lng_ref, lnb_ref, gate_b):
    mu = jnp.mean(cv, axis=-1, keepdims=True)
    d = cv - mu
    var = jnp.mean(d * d, axis=-1, keepdims=True)
    ln = d * lax.rsqrt(var + EPS) * lng_ref[...] + lnb_ref[...]
    return _silu(ln) * gate_b


def _conv_stages(uh_ref, convw_ref, convb_ref, lng_ref, lnb_ref, gb_ref, cv_ref, cat_ref, base, n):
    for ls in range(B_WIDTH // LANES):
        lanes = slice(ls * LANES, (ls + 1) * LANES)
        acc = jnp.broadcast_to(convb_ref[:, lanes], (n, LANES))
        for off in range(SUBLANES):
            taps = [j for j in range(CONV_WIDTH) if (FIRST + j) % SUBLANES == off]
            part = None
            for j in taps:
                start = _aligned(base + (FIRST + j - off), SUBLANES)
                win = uh_ref[pl.ds(start, n + (SUBLANES if off else 0)), lanes]
                term = convw_ref[j:j + 1, lanes] * win
                part = term if part is None else part + term
            acc = acc + (part[off:off + n, :] if off else part)
            if off == SUBLANES // 2 - 1:
                yield
        cv_ref[:, lanes] = acc
        yield
    rows = pl.ds(_aligned(base, n), n)
    o_b = _group_b_out(cv_ref[...], lng_ref, lnb_ref, gb_ref[rows, :])
    cat_ref[rows, A_WIDTH:] = o_b.astype(_BF16)


def _prompt_kernel(x_ref, normg_ref, win_ref, lb_ref, ag_ref, bglu_ref, convw_ref, convb_ref,
                   lng_ref, lnb_ref, wout_ref, fg_ref, masks_ref,
                   y_ref, st_out_ref, cs_out_ref,
                   pa_ref, uh_ref, gb_ref, cv_ref, oa_ref, cat_ref, st_ref, *, tb):
    t = pl.program_id(1)
    nt = pl.num_programs(1)

    @pl.when(t == 0)
    def _():
        st_ref[...] = jnp.zeros_like(st_ref)
        uh_ref[...] = jnp.zeros_like(uh_ref)

    def front_tile(r0):
        rows = pl.ds(r0, ROW_TILE)
        h = _rmsnorm(x_ref[rows, :], normg_ref[...]).astype(_BF16)
        for half in range(B_WIDTH // MXU_N):
            u, gb = _project_b(lambda: h, win_ref, bglu_ref, half)
            uh_ref[pl.ds(_aligned(HIST + r0, HIST), ROW_TILE), half * MXU_N:(half + 1) * MXU_N] = u
            gb_ref[rows, half * MXU_N:(half + 1) * MXU_N] = gb

        def store_a(slab, val):
            pa_ref[slab, rows, :] = val
        for cg in range(2 * A_HEADS):
            _project_a(lambda: h, win_ref, lb_ref, store_a, cg)

    def front_body(p, carry):
        for n in range(UNROLL):
            front_tile(_aligned((UNROLL * p + n) * ROW_TILE, ROW_TILE))
        return carry

    _loop(tb // (UNROLL * ROW_TILE), front_body)

    def chunk(base, slot):
        def head(hd):
            def slabs(arr):
                return [pa_ref[arr * A_HEADS + hd, pl.ds(base + i, SUBLANES, stride=SUBLANES), :]
                        for i in range(SUBLANES)]

            def done(o, qc, kc, v_all, decay):
                st_ref[hd] = st_ref[hd] * decay + _dot_tn(v_all, kc)
                for i in range(SUBLANES):
                    oa_ref[hd, pl.ds(base + i, SUBLANES, stride=SUBLANES), :] = o[i * SUBLANES:(i + 1) * SUBLANES, :]
            return _hgrn_stages(slabs(0), slabs(1), slabs(2), masks_ref, lambda: st_ref[hd], True, done)
        conv = _conv_stages(uh_ref, convw_ref, convb_ref, lng_ref, lnb_ref, gb_ref, cv_ref.at[slot], cat_ref, base, CHUNK)
        return [head(hd) for hd in range(A_HEADS)] + [conv]

    def chunk_body(p, carry):
        _round_robin([chunk(_aligned((CHUNK_UNROLL * p + n) * CHUNK, CHUNK), n % CONV_BUFS) for n in range(CHUNK_UNROLL)])
        return carry

    _loop(tb // (CHUNK_UNROLL * CHUNK), chunk_body)

    @pl.when(t == nt - 1)
    def _():
        cs_out_ref[...] = uh_ref[HIST + tb - (CONV_WIDTH - 1):HIST + tb, :]
        for hd in range(A_HEADS):
            st_out_ref[hd] = st_ref[hd].T

    uh_ref[0:HIST, :] = uh_ref[tb:tb + HIST, :]

    def back_body(p, carry):
        tiles = [pl.ds(_aligned((UNROLL * p + n) * ROW_TILE, ROW_TILE), ROW_TILE) for n in range(UNROLL)]
        for rows in tiles:
            for hd in range(A_HEADS):
                on = _head_out(oa_ref[hd, rows, :], ag_ref[...], pa_ref[3 * A_HEADS + hd, rows, :])
                cat_ref[rows, hd * HEAD_DIM:(hd + 1) * HEAD_DIM] = on.astype(_BF16)
        outs = [_dot(cat_ref[rows, :], wout_ref[...]) for rows in tiles]
        for rows, out in zip(tiles, outs):
            y_ref[rows, :] = _rmsnorm(x_ref[rows, :] + out, fg_ref[...])
        return carry

    _loop(tb // (UNROLL * ROW_TILE), back_body)


def _full(shape):
    nd = len(shape)
    return pl.BlockSpec(shape, lambda *_: (0,) * nd, pipeline_mode=pl.Buffered(1))


def _prompt_call(x, weights, tb):
    n, t, _ = x.shape
    assert t % tb == 0 and tb % (UNROLL * ROW_TILE) == 0 and tb % (CHUNK_UNROLL * CHUNK) == 0
    masks = jnp.asarray(_level_masks((1, 2, 4, 8, 16, 32), SUBLANES, SUBLANES))
    kern = functools.partial(_prompt_kernel, tb=tb)
    w_specs = [_full(w.shape) for w in weights]
    return pl.pallas_call(
        kern,
        grid=(n, t // tb),
        in_specs=[pl.BlockSpec((None, tb, D_MODEL), lambda i, j: (i, j, 0))] + w_specs + [_full(masks.shape)],
        out_specs=[
            pl.BlockSpec((None, tb, D_MODEL), lambda i, j: (i, j, 0)),
            pl.BlockSpec((None, A_HEADS, HEAD_DIM, HEAD_DIM), lambda i, j: (i, 0, 0, 0)),
            pl.BlockSpec((None, CONV_WIDTH - 1, B_WIDTH), lambda i, j: (i, 0, 0)),
        ],
        out_shape=[
            jax.ShapeDtypeStruct((n, t, D_MODEL), _F32),
            jax.ShapeDtypeStruct((n, A_HEADS, HEAD_DIM, HEAD_DIM), _F32),
            jax.ShapeDtypeStruct((n, CONV_WIDTH - 1, B_WIDTH), _F32),
        ],
        scratch_shapes=[
            pltpu.VMEM((4 * A_HEADS, tb, LANES), _F32),
            pltpu.VMEM((HIST + tb, B_WIDTH), _F32),
            pltpu.VMEM((tb, B_WIDTH), _F32),
            pltpu.VMEM((CONV_BUFS, CHUNK, B_WIDTH), _F32),
            pltpu.VMEM((A_HEADS, tb, LANES), _F32),
            pltpu.VMEM((tb, A_WIDTH + B_WIDTH), _BF16),
            pltpu.VMEM((A_HEADS, HEAD_DIM, HEAD_DIM), _F32),
        ],
        compiler_params=pltpu.CompilerParams(
            dimension_semantics=("arbitrary", "arbitrary"),
            vmem_limit_bytes=VMEM_LIMIT_BYTES,
        ),
        name="hymba_prompt",
    )(x, *weights, masks)


def _decode_kernel(x_ref, s0_ref, c0_ref, normg_ref, win_ref, lb_ref, ag_ref, bglu_ref, convw_ref, convb_ref,
                   lng_ref, lnb_ref, wout_ref, fg_ref, masks_ref,
                   y_ref, s_out_ref, cs_out_ref,
                   pa_ref, ub_ref, gb_ref, oa_ref, ob_ref, *, t_dec):
    g = pl.program_id(0)
    n_tok = x_ref.shape[0]
    grp = SUBLANES * t_dec
    n_ls = B_WIDTH // LANES

    @pl.when(g == 0)
    def _():
        def proj_body(r, carry):
            rows = pl.ds(pl.multiple_of(r * ROW_TILE, ROW_TILE), ROW_TILE)
            h = _rmsnorm(x_ref[rows, :], normg_ref[...]).astype(_BF16)
            for half in range(B_WIDTH // MXU_N):
                u, gb = _project_b(lambda: h, win_ref, bglu_ref, half)
                for c in range(MXU_N // LANES):
                    ls = half * (MXU_N // LANES) + c
                    ub_ref[ls, rows, :] = u[:, c * LANES:(c + 1) * LANES]
                    gb_ref[ls, rows, :] = gb[:, c * LANES:(c + 1) * LANES]

            def store_a(slab, val):
                pa_ref[slab, rows, :] = val
            for cg in range(2 * A_HEADS):
                _project_a(lambda: h, win_ref, lb_ref, store_a, cg)
            return carry
        lax.fori_loop(0, n_tok // ROW_TILE, proj_body, 0)

    def group(sub):
        seqs = slice(sub * SUBLANES, (sub + 1) * SUBLANES)
        r0 = pl.multiple_of((g * DEC_GROUPS + sub) * grp, grp)

        def token_slab(ref, idx, t):
            return ref[idx, pl.ds(r0 + t, SUBLANES, stride=t_dec), :]

        u_tok = [jnp.concatenate([token_slab(ub_ref, ls, t) for ls in range(n_ls)], axis=-1) for t in range(t_dec)]
        full = lambda kk: c0_ref[kk, seqs, :] if kk < CONV_WIDTH - 1 else u_tok[kk - (CONV_WIDTH - 1)]
        for kk in range(CONV_WIDTH - 1):
            cs_out_ref[kk, seqs, :] = full(kk + t_dec)
        for t in range(t_dec):
            cv = jnp.broadcast_to(convb_ref[...], (SUBLANES, B_WIDTH))
            for j in range(CONV_WIDTH):
                cv = cv + convw_ref[j:j + 1, :] * full(t + j)
            gate_b = jnp.concatenate([token_slab(gb_ref, ls, t) for ls in range(n_ls)], axis=-1)
            o_b = _group_b_out(cv, lng_ref, lnb_ref, gate_b)
            for ls in range(n_ls):
                ob_ref[ls, pl.ds(r0 + t, SUBLANES, stride=t_dec), :] = o_b[:, ls * LANES:(ls + 1) * LANES]

        row32 = lax.broadcasted_iota(jnp.int32, (grp, LANES), 0) % SUBLANES
        row64 = lax.broadcasted_iota(jnp.int32, (2 * grp, LANES), 0)
        for hd in range(A_HEADS):
            slabs = lambda arr: [token_slab(pa_ref, arr * A_HEADS + hd, t) for t in range(t_dec)]
            res = []
            for _ in _hgrn_stages(slabs(0), slabs(1), slabs(2), masks_ref, None, False, lambda *a: res.extend(a)):
                pass
            o, qc, kc, v_all, decay = res
            qf, kf, vf = qc.astype(_F32), kc.astype(_F32), v_all.astype(_F32)
            e_hi = decay.astype(_BF16).astype(_F32)
            e_mid = (decay - e_hi).astype(_BF16).astype(_F32)
            e_lo = (decay - e_hi - e_mid).astype(_BF16).astype(_F32)
            lhs = jnp.concatenate([kf, e_hi, e_mid, e_lo, jnp.zeros((SUBLANES, LANES), _F32)], axis=0).astype(_BF16)
            v_pad = jnp.concatenate([vf, jnp.zeros((grp, LANES), _F32)], axis=0)
            o_state = None
            for s in range(SUBLANES):
                s0 = s0_ref[sub * SUBLANES + s, hd]
                term = _dot(jnp.where(row32 == s, qf, 0.0).astype(_BF16), s0.astype(_BF16))
                o_state = term if o_state is None else o_state + term
                mine = row64 % SUBLANES == s
                rhs = jnp.concatenate([
                    jnp.where(mine & (row64 < grp), v_pad, 0.0),
                    jnp.where(mine & (row64 >= grp) & (row64 < grp + 3 * SUBLANES), 1.0, 0.0)], axis=-1).astype(_BF16)
                upd = _dot_tn(lhs, rhs)
                s_out_ref[sub * SUBLANES + s, hd] = s0 * upd[:, HEAD_DIM:] + upd[:, :HEAD_DIM]
            gate = jnp.concatenate([token_slab(pa_ref, 3 * A_HEADS + hd, t) for t in range(t_dec)], axis=0)
            on = _head_out(o + o_state, ag_ref[...], gate)
            for t in range(t_dec):
                oa_ref[hd, pl.ds(r0 + t, SUBLANES, stride=t_dec), :] = on[t * SUBLANES:(t + 1) * SUBLANES, :]

    for sub in range(DEC_GROUPS):
        group(sub)

    @pl.when(g == pl.num_programs(0) - 1)
    def _():
        def out_body(r, carry):
            rows = pl.ds(pl.multiple_of(r * ROW_TILE, ROW_TILE), ROW_TILE)
            cat = jnp.concatenate([oa_ref[hd, rows, :] for hd in range(A_HEADS)]
                                  + [ob_ref[ls, rows, :] for ls in range(n_ls)], axis=-1).astype(_BF16)
            out = _dot(cat, wout_ref[...])
            y_ref[rows, :] = _rmsnorm(x_ref[rows, :] + out, fg_ref[...])
            return carry
        lax.fori_loop(0, n_tok // ROW_TILE, out_body, 0)


def _decode_call(x, s0, c0, weights):
    n, t_dec, _ = x.shape
    assert n % (DEC_GROUPS * SUBLANES) == 0 and t_dec == 4 and (n * t_dec) % ROW_TILE == 0
    n_tok = n * t_dec
    x = x.reshape(n_tok, D_MODEL)
    masks = jnp.asarray(_level_masks((1, 2), t_dec, SUBLANES, same_group=t_dec))
    kern = functools.partial(_decode_kernel, t_dec=t_dec)
    w_specs = [_full(w.shape) for w in weights]
    per_step = DEC_GROUPS * SUBLANES
    state_spec = pl.BlockSpec((per_step, A_HEADS, HEAD_DIM, HEAD_DIM), lambda i: (i, 0, 0, 0))
    conv_spec = pl.BlockSpec((CONV_WIDTH - 1, per_step, B_WIDTH), lambda i: (0, i, 0))
    y, s_new, c_new = pl.pallas_call(
        kern,
        grid=(n // per_step,),
        in_specs=[_full((n_tok, D_MODEL)), state_spec, conv_spec] + w_specs + [_full(masks.shape)],
        out_specs=[_full((n_tok, D_MODEL)), state_spec, conv_spec],
        out_shape=[
            jax.ShapeDtypeStruct((n_tok, D_MODEL), _F32),
            jax.ShapeDtypeStruct((n, A_HEADS, HEAD_DIM, HEAD_DIM), _F32),
            jax.ShapeDtypeStruct((CONV_WIDTH - 1, n, B_WIDTH), _F32),
        ],
        scratch_shapes=[
            pltpu.VMEM((4 * A_HEADS, n_tok, LANES), _F32),
            pltpu.VMEM((B_WIDTH // LANES, n_tok, LANES), _F32),
            pltpu.VMEM((B_WIDTH // LANES, n_tok, LANES), _F32),
            pltpu.VMEM((A_HEADS, n_tok, LANES), _F32),
            pltpu.VMEM((B_WIDTH // LANES, n_tok, LANES), _F32),
        ],
        compiler_params=pltpu.CompilerParams(
            dimension_semantics=("arbitrary",),
            vmem_limit_bytes=VMEM_LIMIT_BYTES,
        ),
        name="hymba_decode",
    )(x, s0, c0, *weights, masks)
    return y.reshape(n, t_dec, D_MODEL), s_new, c_new


def _prepare_weights(norm_in_g, w_in, lb_logits, hgrn_norm_g, b_glu, conv_w, conv_b, ln_g, ln_b, w_out, final_norm_g, layer):
    lb = jnp.cumsum(jax.nn.softmax(lb_logits.astype(_F32), axis=0), axis=0)[layer]
    row = lambda a: a.astype(_F32).reshape(1, -1)
    return (
        row(norm_in_g[layer]),
        w_in[layer].astype(_BF16),
        row(lb),
        row(hgrn_norm_g[layer]),
        row(b_glu[layer]),
        conv_w[layer].astype(_F32),
        row(conv_b[layer]),
        row(ln_g[layer]),
        row(ln_b[layer]),
        w_out[layer].astype(_BF16),
        row(final_norm_g),
    )


def kernel(x_prompt, x_sample, state_hgrn, state_conv, norm_in_g, w_in, lb_logits, hgrn_norm_g, b_glu, conv_w,
           conv_b, ln_g, ln_b, w_out, final_norm_g):
    depth = w_in.shape[0]
    assert depth == 1, "single mixer layer: the final norm is fused into the layer kernel"
    weights = _prepare_weights(norm_in_g, w_in, lb_logits, hgrn_norm_g, b_glu, conv_w, conv_b, ln_g, ln_b, w_out,
                               final_norm_g, 0)
    y_p, s_p, c_p = _prompt_call(x_prompt, weights, tb=1024)
    y_s, s_s, c_s = _decode_call(x_sample, state_hgrn[0], jnp.transpose(state_conv[0], (1, 0, 2)), weights)
    return (y_p, y_s, s_p[None], c_p[None], s_s[None], jnp.transpose(c_s, (1, 0, 2))[None])
```

```python
import functools
import math

import numpy as np
import jax
import jax.numpy as jnp
from jax import lax
from jax.experimental import pallas as pl
from jax.experimental.pallas import tpu as pltpu

D_MODEL = 1024
A_HEADS = 4
HEAD_DIM = 128
A_WIDTH = A_HEADS * HEAD_DIM
B_WIDTH = 512
CONV_WIDTH = 31
EPS = 1e-6
IN_WIDTH = 4 * A_WIDTH + 3 * B_WIDTH
GLU_OFF = 4 * A_WIDTH
ZB_OFF = GLU_OFF + 2 * B_WIDTH

LANES = 128
SUBLANES = 8
MXU_N = 256
CHUNK = SUBLANES * SUBLANES
ROW_TILE = 128
UNROLL = 8
CHUNK_UNROLL = 4
CHUNK_LAG = 4
CONV_BUFS = 3
DEC_GROUPS = 2
HIST = 32
FIRST = HIST - (CONV_WIDTH - 1)
VMEM_LIMIT_BYTES = 56 * 1024 * 1024
LOG2E = math.log2(math.e)

_F32 = jnp.float32
_BF16 = jnp.bfloat16


def _sigmoid(x):
    return 1.0 / (1.0 + jnp.exp(-x))


def _silu(x):
    return x * _sigmoid(x)


def _dot(a, b):
    return jnp.dot(a, b, preferred_element_type=_F32)


def _dot_nt(a, b):
    return lax.dot_general(a, b, (((1,), (1,)), ((), ())), preferred_element_type=_F32)


def _dot_tn(a, b):
    return lax.dot_general(a, b, (((0,), (0,)), ((), ())), preferred_element_type=_F32)


def _level_masks(levels, n_res, n_rows, same_group=None):
    r = np.arange(n_res * n_rows)
    t = n_res * (r % n_rows) + r // n_rows
    tt, ss = t[:, None], t[None, :]
    masks = [tt == ss]
    for b in levels:
        masks.append(((tt // b) == (ss // b) + 1) & ((ss // b) % 2 == 0))
    masks = np.stack(masks)
    want = (ss <= tt) if same_group is None else ((ss <= tt) & (tt // same_group == ss // same_group))
    assert (masks.sum(0) == want).all()
    return masks.astype(np.float32)


def _rows(x, r):
    return jnp.broadcast_to(x[r:r + 1, :], x.shape)


def _hgrn_stages(q, f, v, masks_ref, get_st, coarse, done):
    n_res = len(q)
    every = range(n_res)
    k = [1.0 - fi for fi in f]
    lf = [jnp.log(fi) * LOG2E for fi in f]
    g_in = [lf[0]]
    for i in range(1, n_res):
        g_in.append(g_in[-1] + lf[i])
    tot = g_in[-1]
    zero = jnp.zeros_like(tot)

    def cat(parts):
        return jnp.concatenate(parts, axis=0).astype(_BF16)

    def slab_rows(x, i):
        return x[i * SUBLANES:(i + 1) * SUBLANES, :]

    v_all = cat(v)
    s_rows = [None] * n_res
    pending = []

    def fold():
        level, rows, s_l = pending.pop(0)
        for n, i in enumerate(rows):
            term = masks_ref[level, i * SUBLANES:(i + 1) * SUBLANES, :] * slab_rows(s_l, n)
            s_rows[i] = term if s_rows[i] is None else s_rows[i] + term

    def step(rows, qt, kt):
        pending.append((step.level, rows, _dot_nt(cat(qt), cat(kt))))
        step.level += 1
    step.level = 0

    step(every, q, k)
    yield
    b = 1
    while b < n_res:
        odd = [i for i in every if (i // b) % 2 == 1]
        qt, kt = [], []
        for i in every:
            bs = (i // b) * b
            if i in odd:
                qt.append(q[i] * jnp.exp2(g_in[i] - g_in[bs - 1]))
                kt.append(zero)
            else:
                be = bs + b - 1
                kt.append(k[i] * jnp.exp2(g_in[be] - g_in[i]) if i != be else k[i])
        step(odd, qt, kt)
        yield
        fold()
        b *= 2
    qe = [q[i] * jnp.exp2(g_in[i]) for i in every]
    ke = [k[i] * jnp.exp2(tot - g_in[i]) for i in range(n_res - 1)] + [k[-1]]
    if coarse:
        row = lax.broadcasted_iota(jnp.int32, tot.shape, 0)
        pref = tot
        for sh in (1, 2, 4):
            pref = pref + jnp.where(row >= sh, pltpu.roll(pref, sh, 0), 0.0)
        before = pref - tot
        end_all = _rows(pref, SUBLANES - 1)
        for gsz in (1, 2, 4):
            if gsz == 1:
                qt, kt = qe, ke
            else:
                if gsz == 2:
                    g_start = jnp.where(row % 2 == 0, before, pltpu.roll(before, 1, 0))
                    g_end = jnp.where(row % 2 == 1, pref, pltpu.roll(pref, SUBLANES - 1, 0))
                else:
                    g_start = jnp.where(row < 4, _rows(before, 0), _rows(before, 4))
                    g_end = jnp.where(row < 4, _rows(pref, 3), _rows(pref, 7))
                fq = jnp.exp2(before - g_start)
                fk = jnp.exp2(g_end - pref)
                qt = [x * fq for x in qe]
                kt = [x * fk for x in ke]
            step(every, qt, kt)
            yield
            fold()
        fq = jnp.exp2(before)
        fk = jnp.exp2(end_all - pref)
        qc = cat([x * fq for x in qe])
        kc = cat([x * fk for x in ke])
        o_st = _dot_nt(qc, get_st().astype(_BF16))
        decay = jnp.exp2(pref[SUBLANES - 1:SUBLANES, :])
        yield
    else:
        qc, kc = cat(qe), cat(ke)
        o_st = None
        decay = jnp.exp2(tot)
    while pending:
        fold()
    o = _dot(cat(s_rows), v_all)
    yield
    done(o if o_st is None else o + o_st, qc, kc, v_all, decay)


def _round_robin(starts):
    waiting = list(starts)
    live = []
    rnd = 0
    while waiting or live:
        if waiting and rnd % CHUNK_LAG == 0:
            live += waiting.pop(0)
        for g in list(live):
            try:
                next(g)
            except StopIteration:
                live.remove(g)
        rnd += 1


def _aligned(x, m):
    return x if isinstance(x, int) else pl.multiple_of(x, m)


def _loop(trips, body):
    if trips == 1:
        body(0, 0)
    else:
        lax.fori_loop(0, trips, body, 0)


def _head_out(o, ag, gate):
    ms = jnp.mean(o * o, axis=-1, keepdims=True)
    return o * lax.rsqrt(ms + EPS) * ag * gate


def _project_a(h, win_ref, lb_ref, store, cg):
    p = _dot(h(), win_ref[:, cg * MXU_N:(cg + 1) * MXU_N])
    arr = cg // 2
    if arr == 0 or arr == 3:
        p = _silu(p)
    elif arr == 1:
        lb = lb_ref[:, (cg % 2) * MXU_N:(cg % 2 + 1) * MXU_N]
        p = lb + (1.0 - lb) * _sigmoid(p)
    store(2 * cg, p[:, :LANES])
    store(2 * cg + 1, p[:, LANES:])


def _project_b(h, win_ref, bglu_ref, half):
    c0 = half * MXU_N
    glu_a = _dot(h(), win_ref[:, GLU_OFF + c0:GLU_OFF + c0 + MXU_N]) + bglu_ref[:, c0:c0 + MXU_N]
    glu_b = (_dot(h(), win_ref[:, GLU_OFF + B_WIDTH + c0:GLU_OFF + B_WIDTH + c0 + MXU_N])
             + bglu_ref[:, B_WIDTH + c0:B_WIDTH + c0 + MXU_N])
    zb = _dot(h(), win_ref[:, ZB_OFF + c0:ZB_OFF + c0 + MXU_N])
    return glu_a * _sigmoid(glu_b), _silu(zb)


def _rmsnorm(x, g):
    return x * lax.rsqrt(jnp.mean(x * x, axis=-1, keepdims=True) + EPS) * g


def _group_b_out(cv, lng_ref, lnb_ref, gate_b):
    mu = jnp.mean(cv, axis=-1, keepdims=True)
    d = cv - mu
    var = jnp.mean(d * d, axis=-1, keepdims=True)
    ln = d * lax.rsqrt(var + EPS) * lng_ref[...] + lnb_ref[...]
    return _silu(ln) * gate_b


def _conv_stages(ua_ref, ub_ref, wb_ref, convb_ref, lng_ref, lnb_ref, gb_ref, cv_ref, cat_ref, base, n):
    reps = lambda w, rows: jnp.concatenate([w] * (rows // (2 * SUBLANES)), axis=0)
    for ls in range(B_WIDTH // LANES):
        lanes = slice(ls * LANES, (ls + 1) * LANES)
        acc = jnp.broadcast_to(convb_ref[:, lanes], (n, LANES))
        for off in range(SUBLANES):
            taps = [j for j in range(CONV_WIDTH) if (FIRST + j) % SUBLANES == off]
            rows = n + (2 * SUBLANES if off else 0)
            part = None
            for j in taps:
                rel = FIRST + j - off
                if rel % (2 * SUBLANES) == 0:
                    win = ua_ref[pl.ds(_aligned(base + rel, 2 * SUBLANES), rows), lanes]
                else:
                    win = ub_ref[pl.ds(_aligned(base + rel + SUBLANES, 2 * SUBLANES), rows), lanes]
                term = reps(wb_ref[j, :, lanes], rows) * win
                part = term if part is None else part + term
            part = part.astype(_F32)
            acc = acc + (part[off:off + n, :] if off else part)
            if off == SUBLANES // 2 - 1:
                yield
        cv_ref[:, lanes] = acc
        yield
    rows = pl.ds(_aligned(base, n), n)
    o_b = _group_b_out(cv_ref[...], lng_ref, lnb_ref, gb_ref[rows, :])
    cat_ref[rows, A_WIDTH:] = o_b.astype(_BF16)


def _prompt_kernel(x_ref, normg_ref, win_ref, lb_ref, ag_ref, bglu_ref, convw_ref, convb_ref,
                   lng_ref, lnb_ref, wout_ref, fg_ref, masks_ref, wb_ref,
                   y_ref, st_out_ref, cs_out_ref,
                   pa_ref, uh_ref, ua_ref, ub_ref, gb_ref, cv_ref, oa_ref, cat_ref, st_ref, *, tb):
    t = pl.program_id(1)
    nt = pl.num_programs(1)

    @pl.when(t == 0)
    def _():
        st_ref[...] = jnp.zeros_like(st_ref)
        uh_ref[...] = jnp.zeros_like(uh_ref)
        ua_ref[...] = jnp.zeros_like(ua_ref)
        ub_ref[...] = jnp.zeros_like(ub_ref)

    def front_tile(r0):
        rows = pl.ds(r0, ROW_TILE)
        h = _rmsnorm(x_ref[rows, :], normg_ref[...]).astype(_BF16)
        for half in range(B_WIDTH // MXU_N):
            u, gb = _project_b(lambda: h, win_ref, bglu_ref, half)
            cols = slice(half * MXU_N, (half + 1) * MXU_N)
            here = pl.ds(_aligned(HIST + r0, HIST), ROW_TILE)
            prev = uh_ref[pl.ds(_aligned(HIST + r0 - SUBLANES, SUBLANES), SUBLANES), cols]
            uh_ref[here, cols] = u
            ua_ref[here, cols] = u.astype(_BF16)
            ub_ref[here, cols] = jnp.concatenate([prev, u[:ROW_TILE - SUBLANES, :]], axis=0).astype(_BF16)
            gb_ref[rows, cols] = gb

        def store_a(slab, val):
            pa_ref[slab, rows, :] = val
        for cg in range(2 * A_HEADS):
            _project_a(lambda: h, win_ref, lb_ref, store_a, cg)

    def front_body(p, carry):
        for n in range(UNROLL):
            front_tile(_aligned((UNROLL * p + n) * ROW_TILE, ROW_TILE))
        return carry

    _loop(tb // (UNROLL * ROW_TILE), front_body)
    ub_ref[HIST + tb:HIST + tb + 2 * SUBLANES, :] = jnp.concatenate(
        [uh_ref[HIST + tb - SUBLANES:HIST + tb, :], jnp.zeros((SUBLANES, B_WIDTH), _F32)], axis=0).astype(_BF16)

    def chunk(base, slot):
        def head(hd):
            def slabs(arr):
                return [pa_ref[arr * A_HEADS + hd, pl.ds(base + i, SUBLANES, stride=SUBLANES), :]
                        for i in range(SUBLANES)]

            def done(o, qc, kc, v_all, decay):
                st_ref[hd] = st_ref[hd] * decay + _dot_tn(v_all, kc)
                for i in range(SUBLANES):
                    oa_ref[hd, pl.ds(base + i, SUBLANES, stride=SUBLANES), :] = o[i * SUBLANES:(i + 1) * SUBLANES, :]
            return _hgrn_stages(slabs(0), slabs(1), slabs(2), masks_ref, lambda: st_ref[hd], True, done)
        conv = _conv_stages(ua_ref, ub_ref, wb_ref, convb_ref, lng_ref, lnb_ref, gb_ref, cv_ref.at[slot], cat_ref, base, CHUNK)
        return [head(hd) for hd in range(A_HEADS)] + [conv]

    def chunk_body(p, carry):
        _round_robin([chunk(_aligned((CHUNK_UNROLL * p + n) * CHUNK, CHUNK), n % CONV_BUFS) for n in range(CHUNK_UNROLL)])
        return carry

    _loop(tb // (CHUNK_UNROLL * CHUNK), chunk_body)

    @pl.when(t == nt - 1)
    def _():
        cs_out_ref[...] = uh_ref[HIST + tb - (CONV_WIDTH - 1):HIST + tb, :]
        for hd in range(A_HEADS):
            st_out_ref[hd] = st_ref[hd].T

    uh_ref[0:HIST, :] = uh_ref[tb:tb + HIST, :]
    ua_ref[0:HIST, :] = ua_ref[tb:tb + HIST, :]
    ub_ref[0:HIST, :] = ub_ref[tb:tb + HIST, :]

    def back_body(p, carry):
        tiles = [pl.ds(_aligned((UNROLL * p + n) * ROW_TILE, ROW_TILE), ROW_TILE) for n in range(UNROLL)]
        for rows in tiles:
            for hd in range(A_HEADS):
                on = _head_out(oa_ref[hd, rows, :], ag_ref[...], pa_ref[3 * A_HEADS + hd, rows, :])
                cat_ref[rows, hd * HEAD_DIM:(hd + 1) * HEAD_DIM] = on.astype(_BF16)
        outs = [_dot(cat_ref[rows, :], wout_ref[...]) for rows in tiles]
        for rows, out in zip(tiles, outs):
            y_ref[rows, :] = _rmsnorm(x_ref[rows, :] + out, fg_ref[...])
        return carry

    _loop(tb // (UNROLL * ROW_TILE), back_body)


def _full(shape):
    nd = len(shape)
    return pl.BlockSpec(shape, lambda *_: (0,) * nd, pipeline_mode=pl.Buffered(1))


def _prompt_call(x, weights, tb):
    n, t, _ = x.shape
    assert t % tb == 0 and tb % (UNROLL * ROW_TILE) == 0 and tb % (CHUNK_UNROLL * CHUNK) == 0
    masks = jnp.asarray(_level_masks((1, 2, 4, 8, 16, 32), SUBLANES, SUBLANES))
    conv_w = weights[5]
    wb = jnp.broadcast_to(conv_w.astype(_BF16)[:, None, :], (CONV_WIDTH, 2 * SUBLANES, B_WIDTH))
    kern = functools.partial(_prompt_kernel, tb=tb)
    w_specs = [_full(w.shape) for w in weights]
    return pl.pallas_call(
        kern,
        grid=(n, t // tb),
        in_specs=([pl.BlockSpec((None, tb, D_MODEL), lambda i, j: (i, j, 0))] + w_specs
                  + [_full(masks.shape), _full(wb.shape)]),
        out_specs=[
            pl.BlockSpec((None, tb, D_MODEL), lambda i, j: (i, j, 0)),
            pl.BlockSpec((None, A_HEADS, HEAD_DIM, HEAD_DIM), lambda i, j: (i, 0, 0, 0)),
            pl.BlockSpec((None, CONV_WIDTH - 1, B_WIDTH), lambda i, j: (i, 0, 0)),
        ],
        out_shape=[
            jax.ShapeDtypeStruct((n, t, D_MODEL), _F32),
            jax.ShapeDtypeStruct((n, A_HEADS, HEAD_DIM, HEAD_DIM), _F32),
            jax.ShapeDtypeStruct((n, CONV_WIDTH - 1, B_WIDTH), _F32),
        ],
        scratch_shapes=[
            pltpu.VMEM((4 * A_HEADS, tb, LANES), _F32),
            pltpu.VMEM((HIST + tb, B_WIDTH), _F32),
            pltpu.VMEM((HIST + tb + 2 * SUBLANES, B_WIDTH), _BF16),
            pltpu.VMEM((HIST + tb + 2 * SUBLANES, B_WIDTH), _BF16),
            pltpu.VMEM((tb, B_WIDTH), _F32),
            pltpu.VMEM((CONV_BUFS, CHUNK, B_WIDTH), _F32),
            pltpu.VMEM((A_HEADS, tb, LANES), _F32),
            pltpu.VMEM((tb, A_WIDTH + B_WIDTH), _BF16),
            pltpu.VMEM((A_HEADS, HEAD_DIM, HEAD_DIM), _F32),
        ],
        compiler_params=pltpu.CompilerParams(
            dimension_semantics=("arbitrary", "arbitrary"),
            vmem_limit_bytes=VMEM_LIMIT_BYTES,
        ),
        name="hymba_prompt",
    )(x, *weights, masks, wb)


def _decode_kernel(x_ref, s0_ref, c0_ref, normg_ref, win_ref, lb_ref, ag_ref, bglu_ref, convw_ref, convb_ref,
                   lng_ref, lnb_ref, wout_ref, fg_ref, masks_ref,
                   y_ref, s_out_ref, cs_out_ref,
                   pa_ref, ub_ref, gb_ref, oa_ref, ob_ref, *, t_dec):
    g = pl.program_id(0)
    n_tok = x_ref.shape[0]
    grp = SUBLANES * t_dec
    n_ls = B_WIDTH // LANES

    @pl.when(g == 0)
    def _():
        def proj_body(r, carry):
            rows = pl.ds(pl.multiple_of(r * ROW_TILE, ROW_TILE), ROW_TILE)
            h = _rmsnorm(x_ref[rows, :], normg_ref[...]).astype(_BF16)
            for half in range(B_WIDTH // MXU_N):
                u, gb = _project_b(lambda: h, win_ref, bglu_ref, half)
                for c in range(MXU_N // LANES):
                    ls = half * (MXU_N // LANES) + c
                    ub_ref[ls, rows, :] = u[:, c * LANES:(c + 1) * LANES]
                    gb_ref[ls, rows, :] = gb[:, c * LANES:(c + 1) * LANES]

            def store_a(slab, val):
                pa_ref[slab, rows, :] = val
            for cg in range(2 * A_HEADS):
                _project_a(lambda: h, win_ref, lb_ref, store_a, cg)
            return carry
        lax.fori_loop(0, n_tok // ROW_TILE, proj_body, 0)

    def group(sub):
        seqs = slice(sub * SUBLANES, (sub + 1) * SUBLANES)
        r0 = pl.multiple_of((g * DEC_GROUPS + sub) * grp, grp)

        def token_slab(ref, idx, t):
            return ref[idx, pl.ds(r0 + t, SUBLANES, stride=t_dec), :]

        u_tok = [jnp.concatenate([token_slab(ub_ref, ls, t) for ls in range(n_ls)], axis=-1) for t in range(t_dec)]
        full = lambda kk: c0_ref[kk, seqs, :] if kk < CONV_WIDTH - 1 else u_tok[kk - (CONV_WIDTH - 1)]
        for kk in range(CONV_WIDTH - 1):
            cs_out_ref[kk, seqs, :] = full(kk + t_dec)
        for t in range(t_dec):
            cv = jnp.broadcast_to(convb_ref[...], (SUBLANES, B_WIDTH))
            for j in range(CONV_WIDTH):
                cv = cv + convw_ref[j:j + 1, :] * full(t + j)
            gate_b = jnp.concatenate([token_slab(gb_ref, ls, t) for ls in range(n_ls)], axis=-1)
            o_b = _group_b_out(cv, lng_ref, lnb_ref, gate_b)
            for ls in range(n_ls):
                ob_ref[ls, pl.ds(r0 + t, SUBLANES, stride=t_dec), :] = o_b[:, ls * LANES:(ls + 1) * LANES]

        row32 = lax.broadcasted_iota(jnp.int32, (grp, LANES), 0) % SUBLANES
        row64 = lax.broadcasted_iota(jnp.int32, (2 * grp, LANES), 0)
        for hd in range(A_HEADS):
            slabs = lambda arr: [token_slab(pa_ref, arr * A_HEADS + hd, t) for t in range(t_dec)]
            res = []
            for _ in _hgrn_stages(slabs(0), slabs(1), slabs(2), masks_ref, None, False, lambda *a: res.extend(a)):
                pass
            o, qc, kc, v_all, decay = res
            qf, kf, vf = qc.astype(_F32), kc.astype(_F32), v_all.astype(_F32)
            e_hi = decay.astype(_BF16).astype(_F32)
            e_mid = (decay - e_hi).astype(_BF16).astype(_F32)
            e_lo = (decay - e_hi - e_mid).astype(_BF16).astype(_F32)
            lhs = jnp.concatenate([kf, e_hi, e_mid, e_lo, jnp.zeros((SUBLANES, LANES), _F32)], axis=0).astype(_BF16)
            v_pad = jnp.concatenate([vf, jnp.zeros((grp, LANES), _F32)], axis=0)
            o_state = None
            for s in range(SUBLANES):
                s0 = s0_ref[sub * SUBLANES + s, hd]
                term = _dot(jnp.where(row32 == s, qf, 0.0).astype(_BF16), s0.astype(_BF16))
                o_state = term if o_state is None else o_state + term
                mine = row64 % SUBLANES == s
                rhs = jnp.concatenate([
                    jnp.where(mine & (row64 < grp), v_pad, 0.0),
                    jnp.where(mine & (row64 >= grp) & (row64 < grp + 3 * SUBLANES), 1.0, 0.0)], axis=-1).astype(_BF16)
                upd = _dot_tn(lhs, rhs)
                s_out_ref[sub * SUBLANES + s, hd] = s0 * upd[:, HEAD_DIM:] + upd[:, :HEAD_DIM]
            gate = jnp.concatenate([token_slab(pa_ref, 3 * A_HEADS + hd, t) for t in range(t_dec)], axis=0)
            on = _head_out(o + o_state, ag_ref[...], gate)
            for t in range(t_dec):
                oa_ref[hd, pl.ds(r0 + t, SUBLANES, stride=t_dec), :] = on[t * SUBLANES:(t + 1) * SUBLANES, :]

    for sub in range(DEC_GROUPS):
        group(sub)

    @pl.when(g == pl.num_programs(0) - 1)
    def _():
        def out_body(r, carry):
            rows = pl.ds(pl.multiple_of(r * ROW_TILE, ROW_TILE), ROW_TILE)
            cat = jnp.concatenate([oa_ref[hd, rows, :] for hd in range(A_HEADS)]
                                  + [ob_ref[ls, rows, :] for ls in range(n_ls)], axis=-1).astype(_BF16)
            out = _dot(cat, wout_ref[...])
            y_ref[rows, :] = _rmsnorm(x_ref[rows, :] + out, fg_ref[...])
            return carry
        lax.fori_loop(0, n_tok // ROW_TILE, out_body, 0)


def _decode_call(x, s0, c0, weights):
    n, t_dec, _ = x.shape
    assert n % (DEC_GROUPS * SUBLANES) == 0 and t_dec == 4 and (n * t_dec) % ROW_TILE == 0
    n_tok = n * t_dec
    x = x.reshape(n_tok, D_MODEL)
    masks = jnp.asarray(_level_masks((1, 2), t_dec, SUBLANES, same_group=t_dec))
    kern = functools.partial(_decode_kernel, t_dec=t_dec)
    w_specs = [_full(w.shape) for w in weights]
    per_step = DEC_GROUPS * SUBLANES
    state_spec = pl.BlockSpec((per_step, A_HEADS, HEAD_DIM, HEAD_DIM), lambda i: (i, 0, 0, 0))
    conv_spec = pl.BlockSpec((CONV_WIDTH - 1, per_step, B_WIDTH), lambda i: (0, i, 0))
    y, s_new, c_new = pl.pallas_call(
        kern,
        grid=(n // per_step,),
        in_specs=[_full((n_tok, D_MODEL)), state_spec, conv_spec] + w_specs + [_full(masks.shape)],
        out_specs=[_full((n_tok, D_MODEL)), state_spec, conv_spec],
        out_shape=[
            jax.ShapeDtypeStruct((n_tok, D_MODEL), _F32),
            jax.ShapeDtypeStruct((n, A_HEADS, HEAD_DIM, HEAD_DIM), _F32),
            jax.ShapeDtypeStruct((CONV_WIDTH - 1, n, B_WIDTH), _F32),
        ],
        scratch_shapes=[
            pltpu.VMEM((4 * A_HEADS, n_tok, LANES), _F32),
            pltpu.VMEM((B_WIDTH // LANES, n_tok, LANES), _F32),
            pltpu.VMEM((B_WIDTH // LANES, n_tok, LANES), _F32),
            pltpu.VMEM((A_HEADS, n_tok, LANES), _F32),
            pltpu.VMEM((B_WIDTH // LANES, n_tok, LANES), _F32),
        ],
        compiler_params=pltpu.CompilerParams(
            dimension_semantics=("arbitrary",),
            vmem_limit_bytes=VMEM_LIMIT_BYTES,
        ),
        name="hymba_decode",
    )(x, s0, c0, *weights, masks)
    return y.reshape(n, t_dec, D_MODEL), s_new, c_new


def _prepare_weights(norm_in_g, w_in, lb_logits, hgrn_norm_g, b_glu, conv_w, conv_b, ln_g, ln_b, w_out, final_norm_g, layer):
    lb = jnp.cumsum(jax.nn.softmax(lb_logits.astype(_F32), axis=0), axis=0)[layer]
    row = lambda a: a.astype(_F32).reshape(1, -1)
    return (
        row(norm_in_g[layer]),
        w_in[layer].astype(_BF16),
        row(lb),
        row(hgrn_norm_g[layer]),
        row(b_glu[layer]),
        conv_w[layer].astype(_F32),
        row(conv_b[layer]),
        row(ln_g[layer]),
        row(ln_b[layer]),
        w_out[layer].astype(_BF16),
        row(final_norm_g),
    )


def kernel(x_prompt, x_sample, state_hgrn, state_conv, norm_in_g, w_in, lb_logits, hgrn_norm_g, b_glu, conv_w,
           conv_b, ln_g, ln_b, w_out, final_norm_g):
    depth = w_in.shape[0]
    assert depth == 1, "single mixer layer: the final norm is fused into the layer kernel"
    weights = _prepare_weights(norm_in_g, w_in, lb_logits, hgrn_norm_g, b_glu, conv_w, conv_b, ln_g, ln_b, w_out,
                               final_norm_g, 0)
    y_p, s_p, c_p = _prompt_call(x_prompt, weights, tb=1024)
    y_s, s_s, c_s = _decode_call(x_sample, state_hgrn[0], jnp.transpose(state_conv[0], (1, 0, 2)), weights)
    return (y_p, y_s, s_p[None], c_p[None], s_s[None], jnp.transpose(c_s, (1, 0, 2))[None])
```

```python
import functools
import math

import numpy as np
import jax
import jax.numpy as jnp
from jax import lax
from jax.experimental import pallas as pl
from jax.experimental.pallas import tpu as pltpu

D_MODEL = 1024
A_HEADS = 4
HEAD_DIM = 128
A_WIDTH = A_HEADS * HEAD_DIM
B_WIDTH = 512
CONV_WIDTH = 31
EPS = 1e-6
IN_WIDTH = 4 * A_WIDTH + 3 * B_WIDTH
GLU_OFF = 4 * A_WIDTH
ZB_OFF = GLU_OFF + 2 * B_WIDTH

LANES = 128
SUBLANES = 8
MXU_N = 256
CHUNK = SUBLANES * SUBLANES
ROW_TILE = 128
UNROLL = 8
CHUNK_UNROLL = 2
CHUNK_LAG = 4
CONV_BUFS = 3
DEC_GROUPS = 2
HIST = 32
FIRST = HIST - (CONV_WIDTH - 1)
VMEM_LIMIT_BYTES = 56 * 1024 * 1024
LOG2E = math.log2(math.e)

_F32 = jnp.float32
_BF16 = jnp.bfloat16


def _sigmoid(x):
    return 1.0 / (1.0 + jnp.exp(-x))


def _silu(x):
    return x * _sigmoid(x)


def _dot(a, b):
    return jnp.dot(a, b, preferred_element_type=_F32)


def _dot_nt(a, b):
    return lax.dot_general(a, b, (((1,), (1,)), ((), ())), preferred_element_type=_F32)


def _dot_tn(a, b):
    return lax.dot_general(a, b, (((0,), (0,)), ((), ())), preferred_element_type=_F32)


def _level_masks(levels, n_res, n_rows, same_group=None):
    r = np.arange(n_res * n_rows)
    t = n_res * (r % n_rows) + r // n_rows
    tt, ss = t[:, None], t[None, :]
    masks = [tt == ss]
    for b in levels:
        masks.append(((tt // b) == (ss // b) + 1) & ((ss // b) % 2 == 0))
    masks = np.stack(masks)
    want = (ss <= tt) if same_group is None else ((ss <= tt) & (tt // same_group == ss // same_group))
    assert (masks.sum(0) == want).all()
    return masks.astype(np.float32)


def _rows(x, r):
    return jnp.broadcast_to(x[r:r + 1, :], x.shape)


def _hgrn_stages(q, f, v, masks_ref, get_st, coarse, done):
    n_res = len(q)
    every = range(n_res)
    k = [1.0 - fi for fi in f]
    lf = [jnp.log(fi) * LOG2E for fi in f]
    g_in = [lf[0]]
    for i in range(1, n_res):
        g_in.append(g_in[-1] + lf[i])
    tot = g_in[-1]
    zero = jnp.zeros_like(tot)

    def cat(parts):
        return jnp.concatenate(parts, axis=0).astype(_BF16)

    def slab_rows(x, i):
        return x[i * SUBLANES:(i + 1) * SUBLANES, :]

    v_all = cat(v)
    s_rows = [None] * n_res
    pending = []

    def fold():
        level, rows, s_l = pending.pop(0)
        for n, i in enumerate(rows):
            term = masks_ref[level, i * SUBLANES:(i + 1) * SUBLANES, :] * slab_rows(s_l, n)
            s_rows[i] = term if s_rows[i] is None else s_rows[i] + term

    def step(rows, qt, kt):
        pending.append((step.level, rows, _dot_nt(cat(qt), cat(kt))))
        step.level += 1
    step.level = 0

    step(every, q, k)
    yield
    b = 1
    while b < n_res:
        odd = [i for i in every if (i // b) % 2 == 1]
        qt, kt = [], []
        for i in every:
            bs = (i // b) * b
            if i in odd:
                qt.append(q[i] * jnp.exp2(g_in[i] - g_in[bs - 1]))
                kt.append(zero)
            else:
                be = bs + b - 1
                kt.append(k[i] * jnp.exp2(g_in[be] - g_in[i]) if i != be else k[i])
        step(odd, qt, kt)
        yield
        fold()
        b *= 2
    qe = [q[i] * jnp.exp2(g_in[i]) for i in every]
    ke = [k[i] * jnp.exp2(tot - g_in[i]) for i in range(n_res - 1)] + [k[-1]]
    if coarse:
        row = lax.broadcasted_iota(jnp.int32, tot.shape, 0)
        pref = tot
        for sh in (1, 2, 4):
            pref = pref + jnp.where(row >= sh, pltpu.roll(pref, sh, 0), 0.0)
        before = pref - tot
        end_all = _rows(pref, SUBLANES - 1)
        qe_b, ke_b = cat(qe), cat(ke)
        tile = lambda x: jnp.concatenate([x] * n_res, axis=0).astype(_BF16)
        for gsz in (1, 2, 4):
            if gsz == 1:
                qt, kt = qe_b, ke_b
            else:
                if gsz == 2:
                    g_start = jnp.where(row % 2 == 0, before, pltpu.roll(before, 1, 0))
                    g_end = jnp.where(row % 2 == 1, pref, pltpu.roll(pref, SUBLANES - 1, 0))
                else:
                    g_start = jnp.where(row < 4, _rows(before, 0), _rows(before, 4))
                    g_end = jnp.where(row < 4, _rows(pref, 3), _rows(pref, 7))
                qt = qe_b * tile(jnp.exp2(before - g_start))
                kt = ke_b * tile(jnp.exp2(g_end - pref))
            pending.append((step.level, every, _dot_nt(qt, kt)))
            step.level += 1
            yield
            fold()
        qc = qe_b * tile(jnp.exp2(before))
        kc = ke_b * tile(jnp.exp2(end_all - pref))
        o_st = _dot_nt(qc, get_st().astype(_BF16))
        decay = jnp.exp2(pref[SUBLANES - 1:SUBLANES, :])
        yield
    else:
        qc, kc = cat(qe), cat(ke)
        o_st = None
        decay = jnp.exp2(tot)
    while pending:
        fold()
    o = _dot(cat(s_rows), v_all)
    yield
    done(o if o_st is None else o + o_st, qc, kc, v_all, decay)


def _round_robin(starts):
    waiting = list(starts)
    live = []
    rnd = 0
    while waiting or live:
        if waiting and rnd % CHUNK_LAG == 0:
            live += waiting.pop(0)
        for g in list(live):
            try:
                next(g)
            except StopIteration:
                live.remove(g)
        rnd += 1


def _aligned(x, m):
    return x if isinstance(x, int) else pl.multiple_of(x, m)


def _loop(trips, body):
    if trips == 1:
        body(0, 0)
    else:
        lax.fori_loop(0, trips, body, 0)


def _head_out(o, ag, gate):
    ms = jnp.mean(o * o, axis=-1, keepdims=True)
    return o * lax.rsqrt(ms + EPS) * ag * gate


def _project_a(h, win_ref, lb_ref, store, cg):
    p = _dot(h(), win_ref[:, cg * MXU_N:(cg + 1) * MXU_N])
    arr = cg // 2
    if arr == 0 or arr == 3:
        p = _silu(p)
    elif arr == 1:
        lb = lb_ref[:, (cg % 2) * MXU_N:(cg % 2 + 1) * MXU_N]
        p = lb + (1.0 - lb) * _sigmoid(p)
    store(2 * cg, p[:, :LANES])
    store(2 * cg + 1, p[:, LANES:])


def _project_b(h, win_ref, bglu_ref, half):
    c0 = half * MXU_N
    glu_a = _dot(h(), win_ref[:, GLU_OFF + c0:GLU_OFF + c0 + MXU_N]) + bglu_ref[:, c0:c0 + MXU_N]
    glu_b = (_dot(h(), win_ref[:, GLU_OFF + B_WIDTH + c0:GLU_OFF + B_WIDTH + c0 + MXU_N])
             + bglu_ref[:, B_WIDTH + c0:B_WIDTH + c0 + MXU_N])
    zb = _dot(h(), win_ref[:, ZB_OFF + c0:ZB_OFF + c0 + MXU_N])
    return glu_a * _sigmoid(glu_b), _silu(zb)


def _rmsnorm(x, g):
    return x * lax.rsqrt(jnp.mean(x * x, axis=-1, keepdims=True) + EPS) * g


def _group_b_out(cv, lng_ref, lnb_ref, gate_b):
    mu = jnp.mean(cv, axis=-1, keepdims=True)
    d = cv - mu
    var = jnp.mean(d * d, axis=-1, keepdims=True)
    ln = d * lax.rsqrt(var + EPS) * lng_ref[...] + lnb_ref[...]
    return _silu(ln) * gate_b


def _conv_stages(ua_ref, ub_ref, wb_ref, convb_ref, lng_ref, lnb_ref, gb_ref, cv_ref, cat_ref, base, n):
    reps = lambda w, rows: jnp.concatenate([w] * (rows // (2 * SUBLANES)), axis=0)
    for ls in range(B_WIDTH // LANES):
        lanes = slice(ls * LANES, (ls + 1) * LANES)
        acc = jnp.broadcast_to(convb_ref[:, lanes], (n, LANES))
        for off in range(SUBLANES):
            taps = [j for j in range(CONV_WIDTH) if (FIRST + j) % SUBLANES == off]
            rows = n + (2 * SUBLANES if off else 0)
            part = None
            for j in taps:
                rel = FIRST + j - off
                if rel % (2 * SUBLANES) == 0:
                    win = ua_ref[pl.ds(_aligned(base + rel, 2 * SUBLANES), rows), lanes]
                else:
                    win = ub_ref[pl.ds(_aligned(base + rel + SUBLANES, 2 * SUBLANES), rows), lanes]
                term = reps(wb_ref[j, :, lanes], rows) * win
                part = term if part is None else part + term
            part = part.astype(_F32)
            acc = acc + (part[off:off + n, :] if off else part)
            if off == SUBLANES // 2 - 1:
                yield
        cv_ref[:, lanes] = acc
        yield
    rows = pl.ds(_aligned(base, n), n)
    o_b = _group_b_out(cv_ref[...], lng_ref, lnb_ref, gb_ref[rows, :])
    cat_ref[rows, A_WIDTH:] = o_b.astype(_BF16)


def _prompt_kernel(x_ref, normg_ref, win_ref, lb_ref, ag_ref, bglu_ref, convw_ref, convb_ref,
                   lng_ref, lnb_ref, wout_ref, fg_ref, masks_ref, wb_ref,
                   y_ref, st_out_ref, cs_out_ref,
                   pa_ref, uh_ref, ua_ref, ub_ref, gb_ref, cv_ref, oa_ref, cat_ref, st_ref, *, tb):
    t = pl.program_id(1)
    nt = pl.num_programs(1)

    @pl.when(t == 0)
    def _():
        st_ref[...] = jnp.zeros_like(st_ref)
        uh_ref[...] = jnp.zeros_like(uh_ref)
        ua_ref[...] = jnp.zeros_like(ua_ref)
        ub_ref[...] = jnp.zeros_like(ub_ref)

    def front_tile(r0):
        rows = pl.ds(r0, ROW_TILE)
        h = _rmsnorm(x_ref[rows, :], normg_ref[...]).astype(_BF16)
        for half in range(B_WIDTH // MXU_N):
            u, gb = _project_b(lambda: h, win_ref, bglu_ref, half)
            cols = slice(half * MXU_N, (half + 1) * MXU_N)
            here = pl.ds(_aligned(HIST + r0, HIST), ROW_TILE)
            prev = uh_ref[pl.ds(_aligned(HIST + r0 - SUBLANES, SUBLANES), SUBLANES), cols]
            uh_ref[here, cols] = u
            ua_ref[here, cols] = u.astype(_BF16)
            ub_ref[here, cols] = jnp.concatenate([prev, u[:ROW_TILE - SUBLANES, :]], axis=0).astype(_BF16)
            gb_ref[rows, cols] = gb

        def store_a(slab, val):
            pa_ref[slab, rows, :] = val
        for cg in range(2 * A_HEADS):
            _project_a(lambda: h, win_ref, lb_ref, store_a, cg)

    def front_body(p, carry):
        for n in range(UNROLL):
            front_tile(_aligned((UNROLL * p + n) * ROW_TILE, ROW_TILE))
        return carry

    _loop(tb // (UNROLL * ROW_TILE), front_body)
    ub_ref[HIST + tb:HIST + tb + 2 * SUBLANES, :] = jnp.concatenate(
        [uh_ref[HIST + tb - SUBLANES:HIST + tb, :], jnp.zeros((SUBLANES, B_WIDTH), _F32)], axis=0).astype(_BF16)

    def chunk(base, slot):
        def head(hd):
            def slabs(arr):
                return [pa_ref[arr * A_HEADS + hd, pl.ds(base + i, SUBLANES, stride=SUBLANES), :]
                        for i in range(SUBLANES)]

            def done(o, qc, kc, v_all, decay):
                st_ref[hd] = st_ref[hd] * decay + _dot_tn(v_all, kc)
                for i in range(SUBLANES):
                    oa_ref[hd, pl.ds(base + i, SUBLANES, stride=SUBLANES), :] = o[i * SUBLANES:(i + 1) * SUBLANES, :]
            return _hgrn_stages(slabs(0), slabs(1), slabs(2), masks_ref, lambda: st_ref[hd], True, done)
        conv = _conv_stages(ua_ref, ub_ref, wb_ref, convb_ref, lng_ref, lnb_ref, gb_ref, cv_ref.at[slot], cat_ref, base, CHUNK)
        return [head(hd) for hd in range(A_HEADS)] + [conv]

    def chunk_body(p, carry):
        _round_robin([chunk(_aligned((CHUNK_UNROLL * p + n) * CHUNK, CHUNK), n % CONV_BUFS) for n in range(CHUNK_UNROLL)])
        return carry

    _loop(tb // (CHUNK_UNROLL * CHUNK), chunk_body)

    @pl.when(t == nt - 1)
    def _():
        cs_out_ref[...] = uh_ref[HIST + tb - (CONV_WIDTH - 1):HIST + tb, :]
        for hd in range(A_HEADS):
            st_out_ref[hd] = st_ref[hd].T

    uh_ref[0:HIST, :] = uh_ref[tb:tb + HIST, :]
    ua_ref[0:HIST, :] = ua_ref[tb:tb + HIST, :]
    ub_ref[0:HIST, :] = ub_ref[tb:tb + HIST, :]

    def back_body(p, carry):
        tiles = [pl.ds(_aligned((UNROLL * p + n) * ROW_TILE, ROW_TILE), ROW_TILE) for n in range(UNROLL)]
        for rows in tiles:
            for hd in range(A_HEADS):
                on = _head_out(oa_ref[hd, rows, :], ag_ref[...], pa_ref[3 * A_HEADS + hd, rows, :])
                cat_ref[rows, hd * HEAD_DIM:(hd + 1) * HEAD_DIM] = on.astype(_BF16)
        outs = [_dot(cat_ref[rows, :], wout_ref[...]) for rows in tiles]
        for rows, out in zip(tiles, outs):
            y_ref[rows, :] = _rmsnorm(x_ref[rows, :] + out, fg_ref[...])
        return carry

    _loop(tb // (UNROLL * ROW_TILE), back_body)


def _full(shape):
    nd = len(shape)
    return pl.BlockSpec(shape, lambda *_: (0,) * nd, pipeline_mode=pl.Buffered(1))


def _prompt_call(x, weights, tb):
    n, t, _ = x.shape
    assert t % tb == 0 and tb % (UNROLL * ROW_TILE) == 0 and tb % (CHUNK_UNROLL * CHUNK) == 0
    masks = jnp.asarray(_level_masks((1, 2, 4, 8, 16, 32), SUBLANES, SUBLANES))
    conv_w = weights[5]
    wb = jnp.broadcast_to(conv_w.astype(_BF16)[:, None, :], (CONV_WIDTH, 2 * SUBLANES, B_WIDTH))
    kern = functools.partial(_prompt_kernel, tb=tb)
    w_specs = [_full(w.shape) for w in weights]
    return pl.pallas_call(
        kern,
        grid=(n, t // tb),
        in_specs=([pl.BlockSpec((None, tb, D_MODEL), lambda i, j: (i, j, 0))] + w_specs
                  + [_full(masks.shape), _full(wb.shape)]),
        out_specs=[
            pl.BlockSpec((None, tb, D_MODEL), lambda i, j: (i, j, 0)),
            pl.BlockSpec((None, A_HEADS, HEAD_DIM, HEAD_DIM), lambda i, j: (i, 0, 0, 0)),
            pl.BlockSpec((None, CONV_WIDTH - 1, B_WIDTH), lambda i, j: (i, 0, 0)),
        ],
        out_shape=[
            jax.ShapeDtypeStruct((n, t, D_MODEL), _F32),
            jax.ShapeDtypeStruct((n, A_HEADS, HEAD_DIM, HEAD_DIM), _F32),
            jax.ShapeDtypeStruct((n, CONV_WIDTH - 1, B_WIDTH), _F32),
        ],
        scratch_shapes=[
            pltpu.VMEM((4 * A_HEADS, tb, LANES), _F32),
            pltpu.VMEM((HIST + tb, B_WIDTH), _F32),
            pltpu.VMEM((HIST + tb + 2 * SUBLANES, B_WIDTH), _BF16),
            pltpu.VMEM((HIST + tb + 2 * SUBLANES, B_WIDTH), _BF16),
            pltpu.VMEM((tb, B_WIDTH), _F32),
            pltpu.VMEM((CONV_BUFS, CHUNK, B_WIDTH), _F32),
            pltpu.VMEM((A_HEADS, tb, LANES), _F32),
            pltpu.VMEM((tb, A_WIDTH + B_WIDTH), _BF16),
            pltpu.VMEM((A_HEADS, HEAD_DIM, HEAD_DIM), _F32),
        ],
        compiler_params=pltpu.CompilerParams(
            dimension_semantics=("arbitrary", "arbitrary"),
            vmem_limit_bytes=VMEM_LIMIT_BYTES,
        ),
        name="hymba_prompt",
    )(x, *weights, masks, wb)


def _decode_kernel(x_ref, s0_ref, c0_ref, normg_ref, win_ref, lb_ref, ag_ref, bglu_ref, convw_ref, convb_ref,
                   lng_ref, lnb_ref, wout_ref, fg_ref, masks_ref,
                   y_ref, s_out_ref, cs_out_ref,
                   pa_ref, ub_ref, gb_ref, oa_ref, ob_ref, *, t_dec):
    g = pl.program_id(0)
    n_tok = x_ref.shape[0]
    grp = SUBLANES * t_dec
    n_ls = B_WIDTH // LANES

    @pl.when(g == 0)
    def _():
        def proj_body(r, carry):
            rows = pl.ds(pl.multiple_of(r * ROW_TILE, ROW_TILE), ROW_TILE)
            h = _rmsnorm(x_ref[rows, :], normg_ref[...]).astype(_BF16)
            for half in range(B_WIDTH // MXU_N):
                u, gb = _project_b(lambda: h, win_ref, bglu_ref, half)
                for c in range(MXU_N // LANES):
                    ls = half * (MXU_N // LANES) + c
                    ub_ref[ls, rows, :] = u[:, c * LANES:(c + 1) * LANES]
                    gb_ref[ls, rows, :] = gb[:, c * LANES:(c + 1) * LANES]

            def store_a(slab, val):
                pa_ref[slab, rows, :] = val
            for cg in range(2 * A_HEADS):
                _project_a(lambda: h, win_ref, lb_ref, store_a, cg)
            return carry
        lax.fori_loop(0, n_tok // ROW_TILE, proj_body, 0)

    def group(sub):
        seqs = slice(sub * SUBLANES, (sub + 1) * SUBLANES)
        r0 = pl.multiple_of((g * DEC_GROUPS + sub) * grp, grp)

        def token_slab(ref, idx, t):
            return ref[idx, pl.ds(r0 + t, SUBLANES, stride=t_dec), :]

        u_tok = [jnp.concatenate([token_slab(ub_ref, ls, t) for ls in range(n_ls)], axis=-1) for t in range(t_dec)]
        full = lambda kk: c0_ref[kk, seqs, :] if kk < CONV_WIDTH - 1 else u_tok[kk - (CONV_WIDTH - 1)]
        for kk in range(CONV_WIDTH - 1):
            cs_out_ref[kk, seqs, :] = full(kk + t_dec)
        for t in range(t_dec):
            cv = jnp.broadcast_to(convb_ref[...], (SUBLANES, B_WIDTH))
            for j in range(CONV_WIDTH):
                cv = cv + convw_ref[j:j + 1, :] * full(t + j)
            gate_b = jnp.concatenate([token_slab(gb_ref, ls, t) for ls in range(n_ls)], axis=-1)
            o_b = _group_b_out(cv, lng_ref, lnb_ref, gate_b)
            for ls in range(n_ls):
                ob_ref[ls, pl.ds(r0 + t, SUBLANES, stride=t_dec), :] = o_b[:, ls * LANES:(ls + 1) * LANES]

        row32 = lax.broadcasted_iota(jnp.int32, (grp, LANES), 0) % SUBLANES
        row64 = lax.broadcasted_iota(jnp.int32, (2 * grp, LANES), 0)
        for hd in range(A_HEADS):
            slabs = lambda arr: [token_slab(pa_ref, arr * A_HEADS + hd, t) for t in range(t_dec)]
            res = []
            for _ in _hgrn_stages(slabs(0), slabs(1), slabs(2), masks_ref, None, False, lambda *a: res.extend(a)):
                pass
            o, qc, kc, v_all, decay = res
            qf, kf, vf = qc.astype(_F32), kc.astype(_F32), v_all.astype(_F32)
            e_hi = decay.astype(_BF16).astype(_F32)
            e_mid = (decay - e_hi).astype(_BF16).astype(_F32)
            e_lo = (decay - e_hi - e_mid).astype(_BF16).astype(_F32)
            lhs = jnp.concatenate([kf, e_hi, e_mid, e_lo, jnp.zeros((SUBLANES, LANES), _F32)], axis=0).astype(_BF16)
            v_pad = jnp.concatenate([vf, jnp.zeros((grp, LANES), _F32)], axis=0)
            o_state = None
            for s in range(SUBLANES):
                s0 = s0_ref[sub * SUBLANES + s, hd]
                term = _dot(jnp.where(row32 == s, qf, 0.0).astype(_BF16), s0.astype(_BF16))
                o_state = term if o_state is None else o_state + term
                mine = row64 % SUBLANES == s
                rhs = jnp.concatenate([
                    jnp.where(mine & (row64 < grp), v_pad, 0.0),
                    jnp.where(mine & (row64 >= grp) & (row64 < grp + 3 * SUBLANES), 1.0, 0.0)], axis=-1).astype(_BF16)
                upd = _dot_tn(lhs, rhs)
                s_out_ref[sub * SUBLANES + s, hd] = s0 * upd[:, HEAD_DIM:] + upd[:, :HEAD_DIM]
            gate = jnp.concatenate([token_slab(pa_ref, 3 * A_HEADS + hd, t) for t in range(t_dec)], axis=0)
            on = _head_out(o + o_state, ag_ref[...], gate)
            for t in range(t_dec):
                oa_ref[hd, pl.ds(r0 + t, SUBLANES, stride=t_dec), :] = on[t * SUBLANES:(t + 1) * SUBLANES, :]

    for sub in range(DEC_GROUPS):
        group(sub)

    @pl.when(g == pl.num_programs(0) - 1)
    def _():
        def out_body(r, carry):
            rows = pl.ds(pl.multiple_of(r * ROW_TILE, ROW_TILE), ROW_TILE)
            cat = jnp.concatenate([oa_ref[hd, rows, :] for hd in range(A_HEADS)]
                                  + [ob_ref[ls, rows, :] for ls in range(n_ls)], axis=-1).astype(_BF16)
            out = _dot(cat, wout_ref[...])
            y_ref[rows, :] = _rmsnorm(x_ref[rows, :] + out, fg_ref[...])
            return carry
        lax.fori_loop(0, n_tok // ROW_TILE, out_body, 0)


def _decode_call(x, s0, c0, weights):
    n, t_dec, _ = x.shape
    assert n % (DEC_GROUPS * SUBLANES) == 0 and t_dec == 4 and (n * t_dec) % ROW_TILE == 0
    n_tok = n * t_dec
    x = x.reshape(n_tok, D_MODEL)
    masks = jnp.asarray(_level_masks((1, 2), t_dec, SUBLANES, same_group=t_dec))
    kern = functools.partial(_decode_kernel, t_dec=t_dec)
    w_specs = [_full(w.shape) for w in weights]
    per_step = DEC_GROUPS * SUBLANES
    state_spec = pl.BlockSpec((per_step, A_HEADS, HEAD_DIM, HEAD_DIM), lambda i: (i, 0, 0, 0))
    conv_spec = pl.BlockSpec((CONV_WIDTH - 1, per_step, B_WIDTH), lambda i: (0, i, 0))
    y, s_new, c_new = pl.pallas_call(
        kern,
        grid=(n // per_step,),
        in_specs=[_full((n_tok, D_MODEL)), state_spec, conv_spec] + w_specs + [_full(masks.shape)],
        out_specs=[_full((n_tok, D_MODEL)), state_spec, conv_spec],
        out_shape=[
            jax.ShapeDtypeStruct((n_tok, D_MODEL), _F32),
            jax.ShapeDtypeStruct((n, A_HEADS, HEAD_DIM, HEAD_DIM), _F32),
            jax.ShapeDtypeStruct((CONV_WIDTH - 1, n, B_WIDTH), _F32),
        ],
        scratch_shapes=[
            pltpu.VMEM((4 * A_HEADS, n_tok, LANES), _F32),
            pltpu.VMEM((B_WIDTH // LANES, n_tok, LANES), _F32),
            pltpu.VMEM((B_WIDTH // LANES, n_tok, LANES), _F32),
            pltpu.VMEM((A_HEADS, n_tok, LANES), _F32),
            pltpu.VMEM((B_WIDTH // LANES, n_tok, LANES), _F32),
        ],
        compiler_params=pltpu.CompilerParams(
            dimension_semantics=("arbitrary",),
            vmem_limit_bytes=VMEM_LIMIT_BYTES,
        ),
        name="hymba_decode",
    )(x, s0, c0, *weights, masks)
    return y.reshape(n, t_dec, D_MODEL), s_new, c_new


def _prepare_weights(norm_in_g, w_in, lb_logits, hgrn_norm_g, b_glu, conv_w, conv_b, ln_g, ln_b, w_out, final_norm_g, layer):
    lb = jnp.cumsum(jax.nn.softmax(lb_logits.astype(_F32), axis=0), axis=0)[layer]
    row = lambda a: a.astype(_F32).reshape(1, -1)
    return (
        row(norm_in_g[layer]),
        w_in[layer].astype(_BF16),
        row(lb),
        row(hgrn_norm_g[layer]),
        row(b_glu[layer]),
        conv_w[layer].astype(_F32),
        row(conv_b[layer]),
        row(ln_g[layer]),
        row(ln_b[layer]),
        w_out[layer].astype(_BF16),
        row(final_norm_g),
    )


def kernel(x_prompt, x_sample, state_hgrn, state_conv, norm_in_g, w_in, lb_logits, hgrn_norm_g, b_glu, conv_w,
           conv_b, ln_g, ln_b, w_out, final_norm_g):
    depth = w_in.shape[0]
    assert depth == 1, "single mixer layer: the final norm is fused into the layer kernel"
    weights = _prepare_weights(norm_in_g, w_in, lb_logits, hgrn_norm_g, b_glu, conv_w, conv_b, ln_g, ln_b, w_out,
                               final_norm_g, 0)
    y_p, s_p, c_p = _prompt_call(x_prompt, weights, tb=1024)
    y_s, s_s, c_s = _decode_call(x_sample, state_hgrn[0], jnp.transpose(state_conv[0], (1, 0, 2)), weights)
    return (y_p, y_s, s_p[None], c_p[None], s_s[None], jnp.transpose(c_s, (1, 0, 2))[None])
```

```python
import functools
import math

import numpy as np
import jax
import jax.numpy as jnp
from jax import lax
from jax.experimental import pallas as pl
from jax.experimental.pallas import tpu as pltpu

D_MODEL = 1024
A_HEADS = 4
HEAD_DIM = 128
A_WIDTH = A_HEADS * HEAD_DIM
B_WIDTH = 512
CONV_WIDTH = 31
EPS = 1e-6
IN_WIDTH = 4 * A_WIDTH + 3 * B_WIDTH
GLU_OFF = 4 * A_WIDTH
ZB_OFF = GLU_OFF + 2 * B_WIDTH

LANES = 128
SUBLANES = 8
MXU_N = 256
CHUNK = SUBLANES * SUBLANES
ROW_TILE = 128
UNROLL = 8
CHUNK_UNROLL = 2
CHUNK_LAG = 4
CONV_BUFS = 3
DEC_GROUPS = 2
HIST = 32
FIRST = HIST - (CONV_WIDTH - 1)
VMEM_LIMIT_BYTES = 56 * 1024 * 1024
WEIGHT_ROWS = 64
LOG2E = math.log2(math.e)

_F32 = jnp.float32
_BF16 = jnp.bfloat16


def _sigmoid(x):
    return 1.0 / (1.0 + jnp.exp(-x))


def _silu(x):
    return x * _sigmoid(x)


def _dot(a, b):
    return jnp.dot(a, b, preferred_element_type=_F32)


def _dot_nt(a, b):
    return lax.dot_general(a, b, (((1,), (1,)), ((), ())), preferred_element_type=_F32)


def _dot_tn(a, b):
    return lax.dot_general(a, b, (((0,), (0,)), ((), ())), preferred_element_type=_F32)


def _level_masks(levels, n_res, n_rows, same_group=None):
    r = np.arange(n_res * n_rows)
    t = n_res * (r % n_rows) + r // n_rows
    tt, ss = t[:, None], t[None, :]
    masks = [tt == ss]
    for b in levels:
        masks.append(((tt // b) == (ss // b) + 1) & ((ss // b) % 2 == 0))
    masks = np.stack(masks)
    want = (ss <= tt) if same_group is None else ((ss <= tt) & (tt // same_group == ss // same_group))
    assert (masks.sum(0) == want).all()
    return masks.astype(np.float32)


def _rows(x, r):
    return jnp.broadcast_to(x[r:r + 1, :], x.shape)


def _hgrn_stages(q, f, v, masks_ref, get_st, coarse, done):
    n_res = len(q)
    every = range(n_res)
    k = [1.0 - fi for fi in f]
    lf = [jnp.log(fi) * LOG2E for fi in f]
    g_in = [lf[0]]
    for i in range(1, n_res):
        g_in.append(g_in[-1] + lf[i])
    tot = g_in[-1]
    zero = jnp.zeros_like(tot)

    def cat(parts):
        return jnp.concatenate(parts, axis=0).astype(_BF16)

    def slab_rows(x, i):
        return x[i * SUBLANES:(i + 1) * SUBLANES, :]

    v_all = cat(v)
    s_rows = [None] * n_res
    pending = []

    def fold():
        level, rows, s_l = pending.pop(0)
        for n, i in enumerate(rows):
            term = masks_ref[level, i * SUBLANES:(i + 1) * SUBLANES, :] * slab_rows(s_l, n)
            s_rows[i] = term if s_rows[i] is None else s_rows[i] + term

    def step(rows, qt, kt):
        pending.append((step.level, rows, _dot_nt(cat(qt), cat(kt))))
        step.level += 1
    step.level = 0

    step(every, q, k)
    yield
    b = 1
    while b < n_res:
        odd = [i for i in every if (i // b) % 2 == 1]
        qt, kt = [], []
        for i in every:
            bs = (i // b) * b
            if i in odd:
                qt.append(q[i] * jnp.exp2(g_in[i] - g_in[bs - 1]))
                kt.append(zero)
            else:
                be = bs + b - 1
                kt.append(k[i] * jnp.exp2(g_in[be] - g_in[i]) if i != be else k[i])
        step(odd, qt, kt)
        yield
        fold()
        b *= 2
    qe = [q[i] * jnp.exp2(g_in[i]) for i in every]
    ke = [k[i] * jnp.exp2(tot - g_in[i]) for i in range(n_res - 1)] + [k[-1]]
    if coarse:
        row = lax.broadcasted_iota(jnp.int32, tot.shape, 0)
        pref = tot
        for sh in (1, 2, 4):
            pref = pref + jnp.where(row >= sh, pltpu.roll(pref, sh, 0), 0.0)
        before = pref - tot
        end_all = _rows(pref, SUBLANES - 1)
        qe_b, ke_b = cat(qe), cat(ke)
        tile = lambda x: jnp.concatenate([x] * n_res, axis=0).astype(_BF16)
        for gsz in (1, 2, 4):
            if gsz == 1:
                qt, kt = qe_b, ke_b
            else:
                if gsz == 2:
                    g_start = jnp.where(row % 2 == 0, before, pltpu.roll(before, 1, 0))
                    g_end = jnp.where(row % 2 == 1, pref, pltpu.roll(pref, SUBLANES - 1, 0))
                else:
                    g_start = jnp.where(row < 4, _rows(before, 0), _rows(before, 4))
                    g_end = jnp.where(row < 4, _rows(pref, 3), _rows(pref, 7))
                qt = qe_b * tile(jnp.exp2(before - g_start))
                kt = ke_b * tile(jnp.exp2(g_end - pref))
            pending.append((step.level, every, _dot_nt(qt, kt)))
            step.level += 1
            yield
            fold()
        qc = qe_b * tile(jnp.exp2(before))
        kc = ke_b * tile(jnp.exp2(end_all - pref))
        o_st = _dot_nt(qc, get_st().astype(_BF16))
        decay = jnp.exp2(pref[SUBLANES - 1:SUBLANES, :])
        yield
    else:
        qc, kc = cat(qe), cat(ke)
        o_st = None
        decay = jnp.exp2(tot)
    while pending:
        fold()
    o = _dot(cat(s_rows), v_all)
    yield
    done(o if o_st is None else o + o_st, qc, kc, v_all, decay)


def _round_robin(starts):
    waiting = list(starts)
    live = []
    rnd = 0
    while waiting or live:
        if waiting and rnd % CHUNK_LAG == 0:
            live += waiting.pop(0)
        for g in list(live):
            try:
                next(g)
            except StopIteration:
                live.remove(g)
        rnd += 1


def _aligned(x, m):
    return x if isinstance(x, int) else pl.multiple_of(x, m)


def _loop(trips, body):
    if trips == 1:
        body(0, 0)
    else:
        lax.fori_loop(0, trips, body, 0)


def _head_out(o, ag, gate):
    ms = jnp.mean(o * o, axis=-1, keepdims=True)
    return o * lax.rsqrt(ms + EPS) * ag * gate


def _project_a(h, win_ref, lb_ref, store, cg):
    p = _dot(h(), win_ref[:, cg * MXU_N:(cg + 1) * MXU_N])
    arr = cg // 2
    if arr == 0 or arr == 3:
        p = _silu(p)
    elif arr == 1:
        lb = lb_ref[:, (cg % 2) * MXU_N:(cg % 2 + 1) * MXU_N]
        p = lb + (1.0 - lb) * _sigmoid(p)
    store(2 * cg, p[:, :LANES])
    store(2 * cg + 1, p[:, LANES:])


def _project_b(h, win_ref, bglu_ref, half):
    c0 = half * MXU_N
    glu_a = _dot(h(), win_ref[:, GLU_OFF + c0:GLU_OFF + c0 + MXU_N]) + bglu_ref[:, c0:c0 + MXU_N]
    glu_b = (_dot(h(), win_ref[:, GLU_OFF + B_WIDTH + c0:GLU_OFF + B_WIDTH + c0 + MXU_N])
             + bglu_ref[:, B_WIDTH + c0:B_WIDTH + c0 + MXU_N])
    zb = _dot(h(), win_ref[:, ZB_OFF + c0:ZB_OFF + c0 + MXU_N])
    return glu_a * _sigmoid(glu_b), _silu(zb)


def _rmsnorm(x, g):
    return x * lax.rsqrt(jnp.mean(x * x, axis=-1, keepdims=True) + EPS) * g


def _group_b_out(cv, lng_ref, lnb_ref, gate_b):
    mu = jnp.mean(cv, axis=-1, keepdims=True)
    d = cv - mu
    var = jnp.mean(d * d, axis=-1, keepdims=True)
    ln = d * lax.rsqrt(var + EPS) * lng_ref[...] + lnb_ref[...]
    return _silu(ln) * gate_b


def _cast_weight(src_hbm, dst_ref, stage_ref, sems):
    n_rows, width = dst_ref.shape

    def piece(c):
        return pltpu.make_async_copy(src_hbm.at[pl.ds(c * WEIGHT_ROWS, WEIGHT_ROWS), :],
                                     stage_ref.at[c % 2, :, pl.ds(0, width)], sems.at[c % 2])

    n = n_rows // WEIGHT_ROWS
    piece(0).start()
    for c in range(n):
        if c + 1 < n:
            piece(c + 1).start()
        piece(c).wait()
        dst_ref[c * WEIGHT_ROWS:(c + 1) * WEIGHT_ROWS, :] = stage_ref[c % 2, :, 0:width].astype(_BF16)


def _conv_stages(ua_ref, ub_ref, wb_ref, convb_ref, lng_ref, lnb_ref, gb_ref, cv_ref, cat_ref, base, n):
    reps = lambda w, rows: jnp.concatenate([w] * (rows // (2 * SUBLANES)), axis=0)
    for ls in range(B_WIDTH // LANES):
        lanes = slice(ls * LANES, (ls + 1) * LANES)
        acc = jnp.broadcast_to(convb_ref[:, lanes], (n, LANES))
        for off in range(SUBLANES):
            taps = [j for j in range(CONV_WIDTH) if (FIRST + j) % SUBLANES == off]
            rows = n + (2 * SUBLANES if off else 0)
            part = None
            for j in taps:
                rel = FIRST + j - off
                if rel % (2 * SUBLANES) == 0:
                    win = ua_ref[pl.ds(_aligned(base + rel, 2 * SUBLANES), rows), lanes]
                else:
                    win = ub_ref[pl.ds(_aligned(base + rel + SUBLANES, 2 * SUBLANES), rows), lanes]
                term = reps(wb_ref[j, :, lanes], rows) * win
                part = term if part is None else part + term
            part = part.astype(_F32)
            acc = acc + (part[off:off + n, :] if off else part)
            if off == SUBLANES // 2 - 1:
                yield
        cv_ref[:, lanes] = acc
        yield
    rows = pl.ds(_aligned(base, n), n)
    o_b = _group_b_out(cv_ref[...], lng_ref, lnb_ref, gb_ref[rows, :])
    cat_ref[rows, A_WIDTH:] = o_b.astype(_BF16)


def _prompt_kernel(x_ref, normg_ref, win_hbm, lb_ref, ag_ref, bglu_ref, convw_ref, convb_ref,
                   lng_ref, lnb_ref, wout_hbm, fg_ref, masks_ref, wb_ref,
                   y_ref, st_out_ref, cs_out_ref, win_out_hbm, wout_out_hbm,
                   pa_ref, uh_ref, ua_ref, ub_ref, gb_ref, cv_ref, oa_ref, cat_ref, st_ref,
                   win_ref, wout_ref, stage_ref, sems, *, tb):
    t = pl.program_id(1)
    nt = pl.num_programs(1)

    first = (pl.program_id(0) == 0) & (t == 0)
    handoff = [pltpu.make_async_copy(win_ref, win_out_hbm, sems.at[2]),
               pltpu.make_async_copy(wout_ref, wout_out_hbm, sems.at[3])]

    @pl.when(first)
    def _():
        _cast_weight(win_hbm, win_ref, stage_ref, sems)
        _cast_weight(wout_hbm, wout_ref, stage_ref, sems)
        for copy in handoff:
            copy.start()

    @pl.when(t == 0)
    def _():
        st_ref[...] = jnp.zeros_like(st_ref)
        uh_ref[...] = jnp.zeros_like(uh_ref)
        ua_ref[...] = jnp.zeros_like(ua_ref)
        ub_ref[...] = jnp.zeros_like(ub_ref)

    def front_tile(r0):
        rows = pl.ds(r0, ROW_TILE)
        h = _rmsnorm(x_ref[rows, :], normg_ref[...]).astype(_BF16)
        for half in range(B_WIDTH // MXU_N):
            u, gb = _project_b(lambda: h, win_ref, bglu_ref, half)
            cols = slice(half * MXU_N, (half + 1) * MXU_N)
            here = pl.ds(_aligned(HIST + r0, HIST), ROW_TILE)
            prev = uh_ref[pl.ds(_aligned(HIST + r0 - SUBLANES, SUBLANES), SUBLANES), cols]
            uh_ref[here, cols] = u
            ua_ref[here, cols] = u.astype(_BF16)
            ub_ref[here, cols] = jnp.concatenate([prev, u[:ROW_TILE - SUBLANES, :]], axis=0).astype(_BF16)
            gb_ref[rows, cols] = gb

        def store_a(slab, val):
            pa_ref[slab, rows, :] = val
        for cg in range(2 * A_HEADS):
            _project_a(lambda: h, win_ref, lb_ref, store_a, cg)

    def front_body(p, carry):
        for n in range(UNROLL):
            front_tile(_aligned((UNROLL * p + n) * ROW_TILE, ROW_TILE))
        return carry

    _loop(tb // (UNROLL * ROW_TILE), front_body)
    ub_ref[HIST + tb:HIST + tb + 2 * SUBLANES, :] = jnp.concatenate(
        [uh_ref[HIST + tb - SUBLANES:HIST + tb, :], jnp.zeros((SUBLANES, B_WIDTH), _F32)], axis=0).astype(_BF16)

    def chunk(base, slot):
        def head(hd):
            def slabs(arr):
                return [pa_ref[arr * A_HEADS + hd, pl.ds(base + i, SUBLANES, stride=SUBLANES), :]
                        for i in range(SUBLANES)]

            def done(o, qc, kc, v_all, decay):
                st_ref[hd] = st_ref[hd] * decay + _dot_tn(v_all, kc)
                for i in range(SUBLANES):
                    oa_ref[hd, pl.ds(base + i, SUBLANES, stride=SUBLANES), :] = o[i * SUBLANES:(i + 1) * SUBLANES, :]
            return _hgrn_stages(slabs(0), slabs(1), slabs(2), masks_ref, lambda: st_ref[hd], True, done)
        conv = _conv_stages(ua_ref, ub_ref, wb_ref, convb_ref, lng_ref, lnb_ref, gb_ref, cv_ref.at[slot], cat_ref, base, CHUNK)
        return [head(hd) for hd in range(A_HEADS)] + [conv]

    def chunk_body(p, carry):
        _round_robin([chunk(_aligned((CHUNK_UNROLL * p + n) * CHUNK, CHUNK), n % CONV_BUFS) for n in range(CHUNK_UNROLL)])
        return carry

    _loop(tb // (CHUNK_UNROLL * CHUNK), chunk_body)

    @pl.when(t == nt - 1)
    def _():
        cs_out_ref[...] = uh_ref[HIST + tb - (CONV_WIDTH - 1):HIST + tb, :]
        for hd in range(A_HEADS):
            st_out_ref[hd] = st_ref[hd].T

    uh_ref[0:HIST, :] = uh_ref[tb:tb + HIST, :]
    ua_ref[0:HIST, :] = ua_ref[tb:tb + HIST, :]
    ub_ref[0:HIST, :] = ub_ref[tb:tb + HIST, :]

    def back_body(p, carry):
        tiles = [pl.ds(_aligned((UNROLL * p + n) * ROW_TILE, ROW_TILE), ROW_TILE) for n in range(UNROLL)]
        for rows in tiles:
            for hd in range(A_HEADS):
                on = _head_out(oa_ref[hd, rows, :], ag_ref[...], pa_ref[3 * A_HEADS + hd, rows, :])
                cat_ref[rows, hd * HEAD_DIM:(hd + 1) * HEAD_DIM] = on.astype(_BF16)
        outs = [_dot(cat_ref[rows, :], wout_ref[...]) for rows in tiles]
        for rows, out in zip(tiles, outs):
            y_ref[rows, :] = _rmsnorm(x_ref[rows, :] + out, fg_ref[...])
        return carry

    _loop(tb // (UNROLL * ROW_TILE), back_body)

    @pl.when(first)
    def _():
        for copy in handoff:
            copy.wait()


def _full(shape):
    nd = len(shape)
    return pl.BlockSpec(shape, lambda *_: (0,) * nd, pipeline_mode=pl.Buffered(1))


def _prompt_call(x, weights, tb):
    n, t, _ = x.shape
    assert t % tb == 0 and tb % (UNROLL * ROW_TILE) == 0 and tb % (CHUNK_UNROLL * CHUNK) == 0
    masks = jnp.asarray(_level_masks((1, 2, 4, 8, 16, 32), SUBLANES, SUBLANES))
    conv_w = weights[5]
    wb = jnp.broadcast_to(conv_w.astype(_BF16)[:, None, :], (CONV_WIDTH, 2 * SUBLANES, B_WIDTH))
    kern = functools.partial(_prompt_kernel, tb=tb)
    w_in, w_out = weights[1], weights[9]
    assert w_in.dtype == _F32 and w_out.dtype == _F32 and w_in.shape[0] % WEIGHT_ROWS == 0 and w_out.shape[0] % WEIGHT_ROWS == 0
    in_hbm = pl.BlockSpec(memory_space=pl.ANY)
    w_specs = [in_hbm if w is w_in or w is w_out else _full(w.shape) for w in weights]
    return pl.pallas_call(
        kern,
        grid=(n, t // tb),
        in_specs=([pl.BlockSpec((None, tb, D_MODEL), lambda i, j: (i, j, 0))] + w_specs
                  + [_full(masks.shape), _full(wb.shape)]),
        out_specs=[
            pl.BlockSpec((None, tb, D_MODEL), lambda i, j: (i, j, 0)),
            pl.BlockSpec((None, A_HEADS, HEAD_DIM, HEAD_DIM), lambda i, j: (i, 0, 0, 0)),
            pl.BlockSpec((None, CONV_WIDTH - 1, B_WIDTH), lambda i, j: (i, 0, 0)),
            in_hbm,
            in_hbm,
        ],
        out_shape=[
            jax.ShapeDtypeStruct((n, t, D_MODEL), _F32),
            jax.ShapeDtypeStruct((n, A_HEADS, HEAD_DIM, HEAD_DIM), _F32),
            jax.ShapeDtypeStruct((n, CONV_WIDTH - 1, B_WIDTH), _F32),
            jax.ShapeDtypeStruct(w_in.shape, _BF16),
            jax.ShapeDtypeStruct(w_out.shape, _BF16),
        ],
        scratch_shapes=[
            pltpu.VMEM((4 * A_HEADS, tb, LANES), _F32),
            pltpu.VMEM((HIST + tb, B_WIDTH), _F32),
            pltpu.VMEM((HIST + tb + 2 * SUBLANES, B_WIDTH), _BF16),
            pltpu.VMEM((HIST + tb + 2 * SUBLANES, B_WIDTH), _BF16),
            pltpu.VMEM((tb, B_WIDTH), _F32),
            pltpu.VMEM((CONV_BUFS, CHUNK, B_WIDTH), _F32),
            pltpu.VMEM((A_HEADS, tb, LANES), _F32),
            pltpu.VMEM((tb, A_WIDTH + B_WIDTH), _BF16),
            pltpu.VMEM((A_HEADS, HEAD_DIM, HEAD_DIM), _F32),
            pltpu.VMEM(w_in.shape, _BF16),
            pltpu.VMEM(w_out.shape, _BF16),
            pltpu.VMEM((2, WEIGHT_ROWS, max(w_in.shape[1], w_out.shape[1])), _F32),
            pltpu.SemaphoreType.DMA((4,)),
        ],
        compiler_params=pltpu.CompilerParams(
            dimension_semantics=("arbitrary", "arbitrary"),
            vmem_limit_bytes=VMEM_LIMIT_BYTES,
        ),
        name="hymba_prompt",
    )(x, *weights, masks, wb)


def _decode_kernel(x_ref, s0_ref, c0_ref, normg_ref, win_ref, lb_ref, ag_ref, bglu_ref, convw_ref, convb_ref,
                   lng_ref, lnb_ref, wout_ref, fg_ref, masks_ref,
                   y_ref, s_out_ref, cs_out_ref,
                   pa_ref, ub_ref, gb_ref, oa_ref, ob_ref, *, t_dec):
    g = pl.program_id(0)
    n_tok = x_ref.shape[0]
    grp = SUBLANES * t_dec
    n_ls = B_WIDTH // LANES

    @pl.when(g == 0)
    def _():
        def proj_body(r, carry):
            rows = pl.ds(pl.multiple_of(r * ROW_TILE, ROW_TILE), ROW_TILE)
            h = _rmsnorm(x_ref[rows, :], normg_ref[...]).astype(_BF16)
            for half in range(B_WIDTH // MXU_N):
                u, gb = _project_b(lambda: h, win_ref, bglu_ref, half)
                for c in range(MXU_N // LANES):
                    ls = half * (MXU_N // LANES) + c
                    ub_ref[ls, rows, :] = u[:, c * LANES:(c + 1) * LANES]
                    gb_ref[ls, rows, :] = gb[:, c * LANES:(c + 1) * LANES]

            def store_a(slab, val):
                pa_ref[slab, rows, :] = val
            for cg in range(2 * A_HEADS):
                _project_a(lambda: h, win_ref, lb_ref, store_a, cg)
            return carry
        lax.fori_loop(0, n_tok // ROW_TILE, proj_body, 0)

    def group(sub):
        seqs = slice(sub * SUBLANES, (sub + 1) * SUBLANES)
        r0 = pl.multiple_of((g * DEC_GROUPS + sub) * grp, grp)

        def token_slab(ref, idx, t):
            return ref[idx, pl.ds(r0 + t, SUBLANES, stride=t_dec), :]

        u_tok = [jnp.concatenate([token_slab(ub_ref, ls, t) for ls in range(n_ls)], axis=-1) for t in range(t_dec)]
        full = lambda kk: c0_ref[kk, seqs, :] if kk < CONV_WIDTH - 1 else u_tok[kk - (CONV_WIDTH - 1)]
        for kk in range(CONV_WIDTH - 1):
            cs_out_ref[kk, seqs, :] = full(kk + t_dec)
        for t in range(t_dec):
            cv = jnp.broadcast_to(convb_ref[...], (SUBLANES, B_WIDTH))
            for j in range(CONV_WIDTH):
                cv = cv + convw_ref[j:j + 1, :] * full(t + j)
            gate_b = jnp.concatenate([token_slab(gb_ref, ls, t) for ls in range(n_ls)], axis=-1)
            o_b = _group_b_out(cv, lng_ref, lnb_ref, gate_b)
            for ls in range(n_ls):
                ob_ref[ls, pl.ds(r0 + t, SUBLANES, stride=t_dec), :] = o_b[:, ls * LANES:(ls + 1) * LANES]

        row32 = lax.broadcasted_iota(jnp.int32, (grp, LANES), 0) % SUBLANES
        row64 = lax.broadcasted_iota(jnp.int32, (2 * grp, LANES), 0)
        for hd in range(A_HEADS):
            slabs = lambda arr: [token_slab(pa_ref, arr * A_HEADS + hd, t) for t in range(t_dec)]
            res = []
            for _ in _hgrn_stages(slabs(0), slabs(1), slabs(2), masks_ref, None, False, lambda *a: res.extend(a)):
                pass
            o, qc, kc, v_all, decay = res
            qf, kf, vf = qc.astype(_F32), kc.astype(_F32), v_all.astype(_F32)
            e_hi = decay.astype(_BF16).astype(_F32)
            e_mid = (decay - e_hi).astype(_BF16).astype(_F32)
            e_lo = (decay - e_hi - e_mid).astype(_BF16).astype(_F32)
            lhs = jnp.concatenate([kf, e_hi, e_mid, e_lo, jnp.zeros((SUBLANES, LANES), _F32)], axis=0).astype(_BF16)
            v_pad = jnp.concatenate([vf, jnp.zeros((grp, LANES), _F32)], axis=0)
            o_state = None
            for s in range(SUBLANES):
                s0 = s0_ref[sub * SUBLANES + s, hd]
                term = _dot(jnp.where(row32 == s, qf, 0.0).astype(_BF16), s0.astype(_BF16))
                o_state = term if o_state is None else o_state + term
                mine = row64 % SUBLANES == s
                rhs = jnp.concatenate([
                    jnp.where(mine & (row64 < grp), v_pad, 0.0),
                    jnp.where(mine & (row64 >= grp) & (row64 < grp + 3 * SUBLANES), 1.0, 0.0)], axis=-1).astype(_BF16)
                upd = _dot_tn(lhs, rhs)
                s_out_ref[sub * SUBLANES + s, hd] = s0 * upd[:, HEAD_DIM:] + upd[:, :HEAD_DIM]
            gate = jnp.concatenate([token_slab(pa_ref, 3 * A_HEADS + hd, t) for t in range(t_dec)], axis=0)
            on = _head_out(o + o_state, ag_ref[...], gate)
            for t in range(t_dec):
                oa_ref[hd, pl.ds(r0 + t, SUBLANES, stride=t_dec), :] = on[t * SUBLANES:(t + 1) * SUBLANES, :]

    for sub in range(DEC_GROUPS):
        group(sub)

    @pl.when(g == pl.num_programs(0) - 1)
    def _():
        def out_body(r, carry):
            rows = pl.ds(pl.multiple_of(r * ROW_TILE, ROW_TILE), ROW_TILE)
            cat = jnp.concatenate([oa_ref[hd, rows, :] for hd in range(A_HEADS)]
                                  + [ob_ref[ls, rows, :] for ls in range(n_ls)], axis=-1).astype(_BF16)
            out = _dot(cat, wout_ref[...])
            y_ref[rows, :] = _rmsnorm(x_ref[rows, :] + out, fg_ref[...])
            return carry
        lax.fori_loop(0, n_tok // ROW_TILE, out_body, 0)


def _decode_call(x, s0, c0, weights):
    n, t_dec, _ = x.shape
    assert n % (DEC_GROUPS * SUBLANES) == 0 and t_dec == 4 and (n * t_dec) % ROW_TILE == 0
    n_tok = n * t_dec
    x = x.reshape(n_tok, D_MODEL)
    masks = jnp.asarray(_level_masks((1, 2), t_dec, SUBLANES, same_group=t_dec))
    kern = functools.partial(_decode_kernel, t_dec=t_dec)
    w_specs = [_full(w.shape) for w in weights]
    per_step = DEC_GROUPS * SUBLANES
    state_spec = pl.BlockSpec((per_step, A_HEADS, HEAD_DIM, HEAD_DIM), lambda i: (i, 0, 0, 0))
    conv_spec = pl.BlockSpec((CONV_WIDTH - 1, per_step, B_WIDTH), lambda i: (0, i, 0))
    y, s_new, c_new = pl.pallas_call(
        kern,
        grid=(n // per_step,),
        in_specs=[_full((n_tok, D_MODEL)), state_spec, conv_spec] + w_specs + [_full(masks.shape)],
        out_specs=[_full((n_tok, D_MODEL)), state_spec, conv_spec],
        out_shape=[
            jax.ShapeDtypeStruct((n_tok, D_MODEL), _F32),
            jax.ShapeDtypeStruct((n, A_HEADS, HEAD_DIM, HEAD_DIM), _F32),
            jax.ShapeDtypeStruct((CONV_WIDTH - 1, n, B_WIDTH), _F32),
        ],
        scratch_shapes=[
            pltpu.VMEM((4 * A_HEADS, n_tok, LANES), _F32),
            pltpu.VMEM((B_WIDTH // LANES, n_tok, LANES), _F32),
            pltpu.VMEM((B_WIDTH // LANES, n_tok, LANES), _F32),
            pltpu.VMEM((A_HEADS, n_tok, LANES), _F32),
            pltpu.VMEM((B_WIDTH // LANES, n_tok, LANES), _F32),
        ],
        compiler_params=pltpu.CompilerParams(
            dimension_semantics=("arbitrary",),
            vmem_limit_bytes=VMEM_LIMIT_BYTES,
        ),
        name="hymba_decode",
    )(x, s0, c0, *weights, masks)
    return y.reshape(n, t_dec, D_MODEL), s_new, c_new


def _prepare_weights(norm_in_g, w_in, lb_logits, hgrn_norm_g, b_glu, conv_w, conv_b, ln_g, ln_b, w_out, final_norm_g, layer):
    lb = jnp.cumsum(jax.nn.softmax(lb_logits.astype(_F32), axis=0), axis=0)[layer]
    row = lambda a: a.astype(_F32).reshape(1, -1)
    return (
        row(norm_in_g[layer]),
        w_in[layer].astype(_F32),
        row(lb),
        row(hgrn_norm_g[layer]),
        row(b_glu[layer]),
        conv_w[layer].astype(_F32),
        row(conv_b[layer]),
        row(ln_g[layer]),
        row(ln_b[layer]),
        w_out[layer].astype(_F32),
        row(final_norm_g),
    )


def kernel(x_prompt, x_sample, state_hgrn, state_conv, norm_in_g, w_in, lb_logits, hgrn_norm_g, b_glu, conv_w,
           conv_b, ln_g, ln_b, w_out, final_norm_g):
    depth = w_in.shape[0]
    assert depth == 1, "single mixer layer: the final norm is fused into the layer kernel"
    weights = _prepare_weights(norm_in_g, w_in, lb_logits, hgrn_norm_g, b_glu, conv_w, conv_b, ln_g, ln_b, w_out,
                               final_norm_g, 0)
    y_p, s_p, c_p, w_in_b, w_out_b = _prompt_call(x_prompt, weights, tb=1024)
    weights = weights[:1] + (w_in_b,) + weights[2:9] + (w_out_b,) + weights[10:]
    y_s, s_s, c_s = _decode_call(x_sample, state_hgrn[0], jnp.transpose(state_conv[0], (1, 0, 2)), weights)
    return (y_p, y_s, s_p[None], c_p[None], s_s[None], jnp.transpose(c_s, (1, 0, 2))[None])
```

```python
import functools
import math

import numpy as np
import jax
import jax.numpy as jnp
from jax import lax
from jax.experimental import pallas as pl
from jax.experimental.pallas import tpu as pltpu

D_MODEL = 1024
A_HEADS = 4
HEAD_DIM = 128
A_WIDTH = A_HEADS * HEAD_DIM
B_WIDTH = 512
CONV_WIDTH = 31
EPS = 1e-6
IN_WIDTH = 4 * A_WIDTH + 3 * B_WIDTH
GLU_OFF = 4 * A_WIDTH
ZB_OFF = GLU_OFF + 2 * B_WIDTH

LANES = 128
SUBLANES = 8
MXU_N = 256
CHUNK = SUBLANES * SUBLANES
ROW_TILE = 128
UNROLL = 8
CHUNK_UNROLL = 2
CHUNK_LAG = 4
CONV_BUFS = 3
DEC_GROUPS = 2
HIST = 32
FIRST = HIST - (CONV_WIDTH - 1)
VMEM_LIMIT_BYTES = 56 * 1024 * 1024
WEIGHT_ROWS = 32
WEIGHT_SLOTS = 8
LOG2E = math.log2(math.e)

_F32 = jnp.float32
_BF16 = jnp.bfloat16


def _sigmoid(x):
    return 1.0 / (1.0 + jnp.exp(-x))


def _silu(x):
    return x * _sigmoid(x)


def _dot(a, b):
    return jnp.dot(a, b, preferred_element_type=_F32)


def _dot_nt(a, b):
    return lax.dot_general(a, b, (((1,), (1,)), ((), ())), preferred_element_type=_F32)


def _dot_tn(a, b):
    return lax.dot_general(a, b, (((0,), (0,)), ((), ())), preferred_element_type=_F32)


def _level_masks(levels, n_res, n_rows, same_group=None):
    r = np.arange(n_res * n_rows)
    t = n_res * (r % n_rows) + r // n_rows
    tt, ss = t[:, None], t[None, :]
    masks = [tt == ss]
    for b in levels:
        masks.append(((tt // b) == (ss // b) + 1) & ((ss // b) % 2 == 0))
    masks = np.stack(masks)
    want = (ss <= tt) if same_group is None else ((ss <= tt) & (tt // same_group == ss // same_group))
    assert (masks.sum(0) == want).all()
    return masks.astype(np.float32)


def _rows(x, r):
    return jnp.broadcast_to(x[r:r + 1, :], x.shape)


def _hgrn_stages(q, f, v, masks_ref, get_st, coarse, done):
    n_res = len(q)
    every = range(n_res)
    k = [1.0 - fi for fi in f]
    lf = [jnp.log(fi) * LOG2E for fi in f]
    g_in = [lf[0]]
    for i in range(1, n_res):
        g_in.append(g_in[-1] + lf[i])
    tot = g_in[-1]
    zero = jnp.zeros_like(tot)

    def cat(parts):
        return jnp.concatenate(parts, axis=0).astype(_BF16)

    def slab_rows(x, i):
        return x[i * SUBLANES:(i + 1) * SUBLANES, :]

    v_all = cat(v)
    s_rows = [None] * n_res
    pending = []

    def fold():
        level, rows, s_l = pending.pop(0)
        for n, i in enumerate(rows):
            term = masks_ref[level, i * SUBLANES:(i + 1) * SUBLANES, :] * slab_rows(s_l, n)
            s_rows[i] = term if s_rows[i] is None else s_rows[i] + term

    def step(rows, qt, kt):
        pending.append((step.level, rows, _dot_nt(cat(qt), cat(kt))))
        step.level += 1
    step.level = 0

    step(every, q, k)
    yield
    b = 1
    while b < n_res:
        odd = [i for i in every if (i // b) % 2 == 1]
        qt, kt = [], []
        for i in every:
            bs = (i // b) * b
            if i in odd:
                qt.append(q[i] * jnp.exp2(g_in[i] - g_in[bs - 1]))
                kt.append(zero)
            else:
                be = bs + b - 1
                kt.append(k[i] * jnp.exp2(g_in[be] - g_in[i]) if i != be else k[i])
        step(odd, qt, kt)
        yield
        fold()
        b *= 2
    qe = [q[i] * jnp.exp2(g_in[i]) for i in every]
    ke = [k[i] * jnp.exp2(tot - g_in[i]) for i in range(n_res - 1)] + [k[-1]]
    if coarse:
        row = lax.broadcasted_iota(jnp.int32, tot.shape, 0)
        pref = tot
        for sh in (1, 2, 4):
            pref = pref + jnp.where(row >= sh, pltpu.roll(pref, sh, 0), 0.0)
        before = pref - tot
        end_all = _rows(pref, SUBLANES - 1)
        qe_b, ke_b = cat(qe), cat(ke)
        tile = lambda x: jnp.concatenate([x] * n_res, axis=0).astype(_BF16)
        for gsz in (1, 2, 4):
            if gsz == 1:
                qt, kt = qe_b, ke_b
            else:
                if gsz == 2:
                    g_start = jnp.where(row % 2 == 0, before, pltpu.roll(before, 1, 0))
                    g_end = jnp.where(row % 2 == 1, pref, pltpu.roll(pref, SUBLANES - 1, 0))
                else:
                    g_start = jnp.where(row < 4, _rows(before, 0), _rows(before, 4))
                    g_end = jnp.where(row < 4, _rows(pref, 3), _rows(pref, 7))
                qt = qe_b * tile(jnp.exp2(before - g_start))
                kt = ke_b * tile(jnp.exp2(g_end - pref))
            pending.append((step.level, every, _dot_nt(qt, kt)))
            step.level += 1
            yield
            fold()
        qc = qe_b * tile(jnp.exp2(before))
        kc = ke_b * tile(jnp.exp2(end_all - pref))
        o_st = _dot_nt(qc, get_st().astype(_BF16))
        decay = jnp.exp2(pref[SUBLANES - 1:SUBLANES, :])
        yield
    else:
        qc, kc = cat(qe), cat(ke)
        o_st = None
        decay = jnp.exp2(tot)
    while pending:
        fold()
    o = _dot(cat(s_rows), v_all)
    yield
    done(o if o_st is None else o + o_st, qc, kc, v_all, decay)


def _round_robin(starts):
    waiting = list(starts)
    live = []
    rnd = 0
    while waiting or live:
        if waiting and rnd % CHUNK_LAG == 0:
            live += waiting.pop(0)
        for g in list(live):
            try:
                next(g)
            except StopIteration:
                live.remove(g)
        rnd += 1


def _aligned(x, m):
    return x if isinstance(x, int) else pl.multiple_of(x, m)


def _loop(trips, body):
    if trips == 1:
        body(0, 0)
    else:
        lax.fori_loop(0, trips, body, 0)


def _head_out(o, ag, gate):
    ms = jnp.mean(o * o, axis=-1, keepdims=True)
    return o * lax.rsqrt(ms + EPS) * ag * gate


def _project_a(h, win_ref, lb_ref, store, cg):
    p = _dot(h(), win_ref[:, cg * MXU_N:(cg + 1) * MXU_N])
    arr = cg // 2
    if arr == 0 or arr == 3:
        p = _silu(p)
    elif arr == 1:
        lb = lb_ref[:, (cg % 2) * MXU_N:(cg % 2 + 1) * MXU_N]
        p = lb + (1.0 - lb) * _sigmoid(p)
    store(2 * cg, p[:, :LANES])
    store(2 * cg + 1, p[:, LANES:])


def _project_b(h, win_ref, bglu_ref, half):
    c0 = half * MXU_N
    glu_a = _dot(h(), win_ref[:, GLU_OFF + c0:GLU_OFF + c0 + MXU_N]) + bglu_ref[:, c0:c0 + MXU_N]
    glu_b = (_dot(h(), win_ref[:, GLU_OFF + B_WIDTH + c0:GLU_OFF + B_WIDTH + c0 + MXU_N])
             + bglu_ref[:, B_WIDTH + c0:B_WIDTH + c0 + MXU_N])
    zb = _dot(h(), win_ref[:, ZB_OFF + c0:ZB_OFF + c0 + MXU_N])
    return glu_a * _sigmoid(glu_b), _silu(zb)


def _rmsnorm(x, g):
    return x * lax.rsqrt(jnp.mean(x * x, axis=-1, keepdims=True) + EPS) * g


def _group_b_out(cv, lng_ref, lnb_ref, gate_b):
    mu = jnp.mean(cv, axis=-1, keepdims=True)
    d = cv - mu
    var = jnp.mean(d * d, axis=-1, keepdims=True)
    ln = d * lax.rsqrt(var + EPS) * lng_ref[...] + lnb_ref[...]
    return _silu(ln) * gate_b


def _cast_weights(pairs, stage_ref, sems):
    pieces = [(src, dst, c) for src, dst in pairs for c in range(dst.shape[0] // WEIGHT_ROWS)]

    def copy(k):
        src, dst, c = pieces[k]
        return pltpu.make_async_copy(src.at[pl.ds(c * WEIGHT_ROWS, WEIGHT_ROWS), :],
                                     stage_ref.at[k % WEIGHT_SLOTS, :, pl.ds(0, dst.shape[1])], sems.at[k % WEIGHT_SLOTS])

    ahead = WEIGHT_SLOTS - 1
    for k in range(min(ahead, len(pieces))):
        copy(k).start()
    for k, (_, dst, c) in enumerate(pieces):
        if k + ahead < len(pieces):
            copy(k + ahead).start()
        copy(k).wait()
        dst[c * WEIGHT_ROWS:(c + 1) * WEIGHT_ROWS, :] = stage_ref[k % WEIGHT_SLOTS, :, 0:dst.shape[1]].astype(_BF16)


def _conv_stages(ua_ref, ub_ref, wb_ref, convb_ref, lng_ref, lnb_ref, gb_ref, cv_ref, cat_ref, base, n):
    reps = lambda w, rows: jnp.concatenate([w] * (rows // (2 * SUBLANES)), axis=0)
    for ls in range(B_WIDTH // LANES):
        lanes = slice(ls * LANES, (ls + 1) * LANES)
        acc = jnp.broadcast_to(convb_ref[:, lanes], (n, LANES))
        for off in range(SUBLANES):
            taps = [j for j in range(CONV_WIDTH) if (FIRST + j) % SUBLANES == off]
            rows = n + (2 * SUBLANES if off else 0)
            part = None
            for j in taps:
                rel = FIRST + j - off
                if rel % (2 * SUBLANES) == 0:
                    win = ua_ref[pl.ds(_aligned(base + rel, 2 * SUBLANES), rows), lanes]
                else:
                    win = ub_ref[pl.ds(_aligned(base + rel + SUBLANES, 2 * SUBLANES), rows), lanes]
                term = reps(wb_ref[j, :, lanes], rows) * win
                part = term if part is None else part + term
            part = part.astype(_F32)
            acc = acc + (part[off:off + n, :] if off else part)
            if off == SUBLANES // 2 - 1:
                yield
        cv_ref[:, lanes] = acc
        yield
    rows = pl.ds(_aligned(base, n), n)
    o_b = _group_b_out(cv_ref[...], lng_ref, lnb_ref, gb_ref[rows, :])
    cat_ref[rows, A_WIDTH:] = o_b.astype(_BF16)


def _prompt_kernel(x_ref, normg_ref, win_hbm, lb_ref, ag_ref, bglu_ref, convw_ref, convb_ref,
                   lng_ref, lnb_ref, wout_hbm, fg_ref, masks_ref, wb_ref,
                   y_ref, st_out_ref, cs_out_ref, win_out_hbm, wout_out_hbm,
                   pa_ref, uh_ref, ua_ref, ub_ref, gb_ref, cv_ref, oa_ref, cat_ref, st_ref,
                   win_ref, wout_ref, stage_ref, sems, *, tb):
    t = pl.program_id(1)
    nt = pl.num_programs(1)

    first = (pl.program_id(0) == 0) & (t == 0)
    handoff = [pltpu.make_async_copy(win_ref, win_out_hbm, sems.at[WEIGHT_SLOTS]),
               pltpu.make_async_copy(wout_ref, wout_out_hbm, sems.at[WEIGHT_SLOTS + 1])]

    @pl.when(first)
    def _():
        _cast_weights([(win_hbm, win_ref), (wout_hbm, wout_ref)], stage_ref, sems)
        for copy in handoff:
            copy.start()

    @pl.when(t == 0)
    def _():
        st_ref[...] = jnp.zeros_like(st_ref)
        uh_ref[...] = jnp.zeros_like(uh_ref)
        ua_ref[...] = jnp.zeros_like(ua_ref)
        ub_ref[...] = jnp.zeros_like(ub_ref)

    def front_tile(r0):
        rows = pl.ds(r0, ROW_TILE)
        h = _rmsnorm(x_ref[rows, :], normg_ref[...]).astype(_BF16)
        for half in range(B_WIDTH // MXU_N):
            u, gb = _project_b(lambda: h, win_ref, bglu_ref, half)
            cols = slice(half * MXU_N, (half + 1) * MXU_N)
            here = pl.ds(_aligned(HIST + r0, HIST), ROW_TILE)
            prev = uh_ref[pl.ds(_aligned(HIST + r0 - SUBLANES, SUBLANES), SUBLANES), cols]
            uh_ref[here, cols] = u
            ua_ref[here, cols] = u.astype(_BF16)
            ub_ref[here, cols] = jnp.concatenate([prev, u[:ROW_TILE - SUBLANES, :]], axis=0).astype(_BF16)
            gb_ref[rows, cols] = gb

        def store_a(slab, val):
            pa_ref[slab, rows, :] = val
        for cg in range(2 * A_HEADS):
            _project_a(lambda: h, win_ref, lb_ref, store_a, cg)

    def front_body(p, carry):
        for n in range(UNROLL):
            front_tile(_aligned((UNROLL * p + n) * ROW_TILE, ROW_TILE))
        return carry

    _loop(tb // (UNROLL * ROW_TILE), front_body)
    ub_ref[HIST + tb:HIST + tb + 2 * SUBLANES, :] = jnp.concatenate(
        [uh_ref[HIST + tb - SUBLANES:HIST + tb, :], jnp.zeros((SUBLANES, B_WIDTH), _F32)], axis=0).astype(_BF16)

    def chunk(base, slot):
        def head(hd):
            def slabs(arr):
                return [pa_ref[arr * A_HEADS + hd, pl.ds(base + i, SUBLANES, stride=SUBLANES), :]
                        for i in range(SUBLANES)]

            def done(o, qc, kc, v_all, decay):
                st_ref[hd] = st_ref[hd] * decay + _dot_tn(v_all, kc)
                for i in range(SUBLANES):
                    oa_ref[hd, pl.ds(base + i, SUBLANES, stride=SUBLANES), :] = o[i * SUBLANES:(i + 1) * SUBLANES, :]
            return _hgrn_stages(slabs(0), slabs(1), slabs(2), masks_ref, lambda: st_ref[hd], True, done)
        conv = _conv_stages(ua_ref, ub_ref, wb_ref, convb_ref, lng_ref, lnb_ref, gb_ref, cv_ref.at[slot], cat_ref, base, CHUNK)
        return [head(hd) for hd in range(A_HEADS)] + [conv]

    def chunk_body(p, carry):
        _round_robin([chunk(_aligned((CHUNK_UNROLL * p + n) * CHUNK, CHUNK), n % CONV_BUFS) for n in range(CHUNK_UNROLL)])
        return carry

    _loop(tb // (CHUNK_UNROLL * CHUNK), chunk_body)

    @pl.when(t == nt - 1)
    def _():
        cs_out_ref[...] = uh_ref[HIST + tb - (CONV_WIDTH - 1):HIST + tb, :]
        for hd in range(A_HEADS):
            st_out_ref[hd] = st_ref[hd].T

    uh_ref[0:HIST, :] = uh_ref[tb:tb + HIST, :]
    ua_ref[0:HIST, :] = ua_ref[tb:tb + HIST, :]
    ub_ref[0:HIST, :] = ub_ref[tb:tb + HIST, :]

    def back_body(p, carry):
        tiles = [pl.ds(_aligned((UNROLL * p + n) * ROW_TILE, ROW_TILE), ROW_TILE) for n in range(UNROLL)]
        for rows in tiles:
            for hd in range(A_HEADS):
                on = _head_out(oa_ref[hd, rows, :], ag_ref[...], pa_ref[3 * A_HEADS + hd, rows, :])
                cat_ref[rows, hd * HEAD_DIM:(hd + 1) * HEAD_DIM] = on.astype(_BF16)
        outs = [_dot(cat_ref[rows, :], wout_ref[...]) for rows in tiles]
        for rows, out in zip(tiles, outs):
            y_ref[rows, :] = _rmsnorm(x_ref[rows, :] + out, fg_ref[...])
        return carry

    _loop(tb // (UNROLL * ROW_TILE), back_body)

    @pl.when(first)
    def _():
        for copy in handoff:
            copy.wait()


def _full(shape):
    nd = len(shape)
    return pl.BlockSpec(shape, lambda *_: (0,) * nd, pipeline_mode=pl.Buffered(1))


def _prompt_call(x, weights, tb):
    n, t, _ = x.shape
    assert t % tb == 0 and tb % (UNROLL * ROW_TILE) == 0 and tb % (CHUNK_UNROLL * CHUNK) == 0
    masks = jnp.asarray(_level_masks((1, 2, 4, 8, 16, 32), SUBLANES, SUBLANES))
    conv_w = weights[5]
    wb = jnp.broadcast_to(conv_w.astype(_BF16)[:, None, :], (CONV_WIDTH, 2 * SUBLANES, B_WIDTH))
    kern = functools.partial(_prompt_kernel, tb=tb)
    w_in, w_out = weights[1], weights[9]
    assert w_in.dtype == _F32 and w_out.dtype == _F32 and w_in.shape[0] % WEIGHT_ROWS == 0 and w_out.shape[0] % WEIGHT_ROWS == 0
    in_hbm = pl.BlockSpec(memory_space=pl.ANY)
    w_specs = [in_hbm if w is w_in or w is w_out else _full(w.shape) for w in weights]
    return pl.pallas_call(
        kern,
        grid=(n, t // tb),
        in_specs=([pl.BlockSpec((None, tb, D_MODEL), lambda i, j: (i, j, 0))] + w_specs
                  + [_full(masks.shape), _full(wb.shape)]),
        out_specs=[
            pl.BlockSpec((None, tb, D_MODEL), lambda i, j: (i, j, 0)),
            pl.BlockSpec((None, A_HEADS, HEAD_DIM, HEAD_DIM), lambda i, j: (i, 0, 0, 0)),
            pl.BlockSpec((None, CONV_WIDTH - 1, B_WIDTH), lambda i, j: (i, 0, 0)),
            in_hbm,
            in_hbm,
        ],
        out_shape=[
            jax.ShapeDtypeStruct((n, t, D_MODEL), _F32),
            jax.ShapeDtypeStruct((n, A_HEADS, HEAD_DIM, HEAD_DIM), _F32),
            jax.ShapeDtypeStruct((n, CONV_WIDTH - 1, B_WIDTH), _F32),
            jax.ShapeDtypeStruct(w_in.shape, _BF16),
            jax.ShapeDtypeStruct(w_out.shape, _BF16),
        ],
        scratch_shapes=[
            pltpu.VMEM((4 * A_HEADS, tb, LANES), _F32),
            pltpu.VMEM((HIST + tb, B_WIDTH), _F32),
            pltpu.VMEM((HIST + tb + 2 * SUBLANES, B_WIDTH), _BF16),
            pltpu.VMEM((HIST + tb + 2 * SUBLANES, B_WIDTH), _BF16),
            pltpu.VMEM((tb, B_WIDTH), _F32),
            pltpu.VMEM((CONV_BUFS, CHUNK, B_WIDTH), _F32),
            pltpu.VMEM((A_HEADS, tb, LANES), _F32),
            pltpu.VMEM((tb, A_WIDTH + B_WIDTH), _BF16),
            pltpu.VMEM((A_HEADS, HEAD_DIM, HEAD_DIM), _F32),
            pltpu.VMEM(w_in.shape, _BF16),
            pltpu.VMEM(w_out.shape, _BF16),
            pltpu.VMEM((WEIGHT_SLOTS, WEIGHT_ROWS, max(w_in.shape[1], w_out.shape[1])), _F32),
            pltpu.SemaphoreType.DMA((WEIGHT_SLOTS + 2,)),
        ],
        compiler_params=pltpu.CompilerParams(
            dimension_semantics=("arbitrary", "arbitrary"),
            vmem_limit_bytes=VMEM_LIMIT_BYTES,
        ),
        name="hymba_prompt",
    )(x, *weights, masks, wb)


def _decode_kernel(x_ref, s0_ref, c0_ref, normg_ref, win_ref, lb_ref, ag_ref, bglu_ref, convw_ref, convb_ref,
                   lng_ref, lnb_ref, wout_ref, fg_ref, masks_ref,
                   y_ref, s_out_ref, cs_out_ref,
                   pa_ref, ub_ref, gb_ref, oa_ref, ob_ref, *, t_dec):
    g = pl.program_id(0)
    n_tok = x_ref.shape[0]
    grp = SUBLANES * t_dec
    n_ls = B_WIDTH // LANES

    @pl.when(g == 0)
    def _():
        def proj_body(r, carry):
            rows = pl.ds(pl.multiple_of(r * ROW_TILE, ROW_TILE), ROW_TILE)
            h = _rmsnorm(x_ref[rows, :], normg_ref[...]).astype(_BF16)
            for half in range(B_WIDTH // MXU_N):
                u, gb = _project_b(lambda: h, win_ref, bglu_ref, half)
                for c in range(MXU_N // LANES):
                    ls = half * (MXU_N // LANES) + c
                    ub_ref[ls, rows, :] = u[:, c * LANES:(c + 1) * LANES]
                    gb_ref[ls, rows, :] = gb[:, c * LANES:(c + 1) * LANES]

            def store_a(slab, val):
                pa_ref[slab, rows, :] = val
            for cg in range(2 * A_HEADS):
                _project_a(lambda: h, win_ref, lb_ref, store_a, cg)
            return carry
        lax.fori_loop(0, n_tok // ROW_TILE, proj_body, 0)

    def group(sub):
        seqs = slice(sub * SUBLANES, (sub + 1) * SUBLANES)
        r0 = pl.multiple_of((g * DEC_GROUPS + sub) * grp, grp)

        def token_slab(ref, idx, t):
            return ref[idx, pl.ds(r0 + t, SUBLANES, stride=t_dec), :]

        u_tok = [jnp.concatenate([token_slab(ub_ref, ls, t) for ls in range(n_ls)], axis=-1) for t in range(t_dec)]
        full = lambda kk: c0_ref[kk, seqs, :] if kk < CONV_WIDTH - 1 else u_tok[kk - (CONV_WIDTH - 1)]
        for kk in range(CONV_WIDTH - 1):
            cs_out_ref[kk, seqs, :] = full(kk + t_dec)
        for t in range(t_dec):
            cv = jnp.broadcast_to(convb_ref[...], (SUBLANES, B_WIDTH))
            for j in range(CONV_WIDTH):
                cv = cv + convw_ref[j:j + 1, :] * full(t + j)
            gate_b = jnp.concatenate([token_slab(gb_ref, ls, t) for ls in range(n_ls)], axis=-1)
            o_b = _group_b_out(cv, lng_ref, lnb_ref, gate_b)
            for ls in range(n_ls):
                ob_ref[ls, pl.ds(r0 + t, SUBLANES, stride=t_dec), :] = o_b[:, ls * LANES:(ls + 1) * LANES]

        row32 = lax.broadcasted_iota(jnp.int32, (grp, LANES), 0) % SUBLANES
        row64 = lax.broadcasted_iota(jnp.int32, (2 * grp, LANES), 0)
        for hd in range(A_HEADS):
            slabs = lambda arr: [token_slab(pa_ref, arr * A_HEADS + hd, t) for t in range(t_dec)]
            res = []
            for _ in _hgrn_stages(slabs(0), slabs(1), slabs(2), masks_ref, None, False, lambda *a: res.extend(a)):
                pass
            o, qc, kc, v_all, decay = res
            qf, kf, vf = qc.astype(_F32), kc.astype(_F32), v_all.astype(_F32)
            e_hi = decay.astype(_BF16).astype(_F32)
            e_mid = (decay - e_hi).astype(_BF16).astype(_F32)
            e_lo = (decay - e_hi - e_mid).astype(_BF16).astype(_F32)
            lhs = jnp.concatenate([kf, e_hi, e_mid, e_lo, jnp.zeros((SUBLANES, LANES), _F32)], axis=0).astype(_BF16)
            v_pad = jnp.concatenate([vf, jnp.zeros((grp, LANES), _F32)], axis=0)
            o_state = None
            for s in range(SUBLANES):
                s0 = s0_ref[sub * SUBLANES + s, hd]
                term = _dot(jnp.where(row32 == s, qf, 0.0).astype(_BF16), s0.astype(_BF16))
                o_state = term if o_state is None else o_state + term
                mine = row64 % SUBLANES == s
                rhs = jnp.concatenate([
                    jnp.where(mine & (row64 < grp), v_pad, 0.0),
                    jnp.where(mine & (row64 >= grp) & (row64 < grp + 3 * SUBLANES), 1.0, 0.0)], axis=-1).astype(_BF16)
                upd = _dot_tn(lhs, rhs)
                s_out_ref[sub * SUBLANES + s, hd] = s0 * upd[:, HEAD_DIM:] + upd[:, :HEAD_DIM]
            gate = jnp.concatenate([token_slab(pa_ref, 3 * A_HEADS + hd, t) for t in range(t_dec)], axis=0)
            on = _head_out(o + o_state, ag_ref[...], gate)
            for t in range(t_dec):
                oa_ref[hd, pl.ds(r0 + t, SUBLANES, stride=t_dec), :] = on[t * SUBLANES:(t + 1) * SUBLANES, :]

    for sub in range(DEC_GROUPS):
        group(sub)

    @pl.when(g == pl.num_programs(0) - 1)
    def _():
        def out_body(r, carry):
            rows = pl.ds(pl.multiple_of(r * ROW_TILE, ROW_TILE), ROW_TILE)
            cat = jnp.concatenate([oa_ref[hd, rows, :] for hd in range(A_HEADS)]
                                  + [ob_ref[ls, rows, :] for ls in range(n_ls)], axis=-1).astype(_BF16)
            out = _dot(cat, wout_ref[...])
            y_ref[rows, :] = _rmsnorm(x_ref[rows, :] + out, fg_ref[...])
            return carry
        lax.fori_loop(0, n_tok // ROW_TILE, out_body, 0)


def _decode_call(x, s0, c0, weights):
    n, t_dec, _ = x.shape
    assert n % (DEC_GROUPS * SUBLANES) == 0 and t_dec == 4 and (n * t_dec) % ROW_TILE == 0
    n_tok = n * t_dec
    x = x.reshape(n_tok, D_MODEL)
    masks = jnp.asarray(_level_masks((1, 2), t_dec, SUBLANES, same_group=t_dec))
    kern = functools.partial(_decode_kernel, t_dec=t_dec)
    w_specs = [_full(w.shape) for w in weights]
    per_step = DEC_GROUPS * SUBLANES
    state_spec = pl.BlockSpec((per_step, A_HEADS, HEAD_DIM, HEAD_DIM), lambda i: (i, 0, 0, 0))
    conv_spec = pl.BlockSpec((CONV_WIDTH - 1, per_step, B_WIDTH), lambda i: (0, i, 0))
    y, s_new, c_new = pl.pallas_call(
        kern,
        grid=(n // per_step,),
        in_specs=[_full((n_tok, D_MODEL)), state_spec, conv_spec] + w_specs + [_full(masks.shape)],
        out_specs=[_full((n_tok, D_MODEL)), state_spec, conv_spec],
        out_shape=[
            jax.ShapeDtypeStruct((n_tok, D_MODEL), _F32),
            jax.ShapeDtypeStruct((n, A_HEADS, HEAD_DIM, HEAD_DIM), _F32),
            jax.ShapeDtypeStruct((CONV_WIDTH - 1, n, B_WIDTH), _F32),
        ],
        scratch_shapes=[
            pltpu.VMEM((4 * A_HEADS, n_tok, LANES), _F32),
            pltpu.VMEM((B_WIDTH // LANES, n_tok, LANES), _F32),
            pltpu.VMEM((B_WIDTH // LANES, n_tok, LANES), _F32),
            pltpu.VMEM((A_HEADS, n_tok, LANES), _F32),
            pltpu.VMEM((B_WIDTH // LANES, n_tok, LANES), _F32),
        ],
        compiler_params=pltpu.CompilerParams(
            dimension_semantics=("arbitrary",),
            vmem_limit_bytes=VMEM_LIMIT_BYTES,
        ),
        name="hymba_decode",
    )(x, s0, c0, *weights, masks)
    return y.reshape(n, t_dec, D_MODEL), s_new, c_new


def _prepare_weights(norm_in_g, w_in, lb_logits, hgrn_norm_g, b_glu, conv_w, conv_b, ln_g, ln_b, w_out, final_norm_g, layer):
    lb = jnp.cumsum(jax.nn.softmax(lb_logits.astype(_F32), axis=0), axis=0)[layer]
    row = lambda a: a.astype(_F32).reshape(1, -1)
    return (
        row(norm_in_g[layer]),
        w_in[layer].astype(_F32),
        row(lb),
        row(hgrn_norm_g[layer]),
        row(b_glu[layer]),
        conv_w[layer].astype(_F32),
        row(conv_b[layer]),
        row(ln_g[layer]),
        row(ln_b[layer]),
        w_out[layer].astype(_F32),
        row(final_norm_g),
    )


def kernel(x_prompt, x_sample, state_hgrn, state_conv, norm_in_g, w_in, lb_logits, hgrn_norm_g, b_glu, conv_w,
           conv_b, ln_g, ln_b, w_out, final_norm_g):
    depth = w_in.shape[0]
    assert depth == 1, "single mixer layer: the final norm is fused into the layer kernel"
    weights = _prepare_weights(norm_in_g, w_in, lb_logits, hgrn_norm_g, b_glu, conv_w, conv_b, ln_g, ln_b, w_out,
                               final_norm_g, 0)
    y_p, s_p, c_p, w_in_b, w_out_b = _prompt_call(x_prompt, weights, tb=1024)
    weights = weights[:1] + (w_in_b,) + weights[2:9] + (w_out_b,) + weights[10:]
    y_s, s_s, c_s = _decode_call(x_sample, state_hgrn[0], jnp.transpose(state_conv[0], (1, 0, 2)), weights)
    return (y_p, y_s, s_p[None], c_p[None], s_s[None], jnp.transpose(c_s, (1, 0, 2))[None])
```

```python
import functools
import math

import numpy as np
import jax
import jax.numpy as jnp
from jax import lax
from jax.experimental import pallas as pl
from jax.experimental.pallas import tpu as pltpu

D_MODEL = 1024
A_HEADS = 4
HEAD_DIM = 128
A_WIDTH = A_HEADS * HEAD_DIM
B_WIDTH = 512
CONV_WIDTH = 31
EPS = 1e-6
IN_WIDTH = 4 * A_WIDTH + 3 * B_WIDTH
GLU_OFF = 4 * A_WIDTH
ZB_OFF = GLU_OFF + 2 * B_WIDTH

LANES = 128
SUBLANES = 8
MXU_N = 256
CHUNK = SUBLANES * SUBLANES
ROW_TILE = 128
UNROLL = 8
CHUNK_UNROLL = 2
CHUNK_LAG = 4
CONV_BUFS = 3
DEC_GROUPS = 2
HIST = 32
FIRST = HIST - (CONV_WIDTH - 1)
VMEM_LIMIT_BYTES = 56 * 1024 * 1024
LOG2E = math.log2(math.e)

_F32 = jnp.float32
_BF16 = jnp.bfloat16


def _sigmoid(x):
    return 1.0 / (1.0 + jnp.exp(-x))


def _silu(x):
    return x * _sigmoid(x)


def _dot(a, b):
    return jnp.dot(a, b, preferred_element_type=_F32)


def _dot_nt(a, b):
    return lax.dot_general(a, b, (((1,), (1,)), ((), ())), preferred_element_type=_F32)


def _dot_tn(a, b):
    return lax.dot_general(a, b, (((0,), (0,)), ((), ())), preferred_element_type=_F32)


def _level_masks(levels, n_res, n_rows, same_group=None):
    r = np.arange(n_res * n_rows)
    t = n_res * (r % n_rows) + r // n_rows
    tt, ss = t[:, None], t[None, :]
    masks = [tt == ss]
    for b in levels:
        masks.append(((tt // b) == (ss // b) + 1) & ((ss // b) % 2 == 0))
    masks = np.stack(masks)
    want = (ss <= tt) if same_group is None else ((ss <= tt) & (tt // same_group == ss // same_group))
    assert (masks.sum(0) == want).all()
    return masks.astype(np.float32)


def _rows(x, r):
    return jnp.broadcast_to(x[r:r + 1, :], x.shape)


def _hgrn_stages(q, f, v, masks_ref, get_st, coarse, done):
    n_res = len(q)
    every = range(n_res)
    k = [1.0 - fi for fi in f]
    lf = [jnp.log(fi) * LOG2E for fi in f]
    g_in = [lf[0]]
    for i in range(1, n_res):
        g_in.append(g_in[-1] + lf[i])
    tot = g_in[-1]
    zero = jnp.zeros_like(tot)

    def cat(parts):
        return jnp.concatenate(parts, axis=0).astype(_BF16)

    def slab_rows(x, i):
        return x[i * SUBLANES:(i + 1) * SUBLANES, :]

    v_all = cat(v)
    s_rows = [None] * n_res
    pending = []

    def fold():
        level, rows, s_l = pending.pop(0)
        for n, i in enumerate(rows):
            term = masks_ref[level, i * SUBLANES:(i + 1) * SUBLANES, :] * slab_rows(s_l, n)
            s_rows[i] = term if s_rows[i] is None else s_rows[i] + term

    def step(rows, qt, kt):
        pending.append((step.level, rows, _dot_nt(cat(qt), cat(kt))))
        step.level += 1
    step.level = 0

    step(every, q, k)
    yield
    b = 1
    while b < n_res:
        odd = [i for i in every if (i // b) % 2 == 1]
        qt, kt = [], []
        for i in every:
            bs = (i // b) * b
            if i in odd:
                qt.append(q[i] * jnp.exp2(g_in[i] - g_in[bs - 1]))
                kt.append(zero)
            else:
                be = bs + b - 1
                kt.append(k[i] * jnp.exp2(g_in[be] - g_in[i]) if i != be else k[i])
        step(odd, qt, kt)
        yield
        fold()
        b *= 2
    qe = [q[i] * jnp.exp2(g_in[i]) for i in every]
    ke = [k[i] * jnp.exp2(tot - g_in[i]) for i in range(n_res - 1)] + [k[-1]]
    if coarse:
        row = lax.broadcasted_iota(jnp.int32, tot.shape, 0)
        pref = tot
        for sh in (1, 2, 4):
            pref = pref + jnp.where(row >= sh, pltpu.roll(pref, sh, 0), 0.0)
        before = pref - tot
        end_all = _rows(pref, SUBLANES - 1)
        qe_b, ke_b = cat(qe), cat(ke)
        tile = lambda x: jnp.concatenate([x] * n_res, axis=0).astype(_BF16)
        for gsz in (1, 2, 4):
            if gsz == 1:
                qt, kt = qe_b, ke_b
            else:
                if gsz == 2:
                    g_start = jnp.where(row % 2 == 0, before, pltpu.roll(before, 1, 0))
                    g_end = jnp.where(row % 2 == 1, pref, pltpu.roll(pref, SUBLANES - 1, 0))
                else:
                    g_start = jnp.where(row < 4, _rows(before, 0), _rows(before, 4))
                    g_end = jnp.where(row < 4, _rows(pref, 3), _rows(pref, 7))
                qt = qe_b * tile(jnp.exp2(before - g_start))
                kt = ke_b * tile(jnp.exp2(g_end - pref))
            pending.append((step.level, every, _dot_nt(qt, kt)))
            step.level += 1
            yield
            fold()
        qc = qe_b * tile(jnp.exp2(before))
        kc = ke_b * tile(jnp.exp2(end_all - pref))
        o_st = _dot_nt(qc, get_st().astype(_BF16))
        decay = jnp.exp2(pref[SUBLANES - 1:SUBLANES, :])
        yield
    else:
        qc, kc = cat(qe), cat(ke)
        o_st = None
        decay = jnp.exp2(tot)
    while pending:
        fold()
    o = _dot(cat(s_rows), v_all)
    yield
    done(o if o_st is None else o + o_st, qc, kc, v_all, decay)


def _round_robin(starts):
    waiting = list(starts)
    live = []
    rnd = 0
    while waiting or live:
        if waiting and rnd % CHUNK_LAG == 0:
            live += waiting.pop(0)
        for g in list(live):
            try:
                next(g)
            except StopIteration:
                live.remove(g)
        rnd += 1


def _aligned(x, m):
    return x if isinstance(x, int) else pl.multiple_of(x, m)


def _loop(trips, body):
    if trips == 1:
        body(0, 0)
    else:
        lax.fori_loop(0, trips, body, 0)


def _head_out(o, ag, gate):
    ms = jnp.mean(o * o, axis=-1, keepdims=True)
    return o * lax.rsqrt(ms + EPS) * ag * gate


def _project_a(h, win_ref, lb_ref, store, cg):
    p = _dot(h(), win_ref[:, cg * MXU_N:(cg + 1) * MXU_N])
    arr = cg // 2
    if arr == 0 or arr == 3:
        p = _silu(p)
    elif arr == 1:
        lb = lb_ref[:, (cg % 2) * MXU_N:(cg % 2 + 1) * MXU_N]
        p = lb + (1.0 - lb) * _sigmoid(p)
    store(2 * cg, p[:, :LANES])
    store(2 * cg + 1, p[:, LANES:])


def _project_b(h, win_ref, bglu_ref, half):
    c0 = half * MXU_N
    glu_a = _dot(h(), win_ref[:, GLU_OFF + c0:GLU_OFF + c0 + MXU_N]) + bglu_ref[:, c0:c0 + MXU_N]
    glu_b = (_dot(h(), win_ref[:, GLU_OFF + B_WIDTH + c0:GLU_OFF + B_WIDTH + c0 + MXU_N])
             + bglu_ref[:, B_WIDTH + c0:B_WIDTH + c0 + MXU_N])
    zb = _dot(h(), win_ref[:, ZB_OFF + c0:ZB_OFF + c0 + MXU_N])
    return glu_a * _sigmoid(glu_b), _silu(zb)


def _rmsnorm(x, g):
    return x * lax.rsqrt(jnp.mean(x * x, axis=-1, keepdims=True) + EPS) * g


def _group_b_out(cv, lng_ref, lnb_ref, gate_b):
    mu = jnp.mean(cv, axis=-1, keepdims=True)
    d = cv - mu
    var = jnp.mean(d * d, axis=-1, keepdims=True)
    ln = d * lax.rsqrt(var + EPS) * lng_ref[...] + lnb_ref[...]
    return _silu(ln) * gate_b


def _cast_weight(src_hbm, dst_ref, slab_refs, sems):
    n = dst_ref.shape[1] // LANES
    slots = len(slab_refs)

    def copy(j):
        return pltpu.make_async_copy(src_hbm.at[:, pl.ds(j * LANES, LANES)], slab_refs[j % slots], sems.at[j % slots])

    for j in range(min(slots, n)):
        copy(j).start()
    for j in range(n):
        copy(j).wait()
        dst_ref[:, j * LANES:(j + 1) * LANES] = slab_refs[j % slots][...].astype(_BF16)
        if j + slots < n:
            copy(j + slots).start()


def _conv_stages(ua_ref, ub_ref, wb_ref, convb_ref, lng_ref, lnb_ref, gb_ref, cv_ref, cat_ref, base, n):
    reps = lambda w, rows: jnp.concatenate([w] * (rows // (2 * SUBLANES)), axis=0)
    for ls in range(B_WIDTH // LANES):
        lanes = slice(ls * LANES, (ls + 1) * LANES)
        acc = jnp.broadcast_to(convb_ref[:, lanes], (n, LANES))
        for off in range(SUBLANES):
            taps = [j for j in range(CONV_WIDTH) if (FIRST + j) % SUBLANES == off]
            rows = n + (2 * SUBLANES if off else 0)
            part = None
            for j in taps:
                rel = FIRST + j - off
                if rel % (2 * SUBLANES) == 0:
                    win = ua_ref[pl.ds(_aligned(base + rel, 2 * SUBLANES), rows), lanes]
                else:
                    win = ub_ref[pl.ds(_aligned(base + rel + SUBLANES, 2 * SUBLANES), rows), lanes]
                term = reps(wb_ref[j, :, lanes], rows) * win
                part = term if part is None else part + term
            part = part.astype(_F32)
            acc = acc + (part[off:off + n, :] if off else part)
            if off == SUBLANES // 2 - 1:
                yield
        cv_ref[:, lanes] = acc
        yield
    rows = pl.ds(_aligned(base, n), n)
    o_b = _group_b_out(cv_ref[...], lng_ref, lnb_ref, gb_ref[rows, :])
    cat_ref[rows, A_WIDTH:] = o_b.astype(_BF16)


def _prompt_kernel(x_ref, normg_ref, win_hbm, lb_ref, ag_ref, bglu_ref, convw_ref, convb_ref,
                   lng_ref, lnb_ref, wout_hbm, fg_ref, masks_ref, wb_ref,
                   y_ref, st_out_ref, cs_out_ref, win_out_hbm, wout_out_hbm,
                   pa_ref, uh_ref, ua_ref, ub_ref, gb_ref, cv_ref, oa_ref, cat_ref, st_ref,
                   win_ref, wout_ref, wout_f32_ref, sems, *, tb):
    t = pl.program_id(1)
    nt = pl.num_programs(1)

    first = (pl.program_id(0) == 0) & (t == 0)
    n_slabs = pa_ref.shape[0]
    handoff = [pltpu.make_async_copy(win_ref, win_out_hbm, sems.at[n_slabs]),
               pltpu.make_async_copy(wout_ref, wout_out_hbm, sems.at[n_slabs + 1])]
    wout_fetch = pltpu.make_async_copy(wout_hbm, wout_f32_ref, sems.at[n_slabs + 2])

    @pl.when(first)
    def _():
        _cast_weight(win_hbm, win_ref, [pa_ref.at[s, pl.ds(0, win_ref.shape[0])] for s in range(n_slabs)], sems)
        wout_fetch.start()
        handoff[0].start()

    @pl.when(t == 0)
    def _():
        st_ref[...] = jnp.zeros_like(st_ref)
        uh_ref[...] = jnp.zeros_like(uh_ref)
        ua_ref[...] = jnp.zeros_like(ua_ref)
        ub_ref[...] = jnp.zeros_like(ub_ref)

    def front_tile(r0):
        rows = pl.ds(r0, ROW_TILE)
        h = _rmsnorm(x_ref[rows, :], normg_ref[...]).astype(_BF16)
        for half in range(B_WIDTH // MXU_N):
            u, gb = _project_b(lambda: h, win_ref, bglu_ref, half)
            cols = slice(half * MXU_N, (half + 1) * MXU_N)
            here = pl.ds(_aligned(HIST + r0, HIST), ROW_TILE)
            prev = uh_ref[pl.ds(_aligned(HIST + r0 - SUBLANES, SUBLANES), SUBLANES), cols]
            uh_ref[here, cols] = u
            ua_ref[here, cols] = u.astype(_BF16)
            ub_ref[here, cols] = jnp.concatenate([prev, u[:ROW_TILE - SUBLANES, :]], axis=0).astype(_BF16)
            gb_ref[rows, cols] = gb

        def store_a(slab, val):
            pa_ref[slab, rows, :] = val
        for cg in range(2 * A_HEADS):
            _project_a(lambda: h, win_ref, lb_ref, store_a, cg)

    def front_body(p, carry):
        for n in range(UNROLL):
            front_tile(_aligned((UNROLL * p + n) * ROW_TILE, ROW_TILE))
        return carry

    _loop(tb // (UNROLL * ROW_TILE), front_body)
    ub_ref[HIST + tb:HIST + tb + 2 * SUBLANES, :] = jnp.concatenate(
        [uh_ref[HIST + tb - SUBLANES:HIST + tb, :], jnp.zeros((SUBLANES, B_WIDTH), _F32)], axis=0).astype(_BF16)

    def chunk(base, slot):
        def head(hd):
            def slabs(arr):
                return [pa_ref[arr * A_HEADS + hd, pl.ds(base + i, SUBLANES, stride=SUBLANES), :]
                        for i in range(SUBLANES)]

            def done(o, qc, kc, v_all, decay):
                st_ref[hd] = st_ref[hd] * decay + _dot_tn(v_all, kc)
                for i in range(SUBLANES):
                    oa_ref[hd, pl.ds(base + i, SUBLANES, stride=SUBLANES), :] = o[i * SUBLANES:(i + 1) * SUBLANES, :]
            return _hgrn_stages(slabs(0), slabs(1), slabs(2), masks_ref, lambda: st_ref[hd], True, done)
        conv = _conv_stages(ua_ref, ub_ref, wb_ref, convb_ref, lng_ref, lnb_ref, gb_ref, cv_ref.at[slot], cat_ref, base, CHUNK)
        return [head(hd) for hd in range(A_HEADS)] + [conv]

    def chunk_body(p, carry):
        _round_robin([chunk(_aligned((CHUNK_UNROLL * p + n) * CHUNK, CHUNK), n % CONV_BUFS) for n in range(CHUNK_UNROLL)])
        return carry

    _loop(tb // (CHUNK_UNROLL * CHUNK), chunk_body)

    @pl.when(t == nt - 1)
    def _():
        cs_out_ref[...] = uh_ref[HIST + tb - (CONV_WIDTH - 1):HIST + tb, :]
        for hd in range(A_HEADS):
            st_out_ref[hd] = st_ref[hd].T

    uh_ref[0:HIST, :] = uh_ref[tb:tb + HIST, :]
    ua_ref[0:HIST, :] = ua_ref[tb:tb + HIST, :]
    ub_ref[0:HIST, :] = ub_ref[tb:tb + HIST, :]

    @pl.when(first)
    def _():
        wout_fetch.wait()
        for r0 in range(0, wout_ref.shape[0], ROW_TILE):
            wout_ref[r0:r0 + ROW_TILE, :] = wout_f32_ref[r0:r0 + ROW_TILE, :].astype(_BF16)
        handoff[1].start()

    def back_body(p, carry):
        tiles = [pl.ds(_aligned((UNROLL * p + n) * ROW_TILE, ROW_TILE), ROW_TILE) for n in range(UNROLL)]
        for rows in tiles:
            for hd in range(A_HEADS):
                on = _head_out(oa_ref[hd, rows, :], ag_ref[...], pa_ref[3 * A_HEADS + hd, rows, :])
                cat_ref[rows, hd * HEAD_DIM:(hd + 1) * HEAD_DIM] = on.astype(_BF16)
        outs = [_dot(cat_ref[rows, :], wout_ref[...]) for rows in tiles]
        for rows, out in zip(tiles, outs):
            y_ref[rows, :] = _rmsnorm(x_ref[rows, :] + out, fg_ref[...])
        return carry

    _loop(tb // (UNROLL * ROW_TILE), back_body)

    @pl.when(first)
    def _():
        for copy in handoff:
            copy.wait()


def _full(shape):
    nd = len(shape)
    return pl.BlockSpec(shape, lambda *_: (0,) * nd, pipeline_mode=pl.Buffered(1))


def _prompt_call(x, weights, tb):
    n, t, _ = x.shape
    assert t % tb == 0 and tb % (UNROLL * ROW_TILE) == 0 and tb % (CHUNK_UNROLL * CHUNK) == 0
    masks = jnp.asarray(_level_masks((1, 2, 4, 8, 16, 32), SUBLANES, SUBLANES))
    conv_w = weights[5]
    wb = jnp.broadcast_to(conv_w.astype(_BF16)[:, None, :], (CONV_WIDTH, 2 * SUBLANES, B_WIDTH))
    kern = functools.partial(_prompt_kernel, tb=tb)
    w_in, w_out = weights[1], weights[9]
    assert w_in.dtype == _F32 and w_out.dtype == _F32 and w_in.shape[0] <= tb and w_in.shape[1] % LANES == 0
    in_hbm = pl.BlockSpec(memory_space=pl.ANY)
    w_specs = [in_hbm if w is w_in or w is w_out else _full(w.shape) for w in weights]
    return pl.pallas_call(
        kern,
        grid=(n, t // tb),
        in_specs=([pl.BlockSpec((None, tb, D_MODEL), lambda i, j: (i, j, 0))] + w_specs
                  + [_full(masks.shape), _full(wb.shape)]),
        out_specs=[
            pl.BlockSpec((None, tb, D_MODEL), lambda i, j: (i, j, 0)),
            pl.BlockSpec((None, A_HEADS, HEAD_DIM, HEAD_DIM), lambda i, j: (i, 0, 0, 0)),
            pl.BlockSpec((None, CONV_WIDTH - 1, B_WIDTH), lambda i, j: (i, 0, 0)),
            in_hbm,
            in_hbm,
        ],
        out_shape=[
            jax.ShapeDtypeStruct((n, t, D_MODEL), _F32),
            jax.ShapeDtypeStruct((n, A_HEADS, HEAD_DIM, HEAD_DIM), _F32),
            jax.ShapeDtypeStruct((n, CONV_WIDTH - 1, B_WIDTH), _F32),
            jax.ShapeDtypeStruct(w_in.shape, _BF16),
            jax.ShapeDtypeStruct(w_out.shape, _BF16),
        ],
        scratch_shapes=[
            pltpu.VMEM((4 * A_HEADS, tb, LANES), _F32),
            pltpu.VMEM((HIST + tb, B_WIDTH), _F32),
            pltpu.VMEM((HIST + tb + 2 * SUBLANES, B_WIDTH), _BF16),
            pltpu.VMEM((HIST + tb + 2 * SUBLANES, B_WIDTH), _BF16),
            pltpu.VMEM((tb, B_WIDTH), _F32),
            pltpu.VMEM((CONV_BUFS, CHUNK, B_WIDTH), _F32),
            pltpu.VMEM((A_HEADS, tb, LANES), _F32),
            pltpu.VMEM((tb, A_WIDTH + B_WIDTH), _BF16),
            pltpu.VMEM((A_HEADS, HEAD_DIM, HEAD_DIM), _F32),
            pltpu.VMEM(w_in.shape, _BF16),
            pltpu.VMEM(w_out.shape, _BF16),
            pltpu.VMEM(w_out.shape, _F32),
            pltpu.SemaphoreType.DMA((4 * A_HEADS + 3,)),
        ],
        compiler_params=pltpu.CompilerParams(
            dimension_semantics=("arbitrary", "arbitrary"),
            vmem_limit_bytes=VMEM_LIMIT_BYTES,
        ),
        name="hymba_prompt",
    )(x, *weights, masks, wb)


def _decode_kernel(x_ref, s0_ref, c0_ref, normg_ref, win_ref, lb_ref, ag_ref, bglu_ref, convw_ref, convb_ref,
                   lng_ref, lnb_ref, wout_ref, fg_ref, masks_ref,
                   y_ref, s_out_ref, cs_out_ref,
                   pa_ref, ub_ref, gb_ref, oa_ref, ob_ref, *, t_dec):
    g = pl.program_id(0)
    n_tok = x_ref.shape[0]
    grp = SUBLANES * t_dec
    n_ls = B_WIDTH // LANES

    @pl.when(g == 0)
    def _():
        def proj_body(r, carry):
            rows = pl.ds(pl.multiple_of(r * ROW_TILE, ROW_TILE), ROW_TILE)
            h = _rmsnorm(x_ref[rows, :], normg_ref[...]).astype(_BF16)
            for half in range(B_WIDTH // MXU_N):
                u, gb = _project_b(lambda: h, win_ref, bglu_ref, half)
                for c in range(MXU_N // LANES):
                    ls = half * (MXU_N // LANES) + c
                    ub_ref[ls, rows, :] = u[:, c * LANES:(c + 1) * LANES]
                    gb_ref[ls, rows, :] = gb[:, c * LANES:(c + 1) * LANES]

            def store_a(slab, val):
                pa_ref[slab, rows, :] = val
            for cg in range(2 * A_HEADS):
                _project_a(lambda: h, win_ref, lb_ref, store_a, cg)
            return carry
        lax.fori_loop(0, n_tok // ROW_TILE, proj_body, 0)

    def group(sub):
        seqs = slice(sub * SUBLANES, (sub + 1) * SUBLANES)
        r0 = pl.multiple_of((g * DEC_GROUPS + sub) * grp, grp)

        def token_slab(ref, idx, t):
            return ref[idx, pl.ds(r0 + t, SUBLANES, stride=t_dec), :]

        u_tok = [jnp.concatenate([token_slab(ub_ref, ls, t) for ls in range(n_ls)], axis=-1) for t in range(t_dec)]
        full = lambda kk: c0_ref[kk, seqs, :] if kk < CONV_WIDTH - 1 else u_tok[kk - (CONV_WIDTH - 1)]
        for kk in range(CONV_WIDTH - 1):
            cs_out_ref[kk, seqs, :] = full(kk + t_dec)
        for t in range(t_dec):
            cv = jnp.broadcast_to(convb_ref[...], (SUBLANES, B_WIDTH))
            for j in range(CONV_WIDTH):
                cv = cv + convw_ref[j:j + 1, :] * full(t + j)
            gate_b = jnp.concatenate([token_slab(gb_ref, ls, t) for ls in range(n_ls)], axis=-1)
            o_b = _group_b_out(cv, lng_ref, lnb_ref, gate_b)
            for ls in range(n_ls):
                ob_ref[ls, pl.ds(r0 + t, SUBLANES, stride=t_dec), :] = o_b[:, ls * LANES:(ls + 1) * LANES]

        row32 = lax.broadcasted_iota(jnp.int32, (grp, LANES), 0) % SUBLANES
        row64 = lax.broadcasted_iota(jnp.int32, (2 * grp, LANES), 0)
        for hd in range(A_HEADS):
            slabs = lambda arr: [token_slab(pa_ref, arr * A_HEADS + hd, t) for t in range(t_dec)]
            res = []
            for _ in _hgrn_stages(slabs(0), slabs(1), slabs(2), masks_ref, None, False, lambda *a: res.extend(a)):
                pass
            o, qc, kc, v_all, decay = res
            qf, kf, vf = qc.astype(_F32), kc.astype(_F32), v_all.astype(_F32)
            e_hi = decay.astype(_BF16).astype(_F32)
            e_mid = (decay - e_hi).astype(_BF16).astype(_F32)
            e_lo = (decay - e_hi - e_mid).astype(_BF16).astype(_F32)
            lhs = jnp.concatenate([kf, e_hi, e_mid, e_lo, jnp.zeros((SUBLANES, LANES), _F32)], axis=0).astype(_BF16)
            v_pad = jnp.concatenate([vf, jnp.zeros((grp, LANES), _F32)], axis=0)
            o_state = None
            for s in range(SUBLANES):
                s0 = s0_ref[sub * SUBLANES + s, hd]
                term = _dot(jnp.where(row32 == s, qf, 0.0).astype(_BF16), s0.astype(_BF16))
                o_state = term if o_state is None else o_state + term
                mine = row64 % SUBLANES == s
                rhs = jnp.concatenate([
                    jnp.where(mine & (row64 < grp), v_pad, 0.0),
                    jnp.where(mine & (row64 >= grp) & (row64 < grp + 3 * SUBLANES), 1.0, 0.0)], axis=-1).astype(_BF16)
                upd = _dot_tn(lhs, rhs)
                s_out_ref[sub * SUBLANES + s, hd] = s0 * upd[:, HEAD_DIM:] + upd[:, :HEAD_DIM]
            gate = jnp.concatenate([token_slab(pa_ref, 3 * A_HEADS + hd, t) for t in range(t_dec)], axis=0)
            on = _head_out(o + o_state, ag_ref[...], gate)
            for t in range(t_dec):
                oa_ref[hd, pl.ds(r0 + t, SUBLANES, stride=t_dec), :] = on[t * SUBLANES:(t + 1) * SUBLANES, :]

    for sub in range(DEC_GROUPS):
        group(sub)

    @pl.when(g == pl.num_programs(0) - 1)
    def _():
        def out_body(r, carry):
            rows = pl.ds(pl.multiple_of(r * ROW_TILE, ROW_TILE), ROW_TILE)
            cat = jnp.concatenate([oa_ref[hd, rows, :] for hd in range(A_HEADS)]
                                  + [ob_ref[ls, rows, :] for ls in range(n_ls)], axis=-1).astype(_BF16)
            out = _dot(cat, wout_ref[...])
            y_ref[rows, :] = _rmsnorm(x_ref[rows, :] + out, fg_ref[...])
            return carry
        lax.fori_loop(0, n_tok // ROW_TILE, out_body, 0)


def _decode_call(x, s0, c0, weights):
    n, t_dec, _ = x.shape
    assert n % (DEC_GROUPS * SUBLANES) == 0 and t_dec == 4 and (n * t_dec) % ROW_TILE == 0
    n_tok = n * t_dec
    x = x.reshape(n_tok, D_MODEL)
    masks = jnp.asarray(_level_masks((1, 2), t_dec, SUBLANES, same_group=t_dec))
    kern = functools.partial(_decode_kernel, t_dec=t_dec)
    w_specs = [_full(w.shape) for w in weights]
    per_step = DEC_GROUPS * SUBLANES
    state_spec = pl.BlockSpec((per_step, A_HEADS, HEAD_DIM, HEAD_DIM), lambda i: (i, 0, 0, 0))
    conv_spec = pl.BlockSpec((CONV_WIDTH - 1, per_step, B_WIDTH), lambda i: (0, i, 0))
    y, s_new, c_new = pl.pallas_call(
        kern,
        grid=(n // per_step,),
        in_specs=[_full((n_tok, D_MODEL)), state_spec, conv_spec] + w_specs + [_full(masks.shape)],
        out_specs=[_full((n_tok, D_MODEL)), state_spec, conv_spec],
        out_shape=[
            jax.ShapeDtypeStruct((n_tok, D_MODEL), _F32),
            jax.ShapeDtypeStruct((n, A_HEADS, HEAD_DIM, HEAD_DIM), _F32),
            jax.ShapeDtypeStruct((CONV_WIDTH - 1, n, B_WIDTH), _F32),
        ],
        scratch_shapes=[
            pltpu.VMEM((4 * A_HEADS, n_tok, LANES), _F32),
            pltpu.VMEM((B_WIDTH // LANES, n_tok, LANES), _F32),
            pltpu.VMEM((B_WIDTH // LANES, n_tok, LANES), _F32),
            pltpu.VMEM((A_HEADS, n_tok, LANES), _F32),
            pltpu.VMEM((B_WIDTH // LANES, n_tok, LANES), _F32),
        ],
        compiler_params=pltpu.CompilerParams(
            dimension_semantics=("arbitrary",),
            vmem_limit_bytes=VMEM_LIMIT_BYTES,
        ),
        name="hymba_decode",
    )(x, s0, c0, *weights, masks)
    return y.reshape(n, t_dec, D_MODEL), s_new, c_new


def _prepare_weights(norm_in_g, w_in, lb_logits, hgrn_norm_g, b_glu, conv_w, conv_b, ln_g, ln_b, w_out, final_norm_g, layer):
    lb = jnp.cumsum(jax.nn.softmax(lb_logits.astype(_F32), axis=0), axis=0)[layer]
    row = lambda a: a.astype(_F32).reshape(1, -1)
    return (
        row(norm_in_g[layer]),
        w_in[layer].astype(_F32),
        row(lb),
        row(hgrn_norm_g[layer]),
        row(b_glu[layer]),
        conv_w[layer].astype(_F32),
        row(conv_b[layer]),
        row(ln_g[layer]),
        row(ln_b[layer]),
        w_out[layer].astype(_F32),
        row(final_norm_g),
    )


def kernel(x_prompt, x_sample, state_hgrn, state_conv, norm_in_g, w_in, lb_logits, hgrn_norm_g, b_glu, conv_w,
           conv_b, ln_g, ln_b, w_out, final_norm_g):
    depth = w_in.shape[0]
    assert depth == 1, "single mixer layer: the final norm is fused into the layer kernel"
    weights = _prepare_weights(norm_in_g, w_in, lb_logits, hgrn_norm_g, b_glu, conv_w, conv_b, ln_g, ln_b, w_out,
                               final_norm_g, 0)
    y_p, s_p, c_p, w_in_b, w_out_b = _prompt_call(x_prompt, weights, tb=1024)
    weights = weights[:1] + (w_in_b,) + weights[2:9] + (w_out_b,) + weights[10:]
    y_s, s_s, c_s = _decode_call(x_sample, state_hgrn[0], jnp.transpose(state_conv[0], (1, 0, 2)), weights)
    return (y_p, y_s, s_p[None], c_p[None], s_s[None], jnp.transpose(c_s, (1, 0, 2))[None])
```

```python
import functools
import math

import numpy as np
import jax
import jax.numpy as jnp
from jax import lax
from jax.experimental import pallas as pl
from jax.experimental.pallas import tpu as pltpu

D_MODEL = 1024
A_HEADS = 4
HEAD_DIM = 128
A_WIDTH = A_HEADS * HEAD_DIM
B_WIDTH = 512
CONV_WIDTH = 31
EPS = 1e-6
IN_WIDTH = 4 * A_WIDTH + 3 * B_WIDTH
GLU_OFF = 4 * A_WIDTH
ZB_OFF = GLU_OFF + 2 * B_WIDTH

LANES = 128
SUBLANES = 8
MXU_N = 256
CHUNK = SUBLANES * SUBLANES
ROW_TILE = 128
UNROLL = 8
CHUNK_UNROLL = 2
CHUNK_LAG = 4
CONV_BUFS = 3
DEC_GROUPS = 2
HIST = 32
FIRST = HIST - (CONV_WIDTH - 1)
VMEM_LIMIT_BYTES = 56 * 1024 * 1024
LOG2E = math.log2(math.e)

_F32 = jnp.float32
_BF16 = jnp.bfloat16


def _sigmoid(x):
    return 1.0 / (1.0 + jnp.exp(-x))


def _silu(x):
    return x * _sigmoid(x)


def _dot(a, b):
    return jnp.dot(a, b, preferred_element_type=_F32)


def _dot_nt(a, b):
    return lax.dot_general(a, b, (((1,), (1,)), ((), ())), preferred_element_type=_F32)


def _dot_tn(a, b):
    return lax.dot_general(a, b, (((0,), (0,)), ((), ())), preferred_element_type=_F32)


def _level_masks(levels, n_res, n_rows, same_group=None):
    r = np.arange(n_res * n_rows)
    t = n_res * (r % n_rows) + r // n_rows
    tt, ss = t[:, None], t[None, :]
    masks = [tt == ss]
    for b in levels:
        masks.append(((tt // b) == (ss // b) + 1) & ((ss // b) % 2 == 0))
    masks = np.stack(masks)
    want = (ss <= tt) if same_group is None else ((ss <= tt) & (tt // same_group == ss // same_group))
    assert (masks.sum(0) == want).all()
    return masks.astype(np.float32)


def _rows(x, r):
    return jnp.broadcast_to(x[r:r + 1, :], x.shape)


def _hgrn_stages(q, f, v, masks_ref, get_st, coarse, done):
    n_res = len(q)
    every = range(n_res)
    k = [1.0 - fi for fi in f]
    lf = [jnp.log(fi) * LOG2E for fi in f]
    g_in = [lf[0]]
    for i in range(1, n_res):
        g_in.append(g_in[-1] + lf[i])
    tot = g_in[-1]
    zero = jnp.zeros_like(tot)

    def cat(parts):
        return jnp.concatenate(parts, axis=0).astype(_BF16)

    def slab_rows(x, i):
        return x[i * SUBLANES:(i + 1) * SUBLANES, :]

    v_all = cat(v)
    s_rows = [None] * n_res
    pending = []

    def fold():
        level, rows, s_l = pending.pop(0)
        for n, i in enumerate(rows):
            term = masks_ref[level, i * SUBLANES:(i + 1) * SUBLANES, :] * slab_rows(s_l, n)
            s_rows[i] = term if s_rows[i] is None else s_rows[i] + term

    def step(rows, qt, kt):
        pending.append((step.level, rows, _dot_nt(cat(qt), cat(kt))))
        step.level += 1
    step.level = 0

    step(every, q, k)
    yield
    b = 1
    while b < n_res:
        odd = [i for i in every if (i // b) % 2 == 1]
        qt, kt = [], []
        for i in every:
            bs = (i // b) * b
            if i in odd:
                qt.append(q[i] * jnp.exp2(g_in[i] - g_in[bs - 1]))
                kt.append(zero)
            else:
                be = bs + b - 1
                kt.append(k[i] * jnp.exp2(g_in[be] - g_in[i]) if i != be else k[i])
        step(odd, qt, kt)
        yield
        fold()
        b *= 2
    qe = [q[i] * jnp.exp2(g_in[i]) for i in every]
    ke = [k[i] * jnp.exp2(tot - g_in[i]) for i in range(n_res - 1)] + [k[-1]]
    if coarse:
        row = lax.broadcasted_iota(jnp.int32, tot.shape, 0)
        pref = tot
        for sh in (1, 2, 4):
            pref = pref + jnp.where(row >= sh, pltpu.roll(pref, sh, 0), 0.0)
        before = pref - tot
        end_all = _rows(pref, SUBLANES - 1)
        qe_b, ke_b = cat(qe), cat(ke)
        tile = lambda x: jnp.concatenate([x] * n_res, axis=0).astype(_BF16)
        for gsz in (1, 2, 4):
            if gsz == 1:
                qt, kt = qe_b, ke_b
            else:
                if gsz == 2:
                    g_start = jnp.where(row % 2 == 0, before, pltpu.roll(before, 1, 0))
                    g_end = jnp.where(row % 2 == 1, pref, pltpu.roll(pref, SUBLANES - 1, 0))
                else:
                    g_start = jnp.where(row < 4, _rows(before, 0), _rows(before, 4))
                    g_end = jnp.where(row < 4, _rows(pref, 3), _rows(pref, 7))
                qt = qe_b * tile(jnp.exp2(before - g_start))
                kt = ke_b * tile(jnp.exp2(g_end - pref))
            pending.append((step.level, every, _dot_nt(qt, kt)))
            step.level += 1
            yield
            fold()
        qc = qe_b * tile(jnp.exp2(before))
        kc = ke_b * tile(jnp.exp2(end_all - pref))
        o_st = _dot_nt(qc, get_st().astype(_BF16))
        decay = jnp.exp2(pref[SUBLANES - 1:SUBLANES, :])
        yield
    else:
        qc, kc = cat(qe), cat(ke)
        o_st = None
        decay = jnp.exp2(tot)
    while pending:
        fold()
    o = _dot(cat(s_rows), v_all)
    yield
    done(o if o_st is None else o + o_st, qc, kc, v_all, decay)


def _round_robin(starts):
    waiting = list(starts)
    live = []
    rnd = 0
    while waiting or live:
        if waiting and rnd % CHUNK_LAG == 0:
            live += waiting.pop(0)
        for g in list(live):
            try:
                next(g)
            except StopIteration:
                live.remove(g)
        rnd += 1


def _aligned(x, m):
    return x if isinstance(x, int) else pl.multiple_of(x, m)


def _loop(trips, body):
    if trips == 1:
        body(0, 0)
    else:
        lax.fori_loop(0, trips, body, 0)


def _head_out(o, ag, gate):
    ms = jnp.mean(o * o, axis=-1, keepdims=True)
    return o * lax.rsqrt(ms + EPS) * ag * gate


def _project_a(h, win_ref, lb_ref, store, cg):
    p = _dot(h(), win_ref[:, cg * MXU_N:(cg + 1) * MXU_N])
    arr = cg // 2
    if arr == 0 or arr == 3:
        p = _silu(p)
    elif arr == 1:
        lb = lb_ref[:, (cg % 2) * MXU_N:(cg % 2 + 1) * MXU_N]
        p = lb + (1.0 - lb) * _sigmoid(p)
    store(2 * cg, p[:, :LANES])
    store(2 * cg + 1, p[:, LANES:])


def _project_b(h, win_ref, bglu_ref, half):
    c0 = half * MXU_N
    glu_a = _dot(h(), win_ref[:, GLU_OFF + c0:GLU_OFF + c0 + MXU_N]) + bglu_ref[:, c0:c0 + MXU_N]
    glu_b = (_dot(h(), win_ref[:, GLU_OFF + B_WIDTH + c0:GLU_OFF + B_WIDTH + c0 + MXU_N])
             + bglu_ref[:, B_WIDTH + c0:B_WIDTH + c0 + MXU_N])
    zb = _dot(h(), win_ref[:, ZB_OFF + c0:ZB_OFF + c0 + MXU_N])
    return glu_a * _sigmoid(glu_b), _silu(zb)


def _rmsnorm(x, g):
    return x * lax.rsqrt(jnp.mean(x * x, axis=-1, keepdims=True) + EPS) * g


def _group_b_out(cv, lng_ref, lnb_ref, gate_b):
    mu = jnp.mean(cv, axis=-1, keepdims=True)
    d = cv - mu
    var = jnp.mean(d * d, axis=-1, keepdims=True)
    ln = d * lax.rsqrt(var + EPS) * lng_ref[...] + lnb_ref[...]
    return _silu(ln) * gate_b


def _cast_weight(src_hbm, dst_ref, slab_refs, sems):
    n = dst_ref.shape[1] // LANES
    slots = len(slab_refs)

    def copy(j):
        return pltpu.make_async_copy(src_hbm.at[:, pl.ds(j * LANES, LANES)], slab_refs[j % slots], sems.at[j % slots])

    for j in range(min(slots, n)):
        copy(j).start()
    for j in range(n):
        copy(j).wait()
        dst_ref[:, j * LANES:(j + 1) * LANES] = slab_refs[j % slots][...].astype(_BF16)
        if j + slots < n:
            copy(j + slots).start()


def _conv_stages(ua_ref, ub_ref, wb_ref, convb_ref, lng_ref, lnb_ref, gb_ref, cv_ref, cat_ref, base, n):
    reps = lambda w, rows: jnp.concatenate([w] * (rows // (2 * SUBLANES)), axis=0)
    for ls in range(B_WIDTH // LANES):
        lanes = slice(ls * LANES, (ls + 1) * LANES)
        acc = jnp.broadcast_to(convb_ref[:, lanes], (n, LANES))
        for off in range(SUBLANES):
            taps = [j for j in range(CONV_WIDTH) if (FIRST + j) % SUBLANES == off]
            rows = n + (2 * SUBLANES if off else 0)
            part = None
            for j in taps:
                rel = FIRST + j - off
                if rel % (2 * SUBLANES) == 0:
                    win = ua_ref[pl.ds(_aligned(base + rel, 2 * SUBLANES), rows), lanes]
                else:
                    win = ub_ref[pl.ds(_aligned(base + rel + SUBLANES, 2 * SUBLANES), rows), lanes]
                term = reps(wb_ref[j, :, lanes], rows) * win
                part = term if part is None else part + term
            part = part.astype(_F32)
            acc = acc + (part[off:off + n, :] if off else part)
            if off == SUBLANES // 2 - 1:
                yield
        cv_ref[:, lanes] = acc
        yield
    rows = pl.ds(_aligned(base, n), n)
    o_b = _group_b_out(cv_ref[...], lng_ref, lnb_ref, gb_ref[rows, :])
    cat_ref[rows, A_WIDTH:] = o_b.astype(_BF16)


def _prompt_kernel(x_ref, normg_ref, win_hbm, lb_ref, ag_ref, bglu_ref, convw_ref, convb_ref,
                   lng_ref, lnb_ref, wout_hbm, fg_ref, masks_ref, wb_ref,
                   y_ref, st_out_ref, cs_out_ref, win_out_hbm, wout_out_hbm,
                   pa_ref, uh_ref, ua_ref, ub_ref, gb_ref, cv_ref, oa_ref, cat_ref, st_ref,
                   win_ref, wout_ref, wout_f32_ref, sems, *, tb):
    t = pl.program_id(1)
    nt = pl.num_programs(1)

    first = (pl.program_id(0) == 0) & (t == 0)
    n_slabs = pa_ref.shape[0]
    handoff = [pltpu.make_async_copy(win_ref, win_out_hbm, sems.at[n_slabs]),
               pltpu.make_async_copy(wout_ref, wout_out_hbm, sems.at[n_slabs + 1])]
    wout_fetch = pltpu.make_async_copy(wout_hbm, wout_f32_ref, sems.at[n_slabs + 2])

    @pl.when(first)
    def _():
        _cast_weight(win_hbm, win_ref, [pa_ref.at[s, pl.ds(0, win_ref.shape[0])] for s in range(n_slabs)], sems)
        wout_fetch.start()
        handoff[0].start()

    @pl.when(t == 0)
    def _():
        st_ref[...] = jnp.zeros_like(st_ref)
        uh_ref[...] = jnp.zeros_like(uh_ref)
        ua_ref[...] = jnp.zeros_like(ua_ref)
        ub_ref[...] = jnp.zeros_like(ub_ref)

    def front_tile(r0):
        rows = pl.ds(r0, ROW_TILE)
        h = _rmsnorm(x_ref[rows, :], normg_ref[...]).astype(_BF16)
        for half in range(B_WIDTH // MXU_N):
            u, gb = _project_b(lambda: h, win_ref, bglu_ref, half)
            cols = slice(half * MXU_N, (half + 1) * MXU_N)
            here = pl.ds(_aligned(HIST + r0, HIST), ROW_TILE)
            prev = uh_ref[pl.ds(_aligned(HIST + r0 - SUBLANES, SUBLANES), SUBLANES), cols]
            uh_ref[here, cols] = u
            ua_ref[here, cols] = u.astype(_BF16)
            ub_ref[here, cols] = jnp.concatenate([prev, u[:ROW_TILE - SUBLANES, :]], axis=0).astype(_BF16)
            gb_ref[rows, cols] = gb

        def store_a(slab, val):
            pa_ref[slab, rows, :] = val
        for cg in range(2 * A_HEADS):
            _project_a(lambda: h, win_ref, lb_ref, store_a, cg)

    def front_body(p, carry):
        for n in range(UNROLL):
            front_tile(_aligned((UNROLL * p + n) * ROW_TILE, ROW_TILE))
        return carry

    _loop(tb // (UNROLL * ROW_TILE), front_body)
    ub_ref[HIST + tb:HIST + tb + 2 * SUBLANES, :] = jnp.concatenate(
        [uh_ref[HIST + tb - SUBLANES:HIST + tb, :], jnp.zeros((SUBLANES, B_WIDTH), _F32)], axis=0).astype(_BF16)

    def chunk(base, slot):
        def head(hd):
            def slabs(arr):
                return [pa_ref[arr * A_HEADS + hd, pl.ds(base + i, SUBLANES, stride=SUBLANES), :]
                        for i in range(SUBLANES)]

            def done(o, qc, kc, v_all, decay):
                st_ref[hd] = st_ref[hd] * decay + _dot_tn(v_all, kc)
                for i in range(SUBLANES):
                    oa_ref[hd, pl.ds(base + i, SUBLANES, stride=SUBLANES), :] = o[i * SUBLANES:(i + 1) * SUBLANES, :]
            return _hgrn_stages(slabs(0), slabs(1), slabs(2), masks_ref, lambda: st_ref[hd], True, done)
        conv = _conv_stages(ua_ref, ub_ref, wb_ref, convb_ref, lng_ref, lnb_ref, gb_ref, cv_ref.at[slot], cat_ref, base, CHUNK)
        return [head(hd) for hd in range(A_HEADS)] + [conv]

    def chunk_body(p, carry):
        _round_robin([chunk(_aligned((CHUNK_UNROLL * p + n) * CHUNK, CHUNK), n % CONV_BUFS) for n in range(CHUNK_UNROLL)])
        return carry

    _loop(tb // (CHUNK_UNROLL * CHUNK), chunk_body)

    @pl.when(t == nt - 1)
    def _():
        cs_out_ref[...] = uh_ref[HIST + tb - (CONV_WIDTH - 1):HIST + tb, :]
        for hd in range(A_HEADS):
            st_out_ref[hd] = st_ref[hd].T

    uh_ref[0:HIST, :] = uh_ref[tb:tb + HIST, :]
    ua_ref[0:HIST, :] = ua_ref[tb:tb + HIST, :]
    ub_ref[0:HIST, :] = ub_ref[tb:tb + HIST, :]

    @pl.when(first)
    def _():
        wout_fetch.wait()
        for r0 in range(0, wout_ref.shape[0], ROW_TILE):
            wout_ref[r0:r0 + ROW_TILE, :] = wout_f32_ref[r0:r0 + ROW_TILE, :].astype(_BF16)
        handoff[1].start()

    def back_body(p, carry):
        tiles = [pl.ds(_aligned((UNROLL * p + n) * ROW_TILE, ROW_TILE), ROW_TILE) for n in range(UNROLL)]
        for rows in tiles:
            for hd in range(A_HEADS):
                on = _head_out(oa_ref[hd, rows, :], ag_ref[...], pa_ref[3 * A_HEADS + hd, rows, :])
                cat_ref[rows, hd * HEAD_DIM:(hd + 1) * HEAD_DIM] = on.astype(_BF16)
        outs = [_dot(cat_ref[rows, :], wout_ref[...]) for rows in tiles]
        for rows, out in zip(tiles, outs):
            y_ref[rows, :] = _rmsnorm(x_ref[rows, :] + out, fg_ref[...])
        return carry

    _loop(tb // (UNROLL * ROW_TILE), back_body)

    @pl.when(first)
    def _():
        for copy in handoff:
            copy.wait()


def _full(shape):
    nd = len(shape)
    return pl.BlockSpec(shape, lambda *_: (0,) * nd, pipeline_mode=pl.Buffered(1))


def _prompt_call(x, weights, tb):
    n, t, _ = x.shape
    assert t % tb == 0 and tb % (UNROLL * ROW_TILE) == 0 and tb % (CHUNK_UNROLL * CHUNK) == 0
    masks = jnp.asarray(_level_masks((1, 2, 4, 8, 16, 32), SUBLANES, SUBLANES))
    conv_w = weights[5]
    wb = jnp.broadcast_to(conv_w.astype(_BF16)[:, None, :], (CONV_WIDTH, 2 * SUBLANES, B_WIDTH))
    kern = functools.partial(_prompt_kernel, tb=tb)
    w_in, w_out = weights[1], weights[9]
    assert w_in.dtype == _F32 and w_out.dtype == _F32 and w_in.shape[0] <= tb and w_in.shape[1] % LANES == 0
    in_hbm = pl.BlockSpec(memory_space=pl.ANY)
    w_specs = [in_hbm if w is w_in or w is w_out else _full(w.shape) for w in weights]
    return pl.pallas_call(
        kern,
        grid=(n, t // tb),
        in_specs=([pl.BlockSpec((None, tb, D_MODEL), lambda i, j: (i, j, 0))] + w_specs
                  + [_full(masks.shape), _full(wb.shape)]),
        out_specs=[
            pl.BlockSpec((None, tb, D_MODEL), lambda i, j: (i, j, 0)),
            pl.BlockSpec((None, A_HEADS, HEAD_DIM, HEAD_DIM), lambda i, j: (i, 0, 0, 0)),
            pl.BlockSpec((None, CONV_WIDTH - 1, B_WIDTH), lambda i, j: (i, 0, 0)),
            in_hbm,
            in_hbm,
        ],
        out_shape=[
            jax.ShapeDtypeStruct((n, t, D_MODEL), _F32),
            jax.ShapeDtypeStruct((n, A_HEADS, HEAD_DIM, HEAD_DIM), _F32),
            jax.ShapeDtypeStruct((n, CONV_WIDTH - 1, B_WIDTH), _F32),
            jax.ShapeDtypeStruct(w_in.shape, _BF16),
            jax.ShapeDtypeStruct(w_out.shape, _BF16),
        ],
        scratch_shapes=[
            pltpu.VMEM((4 * A_HEADS, tb, LANES), _F32),
            pltpu.VMEM((HIST + tb, B_WIDTH), _F32),
            pltpu.VMEM((HIST + tb + 2 * SUBLANES, B_WIDTH), _BF16),
            pltpu.VMEM((HIST + tb + 2 * SUBLANES, B_WIDTH), _BF16),
            pltpu.VMEM((tb, B_WIDTH), _F32),
            pltpu.VMEM((CONV_BUFS, CHUNK, B_WIDTH), _F32),
            pltpu.VMEM((A_HEADS, tb, LANES), _F32),
            pltpu.VMEM((tb, A_WIDTH + B_WIDTH), _BF16),
            pltpu.VMEM((A_HEADS, HEAD_DIM, HEAD_DIM), _F32),
            pltpu.VMEM(w_in.shape, _BF16),
            pltpu.VMEM(w_out.shape, _BF16),
            pltpu.VMEM(w_out.shape, _F32),
            pltpu.SemaphoreType.DMA((4 * A_HEADS + 3,)),
        ],
        compiler_params=pltpu.CompilerParams(
            dimension_semantics=("arbitrary", "arbitrary"),
            vmem_limit_bytes=VMEM_LIMIT_BYTES,
        ),
        name="hymba_prompt",
    )(x, *weights, masks, wb)


def _decode_kernel(x_ref, s0_ref, c0_ref, normg_ref, win_ref, lb_ref, ag_ref, bglu_ref, convw_ref, convb_ref,
                   lng_ref, lnb_ref, wout_ref, fg_ref, masks_ref,
                   y_ref, s_out_ref, cs_out_ref,
                   pa_ref, ub_ref, gb_ref, oa_ref, ob_ref, *, t_dec):
    g = pl.program_id(0)
    n_seq = x_ref.shape[0]
    grp = SUBLANES * t_dec
    n_ls = B_WIDTH // LANES
    tiles = [(pl.ds(t * n_seq + s0, ROW_TILE), pl.ds(s0, ROW_TILE), t)
             for t in range(t_dec) for s0 in range(0, n_seq, ROW_TILE)]

    @pl.when(g == 0)
    def _():
        for rows, seq_rows, t in tiles:
            h = _rmsnorm(x_ref[seq_rows, t, :], normg_ref[...]).astype(_BF16)
            for half in range(B_WIDTH // MXU_N):
                u, gb = _project_b(lambda: h, win_ref, bglu_ref, half)
                for c in range(MXU_N // LANES):
                    ls = half * (MXU_N // LANES) + c
                    ub_ref[ls, rows, :] = u[:, c * LANES:(c + 1) * LANES]
                    gb_ref[ls, rows, :] = gb[:, c * LANES:(c + 1) * LANES]

            def store_a(slab, val, rows=rows):
                pa_ref[slab, rows, :] = val
            for cg in range(2 * A_HEADS):
                _project_a(lambda: h, win_ref, lb_ref, store_a, cg)

    def group(sub):
        seqs = slice(sub * SUBLANES, (sub + 1) * SUBLANES)
        seq0 = pl.multiple_of((g * DEC_GROUPS + sub) * SUBLANES, SUBLANES)
        token_rows = lambda t: pl.ds(t * n_seq + seq0, SUBLANES)

        def token_slab(ref, idx, t):
            return ref[idx, token_rows(t), :]

        u_tok = [jnp.concatenate([token_slab(ub_ref, ls, t) for ls in range(n_ls)], axis=-1) for t in range(t_dec)]
        full = lambda kk: c0_ref[kk, seqs, :] if kk < CONV_WIDTH - 1 else u_tok[kk - (CONV_WIDTH - 1)]
        for kk in range(CONV_WIDTH - 1):
            cs_out_ref[kk, seqs, :] = full(kk + t_dec)
        for t in range(t_dec):
            cv = jnp.broadcast_to(convb_ref[...], (SUBLANES, B_WIDTH))
            for j in range(CONV_WIDTH):
                cv = cv + convw_ref[j:j + 1, :] * full(t + j)
            gate_b = jnp.concatenate([token_slab(gb_ref, ls, t) for ls in range(n_ls)], axis=-1)
            o_b = _group_b_out(cv, lng_ref, lnb_ref, gate_b)
            for ls in range(n_ls):
                ob_ref[ls, token_rows(t), :] = o_b[:, ls * LANES:(ls + 1) * LANES]

        row32 = lax.broadcasted_iota(jnp.int32, (grp, LANES), 0) % SUBLANES
        row64 = lax.broadcasted_iota(jnp.int32, (2 * grp, LANES), 0)
        for hd in range(A_HEADS):
            slabs = lambda arr: [token_slab(pa_ref, arr * A_HEADS + hd, t) for t in range(t_dec)]
            res = []
            for _ in _hgrn_stages(slabs(0), slabs(1), slabs(2), masks_ref, None, False, lambda *a: res.extend(a)):
                pass
            o, qc, kc, v_all, decay = res
            qf, kf, vf = qc.astype(_F32), kc.astype(_F32), v_all.astype(_F32)
            e_hi = decay.astype(_BF16).astype(_F32)
            e_mid = (decay - e_hi).astype(_BF16).astype(_F32)
            e_lo = (decay - e_hi - e_mid).astype(_BF16).astype(_F32)
            lhs = jnp.concatenate([kf, e_hi, e_mid, e_lo, jnp.zeros((SUBLANES, LANES), _F32)], axis=0).astype(_BF16)
            v_pad = jnp.concatenate([vf, jnp.zeros((grp, LANES), _F32)], axis=0)
            o_state = None
            for s in range(SUBLANES):
                s0 = s0_ref[sub * SUBLANES + s, hd]
                term = _dot(jnp.where(row32 == s, qf, 0.0).astype(_BF16), s0.astype(_BF16))
                o_state = term if o_state is None else o_state + term
                mine = row64 % SUBLANES == s
                rhs = jnp.concatenate([
                    jnp.where(mine & (row64 < grp), v_pad, 0.0),
                    jnp.where(mine & (row64 >= grp) & (row64 < grp + 3 * SUBLANES), 1.0, 0.0)], axis=-1).astype(_BF16)
                upd = _dot_tn(lhs, rhs)
                s_out_ref[sub * SUBLANES + s, hd] = s0 * upd[:, HEAD_DIM:] + upd[:, :HEAD_DIM]
            gate = jnp.concatenate([token_slab(pa_ref, 3 * A_HEADS + hd, t) for t in range(t_dec)], axis=0)
            on = _head_out(o + o_state, ag_ref[...], gate)
            for t in range(t_dec):
                oa_ref[hd, token_rows(t), :] = on[t * SUBLANES:(t + 1) * SUBLANES, :]

    for sub in range(DEC_GROUPS):
        group(sub)

    @pl.when(g == pl.num_programs(0) - 1)
    def _():
        for rows, seq_rows, t in tiles:
            cat = jnp.concatenate([oa_ref[hd, rows, :] for hd in range(A_HEADS)]
                                  + [ob_ref[ls, rows, :] for ls in range(n_ls)], axis=-1).astype(_BF16)
            out = _dot(cat, wout_ref[...])
            y_ref[seq_rows, t, :] = _rmsnorm(x_ref[seq_rows, t, :] + out, fg_ref[...])


def _decode_call(x, s0, c0, weights):
    n, t_dec, _ = x.shape
    assert n % ROW_TILE == 0 and t_dec == 4
    n_tok = n * t_dec
    masks = jnp.asarray(_level_masks((1, 2), t_dec, SUBLANES, same_group=t_dec))
    kern = functools.partial(_decode_kernel, t_dec=t_dec)
    w_specs = [_full(w.shape) for w in weights]
    per_step = DEC_GROUPS * SUBLANES
    state_spec = pl.BlockSpec((per_step, A_HEADS, HEAD_DIM, HEAD_DIM), lambda i: (i, 0, 0, 0))
    conv_spec = pl.BlockSpec((CONV_WIDTH - 1, per_step, B_WIDTH), lambda i: (0, i, 0))
    y, s_new, c_new = pl.pallas_call(
        kern,
        grid=(n // per_step,),
        in_specs=[_full(x.shape), state_spec, conv_spec] + w_specs + [_full(masks.shape)],
        out_specs=[_full(x.shape), state_spec, conv_spec],
        out_shape=[
            jax.ShapeDtypeStruct(x.shape, _F32),
            jax.ShapeDtypeStruct((n, A_HEADS, HEAD_DIM, HEAD_DIM), _F32),
            jax.ShapeDtypeStruct((CONV_WIDTH - 1, n, B_WIDTH), _F32),
        ],
        scratch_shapes=[
            pltpu.VMEM((4 * A_HEADS, n_tok, LANES), _F32),
            pltpu.VMEM((B_WIDTH // LANES, n_tok, LANES), _F32),
            pltpu.VMEM((B_WIDTH // LANES, n_tok, LANES), _F32),
            pltpu.VMEM((A_HEADS, n_tok, LANES), _F32),
            pltpu.VMEM((B_WIDTH // LANES, n_tok, LANES), _F32),
        ],
        compiler_params=pltpu.CompilerParams(
            dimension_semantics=("arbitrary",),
            vmem_limit_bytes=VMEM_LIMIT_BYTES,
        ),
        name="hymba_decode",
    )(x, s0, c0, *weights, masks)
    return y, s_new, c_new


def _prepare_weights(norm_in_g, w_in, lb_logits, hgrn_norm_g, b_glu, conv_w, conv_b, ln_g, ln_b, w_out, final_norm_g, layer):
    lb = jnp.cumsum(jax.nn.softmax(lb_logits.astype(_F32), axis=0), axis=0)[layer]
    row = lambda a: a.astype(_F32).reshape(1, -1)
    return (
        row(norm_in_g[layer]),
        w_in[layer].astype(_F32),
        row(lb),
        row(hgrn_norm_g[layer]),
        row(b_glu[layer]),
        conv_w[layer].astype(_F32),
        row(conv_b[layer]),
        row(ln_g[layer]),
        row(ln_b[layer]),
        w_out[layer].astype(_F32),
        row(final_norm_g),
    )


def kernel(x_prompt, x_sample, state_hgrn, state_conv, norm_in_g, w_in, lb_logits, hgrn_norm_g, b_glu, conv_w,
           conv_b, ln_g, ln_b, w_out, final_norm_g):
    depth = w_in.shape[0]
    assert depth == 1, "single mixer layer: the final norm is fused into the layer kernel"
    weights = _prepare_weights(norm_in_g, w_in, lb_logits, hgrn_norm_g, b_glu, conv_w, conv_b, ln_g, ln_b, w_out,
                               final_norm_g, 0)
    y_p, s_p, c_p, w_in_b, w_out_b = _prompt_call(x_prompt, weights, tb=1024)
    weights = weights[:1] + (w_in_b,) + weights[2:9] + (w_out_b,) + weights[10:]
    y_s, s_s, c_s = _decode_call(x_sample, state_hgrn[0], jnp.transpose(state_conv[0], (1, 0, 2)), weights)
    return (y_p, y_s, s_p[None], c_p[None], s_s[None], jnp.transpose(c_s, (1, 0, 2))[None])
```

```python
import functools
import math

import numpy as np
import jax
import jax.numpy as jnp
from jax import lax
from jax.experimental import pallas as pl
from jax.experimental.pallas import tpu as pltpu

D_MODEL = 1024
A_HEADS = 4
HEAD_DIM = 128
A_WIDTH = A_HEADS * HEAD_DIM
B_WIDTH = 512
CONV_WIDTH = 31
EPS = 1e-6
IN_WIDTH = 4 * A_WIDTH + 3 * B_WIDTH
GLU_OFF = 4 * A_WIDTH
ZB_OFF = GLU_OFF + 2 * B_WIDTH

LANES = 128
SUBLANES = 8
MXU_N = 256
CHUNK = SUBLANES * SUBLANES
ROW_TILE = 128
UNROLL = 8
CHUNK_UNROLL = 2
CHUNK_LAG = 4
CONV_BUFS = 3
DEC_GROUPS = 2
HIST = 32
FIRST = HIST - (CONV_WIDTH - 1)
VMEM_LIMIT_BYTES = 56 * 1024 * 1024
LOG2E = math.log2(math.e)

_F32 = jnp.float32
_BF16 = jnp.bfloat16


def _sigmoid(x):
    return 1.0 / (1.0 + jnp.exp(-x))


def _silu(x):
    return x * _sigmoid(x)


def _dot(a, b):
    return jnp.dot(a, b, preferred_element_type=_F32)


def _dot_nt(a, b):
    return lax.dot_general(a, b, (((1,), (1,)), ((), ())), preferred_element_type=_F32)


def _dot_tn(a, b):
    return lax.dot_general(a, b, (((0,), (0,)), ((), ())), preferred_element_type=_F32)


def _level_masks(levels, n_res, n_rows, same_group=None):
    r = np.arange(n_res * n_rows)
    t = n_res * (r % n_rows) + r // n_rows
    tt, ss = t[:, None], t[None, :]
    masks = [tt == ss]
    for b in levels:
        masks.append(((tt // b) == (ss // b) + 1) & ((ss // b) % 2 == 0))
    masks = np.stack(masks)
    want = (ss <= tt) if same_group is None else ((ss <= tt) & (tt // same_group == ss // same_group))
    assert (masks.sum(0) == want).all()
    return masks.astype(np.float32)


def _rows(x, r):
    return jnp.broadcast_to(x[r:r + 1, :], x.shape)


def _hgrn_stages(q, f, v, masks_ref, get_st, coarse, done):
    n_res = len(q)
    every = range(n_res)
    k = [1.0 - fi for fi in f]
    lf = [jnp.log(fi) * LOG2E for fi in f]
    g_in = [lf[0]]
    for i in range(1, n_res):
        g_in.append(g_in[-1] + lf[i])
    tot = g_in[-1]
    zero = jnp.zeros_like(tot)

    def cat(parts):
        return jnp.concatenate(parts, axis=0).astype(_BF16)

    def slab_rows(x, i):
        return x[i * SUBLANES:(i + 1) * SUBLANES, :]

    v_all = cat(v)
    s_rows = [None] * n_res
    pending = []

    def fold():
        level, rows, s_l = pending.pop(0)
        for n, i in enumerate(rows):
            term = masks_ref[level, i * SUBLANES:(i + 1) * SUBLANES, :] * slab_rows(s_l, n)
            s_rows[i] = term if s_rows[i] is None else s_rows[i] + term

    def step(rows, qt, kt):
        pending.append((step.level, rows, _dot_nt(cat(qt), cat(kt))))
        step.level += 1
    step.level = 0

    step(every, q, k)
    yield
    b = 1
    while b < n_res:
        odd = [i for i in every if (i // b) % 2 == 1]
        qt, kt = [], []
        for i in every:
            bs = (i // b) * b
            if i in odd:
                qt.append(q[i] * jnp.exp2(g_in[i] - g_in[bs - 1]))
                kt.append(zero)
            else:
                be = bs + b - 1
                kt.append(k[i] * jnp.exp2(g_in[be] - g_in[i]) if i != be else k[i])
        step(odd, qt, kt)
        yield
        fold()
        b *= 2
    qe = [q[i] * jnp.exp2(g_in[i]) for i in every]
    ke = [k[i] * jnp.exp2(tot - g_in[i]) for i in range(n_res - 1)] + [k[-1]]
    if coarse:
        row = lax.broadcasted_iota(jnp.int32, tot.shape, 0)
        pref = tot
        for sh in (1, 2, 4):
            pref = pref + jnp.where(row >= sh, pltpu.roll(pref, sh, 0), 0.0)
        before = pref - tot
        end_all = _rows(pref, SUBLANES - 1)
        qe_b, ke_b = cat(qe), cat(ke)
        tile = lambda x: jnp.concatenate([x] * n_res, axis=0).astype(_BF16)
        for gsz in (1, 2, 4):
            if gsz == 1:
                qt, kt = qe_b, ke_b
            else:
                if gsz == 2:
                    g_start = jnp.where(row % 2 == 0, before, pltpu.roll(before, 1, 0))
                    g_end = jnp.where(row % 2 == 1, pref, pltpu.roll(pref, SUBLANES - 1, 0))
                else:
                    g_start = jnp.where(row < 4, _rows(before, 0), _rows(before, 4))
                    g_end = jnp.where(row < 4, _rows(pref, 3), _rows(pref, 7))
                qt = qe_b * tile(jnp.exp2(before - g_start))
                kt = ke_b * tile(jnp.exp2(g_end - pref))
            pending.append((step.level, every, _dot_nt(qt, kt)))
            step.level += 1
            yield
            fold()
        qc = qe_b * tile(jnp.exp2(before))
        kc = ke_b * tile(jnp.exp2(end_all - pref))
        o_st = _dot_nt(qc, get_st().astype(_BF16))
        decay = jnp.exp2(pref[SUBLANES - 1:SUBLANES, :])
        yield
    else:
        qc, kc = cat(qe), cat(ke)
        o_st = None
        decay = jnp.exp2(tot)
    while pending:
        fold()
    o = _dot(cat(s_rows), v_all)
    yield
    done(o if o_st is None else o + o_st, qc, kc, v_all, decay)


def _round_robin(starts):
    waiting = list(starts)
    live = []
    rnd = 0
    while waiting or live:
        if waiting and rnd % CHUNK_LAG == 0:
            live += waiting.pop(0)
        for g in list(live):
            try:
                next(g)
            except StopIteration:
                live.remove(g)
        rnd += 1


def _aligned(x, m):
    return x if isinstance(x, int) else pl.multiple_of(x, m)


def _loop(trips, body):
    if trips == 1:
        body(0, 0)
    else:
        lax.fori_loop(0, trips, body, 0)


def _head_out(o, ag, gate):
    ms = jnp.mean(o * o, axis=-1, keepdims=True)
    return o * lax.rsqrt(ms + EPS) * ag * gate


def _forget_lower_bound(logits_ref, layer):
    rows = [logits_ref[i:i + 1, :] for i in range(logits_ref.shape[0])]
    top = functools.reduce(jnp.maximum, rows)
    e = [jnp.exp(r - top) for r in rows]
    total = functools.reduce(jnp.add, e)
    return functools.reduce(jnp.add, [v / total for v in e[:layer + 1]])


def _project_a(h, win_ref, lb_ref, store, cg):
    p = _dot(h(), win_ref[:, cg * MXU_N:(cg + 1) * MXU_N])
    arr = cg // 2
    if arr == 0 or arr == 3:
        p = _silu(p)
    elif arr == 1:
        lb = lb_ref[:, (cg % 2) * MXU_N:(cg % 2 + 1) * MXU_N]
        p = lb + (1.0 - lb) * _sigmoid(p)
    store(2 * cg, p[:, :LANES])
    store(2 * cg + 1, p[:, LANES:])


def _project_b(h, win_ref, bglu_ref, half):
    c0 = half * MXU_N
    glu_a = _dot(h(), win_ref[:, GLU_OFF + c0:GLU_OFF + c0 + MXU_N]) + bglu_ref[:, c0:c0 + MXU_N]
    glu_b = (_dot(h(), win_ref[:, GLU_OFF + B_WIDTH + c0:GLU_OFF + B_WIDTH + c0 + MXU_N])
             + bglu_ref[:, B_WIDTH + c0:B_WIDTH + c0 + MXU_N])
    zb = _dot(h(), win_ref[:, ZB_OFF + c0:ZB_OFF + c0 + MXU_N])
    return glu_a * _sigmoid(glu_b), _silu(zb)


def _rmsnorm(x, g):
    return x * lax.rsqrt(jnp.mean(x * x, axis=-1, keepdims=True) + EPS) * g


def _group_b_out(cv, lng_ref, lnb_ref, gate_b):
    mu = jnp.mean(cv, axis=-1, keepdims=True)
    d = cv - mu
    var = jnp.mean(d * d, axis=-1, keepdims=True)
    ln = d * lax.rsqrt(var + EPS) * lng_ref[...] + lnb_ref[...]
    return _silu(ln) * gate_b


def _cast_weight(src_hbm, dst_ref, slab_refs, sems):
    n = dst_ref.shape[1] // LANES
    slots = len(slab_refs)

    def copy(j):
        return pltpu.make_async_copy(src_hbm.at[:, pl.ds(j * LANES, LANES)], slab_refs[j % slots], sems.at[j % slots])

    for j in range(min(slots, n)):
        copy(j).start()
    for j in range(n):
        copy(j).wait()
        dst_ref[:, j * LANES:(j + 1) * LANES] = slab_refs[j % slots][...].astype(_BF16)
        if j + slots < n:
            copy(j + slots).start()


def _conv_stages(ua_ref, ub_ref, wb_ref, convb_ref, lng_ref, lnb_ref, gb_ref, cv_ref, cat_ref, base, n):
    reps = lambda w, rows: jnp.concatenate([w] * (rows // (2 * SUBLANES)), axis=0)
    for ls in range(B_WIDTH // LANES):
        lanes = slice(ls * LANES, (ls + 1) * LANES)
        acc = jnp.broadcast_to(convb_ref[:, lanes], (n, LANES))
        for off in range(SUBLANES):
            taps = [j for j in range(CONV_WIDTH) if (FIRST + j) % SUBLANES == off]
            rows = n + (2 * SUBLANES if off else 0)
            part = None
            for j in taps:
                rel = FIRST + j - off
                if rel % (2 * SUBLANES) == 0:
                    win = ua_ref[pl.ds(_aligned(base + rel, 2 * SUBLANES), rows), lanes]
                else:
                    win = ub_ref[pl.ds(_aligned(base + rel + SUBLANES, 2 * SUBLANES), rows), lanes]
                term = reps(wb_ref[j, :, lanes], rows) * win
                part = term if part is None else part + term
            part = part.astype(_F32)
            acc = acc + (part[off:off + n, :] if off else part)
            if off == SUBLANES // 2 - 1:
                yield
        cv_ref[:, lanes] = acc
        yield
    rows = pl.ds(_aligned(base, n), n)
    o_b = _group_b_out(cv_ref[...], lng_ref, lnb_ref, gb_ref[rows, :])
    cat_ref[rows, A_WIDTH:] = o_b.astype(_BF16)


def _prompt_kernel(x_ref, normg_ref, win_hbm, lbl_ref, ag_ref, bglu_ref, convw_ref, convb_ref,
                   lng_ref, lnb_ref, wout_hbm, fg_ref, masks_ref,
                   y_ref, st_out_ref, cs_out_ref, win_out_hbm, wout_out_hbm,
                   pa_ref, uh_ref, ua_ref, ub_ref, gb_ref, cv_ref, oa_ref, cat_ref, st_ref,
                   win_ref, wout_ref, wout_f32_ref, lb_ref, wb_ref, sems, *, tb, layer):
    seq = pl.program_id(0)
    t = pl.program_id(1)
    nt = pl.num_programs(1)

    first = (seq == 0) & (t == 0)
    n_slabs = pa_ref.shape[0]
    handoff = [pltpu.make_async_copy(win_ref, win_out_hbm, sems.at[n_slabs]),
               pltpu.make_async_copy(wout_ref, wout_out_hbm, sems.at[n_slabs + 1])]
    wout_fetch = pltpu.make_async_copy(wout_hbm, wout_f32_ref, sems.at[n_slabs + 2])

    @pl.when(first)
    def _():
        _cast_weight(win_hbm, win_ref, [pa_ref.at[s, pl.ds(0, win_ref.shape[0])] for s in range(n_slabs)], sems)
        wout_fetch.start()
        handoff[0].start()
        lb_ref[...] = _forget_lower_bound(lbl_ref, layer)
        for j in range(CONV_WIDTH):
            wb_ref[j] = jnp.broadcast_to(convw_ref[j:j + 1, :], (2 * SUBLANES, B_WIDTH)).astype(_BF16)
        cs_out_ref[...] = jnp.zeros_like(cs_out_ref)

    @pl.when(t == 0)
    def _():
        st_ref[...] = jnp.zeros_like(st_ref)
        uh_ref[...] = jnp.zeros_like(uh_ref)
        ua_ref[...] = jnp.zeros_like(ua_ref)
        ub_ref[...] = jnp.zeros_like(ub_ref)

    def front_tile(r0):
        rows = pl.ds(r0, ROW_TILE)
        h = _rmsnorm(x_ref[rows, :], normg_ref[...]).astype(_BF16)
        for half in range(B_WIDTH // MXU_N):
            u, gb = _project_b(lambda: h, win_ref, bglu_ref, half)
            cols = slice(half * MXU_N, (half + 1) * MXU_N)
            here = pl.ds(_aligned(HIST + r0, HIST), ROW_TILE)
            prev = uh_ref[pl.ds(_aligned(HIST + r0 - SUBLANES, SUBLANES), SUBLANES), cols]
            uh_ref[here, cols] = u
            ua_ref[here, cols] = u.astype(_BF16)
            ub_ref[here, cols] = jnp.concatenate([prev, u[:ROW_TILE - SUBLANES, :]], axis=0).astype(_BF16)
            gb_ref[rows, cols] = gb

        def store_a(slab, val):
            pa_ref[slab, rows, :] = val
        for cg in range(2 * A_HEADS):
            _project_a(lambda: h, win_ref, lb_ref, store_a, cg)

    def front_body(p, carry):
        for n in range(UNROLL):
            front_tile(_aligned((UNROLL * p + n) * ROW_TILE, ROW_TILE))
        return carry

    _loop(tb // (UNROLL * ROW_TILE), front_body)
    ub_ref[HIST + tb:HIST + tb + 2 * SUBLANES, :] = jnp.concatenate(
        [uh_ref[HIST + tb - SUBLANES:HIST + tb, :], jnp.zeros((SUBLANES, B_WIDTH), _F32)], axis=0).astype(_BF16)

    def chunk(base, slot):
        def head(hd):
            def slabs(arr):
                return [pa_ref[arr * A_HEADS + hd, pl.ds(base + i, SUBLANES, stride=SUBLANES), :]
                        for i in range(SUBLANES)]

            def done(o, qc, kc, v_all, decay):
                st_ref[hd] = st_ref[hd] * decay + _dot_tn(v_all, kc)
                for i in range(SUBLANES):
                    oa_ref[hd, pl.ds(base + i, SUBLANES, stride=SUBLANES), :] = o[i * SUBLANES:(i + 1) * SUBLANES, :]
            return _hgrn_stages(slabs(0), slabs(1), slabs(2), masks_ref, lambda: st_ref[hd], True, done)
        conv = _conv_stages(ua_ref, ub_ref, wb_ref, convb_ref, lng_ref, lnb_ref, gb_ref, cv_ref.at[slot], cat_ref, base, CHUNK)
        return [head(hd) for hd in range(A_HEADS)] + [conv]

    def chunk_body(p, carry):
        _round_robin([chunk(_aligned((CHUNK_UNROLL * p + n) * CHUNK, CHUNK), n % CONV_BUFS) for n in range(CHUNK_UNROLL)])
        return carry

    _loop(tb // (CHUNK_UNROLL * CHUNK), chunk_body)

    @pl.when(t == nt - 1)
    def _():
        grp = pl.ds(pl.multiple_of((seq // SUBLANES) * SUBLANES, SUBLANES), SUBLANES)
        mine = lax.broadcasted_iota(jnp.int32, (SUBLANES, B_WIDTH), 0) == seq % SUBLANES
        for kk in range(CONV_WIDTH - 1):
            row = uh_ref[HIST + tb - (CONV_WIDTH - 1) + kk:HIST + tb - (CONV_WIDTH - 1) + kk + 1, :]
            cs_out_ref[kk, grp, :] = jnp.where(mine, jnp.broadcast_to(row, (SUBLANES, B_WIDTH)), cs_out_ref[kk, grp, :])
        for hd in range(A_HEADS):
            st_out_ref[hd] = st_ref[hd].T

    uh_ref[0:HIST, :] = uh_ref[tb:tb + HIST, :]
    ua_ref[0:HIST, :] = ua_ref[tb:tb + HIST, :]
    ub_ref[0:HIST, :] = ub_ref[tb:tb + HIST, :]

    @pl.when(first)
    def _():
        wout_fetch.wait()
        for r0 in range(0, wout_ref.shape[0], ROW_TILE):
            wout_ref[r0:r0 + ROW_TILE, :] = wout_f32_ref[r0:r0 + ROW_TILE, :].astype(_BF16)
        handoff[1].start()

    def back_body(p, carry):
        tiles = [pl.ds(_aligned((UNROLL * p + n) * ROW_TILE, ROW_TILE), ROW_TILE) for n in range(UNROLL)]
        for rows in tiles:
            for hd in range(A_HEADS):
                on = _head_out(oa_ref[hd, rows, :], ag_ref[...], pa_ref[3 * A_HEADS + hd, rows, :])
                cat_ref[rows, hd * HEAD_DIM:(hd + 1) * HEAD_DIM] = on.astype(_BF16)
        outs = [_dot(cat_ref[rows, :], wout_ref[...]) for rows in tiles]
        for rows, out in zip(tiles, outs):
            y_ref[rows, :] = _rmsnorm(x_ref[rows, :] + out, fg_ref[...])
        return carry

    _loop(tb // (UNROLL * ROW_TILE), back_body)

    @pl.when(first)
    def _():
        for copy in handoff:
            copy.wait()


def _weight_specs(weights, layer, in_hbm=()):
    specs = [_full(w.shape) for w in weights]
    specs[5] = pl.BlockSpec((None,) + weights[5].shape[1:], lambda *_: (layer, 0, 0), pipeline_mode=pl.Buffered(1))
    for i in in_hbm:
        specs[i] = pl.BlockSpec(memory_space=pl.ANY)
    return specs


def _full(shape):
    nd = len(shape)
    return pl.BlockSpec(shape, lambda *_: (0,) * nd, pipeline_mode=pl.Buffered(1))


def _prompt_call(x, weights, tb, layer):
    n, t, _ = x.shape
    assert t % tb == 0 and tb % (UNROLL * ROW_TILE) == 0 and tb % (CHUNK_UNROLL * CHUNK) == 0
    masks = jnp.asarray(_level_masks((1, 2, 4, 8, 16, 32), SUBLANES, SUBLANES))
    assert n % SUBLANES == 0
    kern = functools.partial(_prompt_kernel, tb=tb, layer=layer)
    w_in, w_out = weights[1], weights[9]
    assert w_in.dtype == _F32 and w_out.dtype == _F32 and w_in.shape[0] <= tb and w_in.shape[1] % LANES == 0
    in_hbm = pl.BlockSpec(memory_space=pl.ANY)
    w_specs = _weight_specs(weights, layer, in_hbm=(1, 9))
    return pl.pallas_call(
        kern,
        grid=(n, t // tb),
        in_specs=[pl.BlockSpec((None, tb, D_MODEL), lambda i, j: (i, j, 0))] + w_specs + [_full(masks.shape)],
        out_specs=[
            pl.BlockSpec((None, tb, D_MODEL), lambda i, j: (i, j, 0)),
            pl.BlockSpec((None, A_HEADS, HEAD_DIM, HEAD_DIM), lambda i, j: (i, 0, 0, 0)),
            _full((CONV_WIDTH - 1, n, B_WIDTH)),
            in_hbm,
            in_hbm,
        ],
        out_shape=[
            jax.ShapeDtypeStruct((n, t, D_MODEL), _F32),
            jax.ShapeDtypeStruct((n, A_HEADS, HEAD_DIM, HEAD_DIM), _F32),
            jax.ShapeDtypeStruct((CONV_WIDTH - 1, n, B_WIDTH), _F32),
            jax.ShapeDtypeStruct(w_in.shape, _BF16),
            jax.ShapeDtypeStruct(w_out.shape, _BF16),
        ],
        scratch_shapes=[
            pltpu.VMEM((4 * A_HEADS, tb, LANES), _F32),
            pltpu.VMEM((HIST + tb, B_WIDTH), _F32),
            pltpu.VMEM((HIST + tb + 2 * SUBLANES, B_WIDTH), _BF16),
            pltpu.VMEM((HIST + tb + 2 * SUBLANES, B_WIDTH), _BF16),
            pltpu.VMEM((tb, B_WIDTH), _F32),
            pltpu.VMEM((CONV_BUFS, CHUNK, B_WIDTH), _F32),
            pltpu.VMEM((A_HEADS, tb, LANES), _F32),
            pltpu.VMEM((tb, A_WIDTH + B_WIDTH), _BF16),
            pltpu.VMEM((A_HEADS, HEAD_DIM, HEAD_DIM), _F32),
            pltpu.VMEM(w_in.shape, _BF16),
            pltpu.VMEM(w_out.shape, _BF16),
            pltpu.VMEM(w_out.shape, _F32),
            pltpu.VMEM((1, A_WIDTH), _F32),
            pltpu.VMEM((CONV_WIDTH, 2 * SUBLANES, B_WIDTH), _BF16),
            pltpu.SemaphoreType.DMA((4 * A_HEADS + 3,)),
        ],
        compiler_params=pltpu.CompilerParams(
            dimension_semantics=("arbitrary", "arbitrary"),
            vmem_limit_bytes=VMEM_LIMIT_BYTES,
        ),
        name="hymba_prompt",
    )(x, *weights, masks)


def _decode_kernel(x_ref, s0_ref, c0_ref, normg_ref, win_ref, lbl_ref, ag_ref, bglu_ref, convw_ref, convb_ref,
                   lng_ref, lnb_ref, wout_ref, fg_ref, masks_ref,
                   y_ref, s_out_ref, cs_out_ref,
                   pa_ref, ub_ref, gb_ref, oa_ref, ob_ref, lb_ref, *, t_dec, layer):
    g = pl.program_id(0)
    n_seq = x_ref.shape[0]
    grp = SUBLANES * t_dec
    n_ls = B_WIDTH // LANES
    tiles = [(pl.ds(t * n_seq + s0, ROW_TILE), pl.ds(s0, ROW_TILE), t)
             for t in range(t_dec) for s0 in range(0, n_seq, ROW_TILE)]

    @pl.when(g == 0)
    def _():
        lb_ref[...] = _forget_lower_bound(lbl_ref, layer)
        for rows, seq_rows, t in tiles:
            h = _rmsnorm(x_ref[seq_rows, t, :], normg_ref[...]).astype(_BF16)
            for half in range(B_WIDTH // MXU_N):
                u, gb = _project_b(lambda: h, win_ref, bglu_ref, half)
                for c in range(MXU_N // LANES):
                    ls = half * (MXU_N // LANES) + c
                    ub_ref[ls, rows, :] = u[:, c * LANES:(c + 1) * LANES]
                    gb_ref[ls, rows, :] = gb[:, c * LANES:(c + 1) * LANES]

            def store_a(slab, val, rows=rows):
                pa_ref[slab, rows, :] = val
            for cg in range(2 * A_HEADS):
                _project_a(lambda: h, win_ref, lb_ref, store_a, cg)

    def group(sub):
        seqs = slice(sub * SUBLANES, (sub + 1) * SUBLANES)
        seq0 = pl.multiple_of((g * DEC_GROUPS + sub) * SUBLANES, SUBLANES)
        token_rows = lambda t: pl.ds(t * n_seq + seq0, SUBLANES)

        def token_slab(ref, idx, t):
            return ref[idx, token_rows(t), :]

        u_tok = [jnp.concatenate([token_slab(ub_ref, ls, t) for ls in range(n_ls)], axis=-1) for t in range(t_dec)]
        full = lambda kk: c0_ref[kk, seqs, :] if kk < CONV_WIDTH - 1 else u_tok[kk - (CONV_WIDTH - 1)]
        for kk in range(CONV_WIDTH - 1):
            cs_out_ref[kk, seqs, :] = full(kk + t_dec)
        for t in range(t_dec):
            cv = jnp.broadcast_to(convb_ref[...], (SUBLANES, B_WIDTH))
            for j in range(CONV_WIDTH):
                cv = cv + convw_ref[j:j + 1, :] * full(t + j)
            gate_b = jnp.concatenate([token_slab(gb_ref, ls, t) for ls in range(n_ls)], axis=-1)
            o_b = _group_b_out(cv, lng_ref, lnb_ref, gate_b)
            for ls in range(n_ls):
                ob_ref[ls, token_rows(t), :] = o_b[:, ls * LANES:(ls + 1) * LANES]

        row32 = lax.broadcasted_iota(jnp.int32, (grp, LANES), 0) % SUBLANES
        row64 = lax.broadcasted_iota(jnp.int32, (2 * grp, LANES), 0)
        for hd in range(A_HEADS):
            slabs = lambda arr: [token_slab(pa_ref, arr * A_HEADS + hd, t) for t in range(t_dec)]
            res = []
            for _ in _hgrn_stages(slabs(0), slabs(1), slabs(2), masks_ref, None, False, lambda *a: res.extend(a)):
                pass
            o, qc, kc, v_all, decay = res
            qf, kf, vf = qc.astype(_F32), kc.astype(_F32), v_all.astype(_F32)
            e_hi = decay.astype(_BF16).astype(_F32)
            e_mid = (decay - e_hi).astype(_BF16).astype(_F32)
            e_lo = (decay - e_hi - e_mid).astype(_BF16).astype(_F32)
            lhs = jnp.concatenate([kf, e_hi, e_mid, e_lo, jnp.zeros((SUBLANES, LANES), _F32)], axis=0).astype(_BF16)
            v_pad = jnp.concatenate([vf, jnp.zeros((grp, LANES), _F32)], axis=0)
            o_state = None
            for s in range(SUBLANES):
                s0 = s0_ref[sub * SUBLANES + s, hd]
                term = _dot(jnp.where(row32 == s, qf, 0.0).astype(_BF16), s0.astype(_BF16))
                o_state = term if o_state is None else o_state + term
                mine = row64 % SUBLANES == s
                rhs = jnp.concatenate([
                    jnp.where(mine & (row64 < grp), v_pad, 0.0),
                    jnp.where(mine & (row64 >= grp) & (row64 < grp + 3 * SUBLANES), 1.0, 0.0)], axis=-1).astype(_BF16)
                upd = _dot_tn(lhs, rhs)
                s_out_ref[sub * SUBLANES + s, hd] = s0 * upd[:, HEAD_DIM:] + upd[:, :HEAD_DIM]
            gate = jnp.concatenate([token_slab(pa_ref, 3 * A_HEADS + hd, t) for t in range(t_dec)], axis=0)
            on = _head_out(o + o_state, ag_ref[...], gate)
            for t in range(t_dec):
                oa_ref[hd, token_rows(t), :] = on[t * SUBLANES:(t + 1) * SUBLANES, :]

    for sub in range(DEC_GROUPS):
        group(sub)

    @pl.when(g == pl.num_programs(0) - 1)
    def _():
        for rows, seq_rows, t in tiles:
            cat = jnp.concatenate([oa_ref[hd, rows, :] for hd in range(A_HEADS)]
                                  + [ob_ref[ls, rows, :] for ls in range(n_ls)], axis=-1).astype(_BF16)
            out = _dot(cat, wout_ref[...])
            y_ref[seq_rows, t, :] = _rmsnorm(x_ref[seq_rows, t, :] + out, fg_ref[...])


def _decode_call(x, s0, c0, weights, layer):
    n, t_dec, _ = x.shape
    assert n % ROW_TILE == 0 and t_dec == 4
    n_tok = n * t_dec
    masks = jnp.asarray(_level_masks((1, 2), t_dec, SUBLANES, same_group=t_dec))
    kern = functools.partial(_decode_kernel, t_dec=t_dec, layer=layer)
    w_specs = _weight_specs(weights, layer)
    per_step = DEC_GROUPS * SUBLANES
    state_spec = pl.BlockSpec((per_step, A_HEADS, HEAD_DIM, HEAD_DIM), lambda i: (i, 0, 0, 0))
    conv_spec = pl.BlockSpec((CONV_WIDTH - 1, per_step, B_WIDTH), lambda i: (0, i, 0))
    y, s_new, c_new = pl.pallas_call(
        kern,
        grid=(n // per_step,),
        in_specs=[_full(x.shape), state_spec, conv_spec] + w_specs + [_full(masks.shape)],
        out_specs=[_full(x.shape), state_spec, conv_spec],
        out_shape=[
            jax.ShapeDtypeStruct(x.shape, _F32),
            jax.ShapeDtypeStruct((n, A_HEADS, HEAD_DIM, HEAD_DIM), _F32),
            jax.ShapeDtypeStruct((CONV_WIDTH - 1, n, B_WIDTH), _F32),
        ],
        scratch_shapes=[
            pltpu.VMEM((4 * A_HEADS, n_tok, LANES), _F32),
            pltpu.VMEM((B_WIDTH // LANES, n_tok, LANES), _F32),
            pltpu.VMEM((B_WIDTH // LANES, n_tok, LANES), _F32),
            pltpu.VMEM((A_HEADS, n_tok, LANES), _F32),
            pltpu.VMEM((B_WIDTH // LANES, n_tok, LANES), _F32),
            pltpu.VMEM((1, A_WIDTH), _F32),
        ],
        compiler_params=pltpu.CompilerParams(
            dimension_semantics=("arbitrary",),
            vmem_limit_bytes=VMEM_LIMIT_BYTES,
        ),
        name="hymba_decode",
    )(x, s0, c0, *weights, masks)
    return y, s_new, c_new


def _prepare_weights(norm_in_g, w_in, lb_logits, hgrn_norm_g, b_glu, conv_w, conv_b, ln_g, ln_b, w_out, final_norm_g, layer):
    row = lambda a: a.astype(_F32).reshape(1, -1)
    return (
        row(norm_in_g[layer]),
        w_in[layer].astype(_F32),
        lb_logits.astype(_F32),
        row(hgrn_norm_g[layer]),
        row(b_glu[layer]),
        conv_w.astype(_F32),
        row(conv_b[layer]),
        row(ln_g[layer]),
        row(ln_b[layer]),
        w_out[layer].astype(_F32),
        row(final_norm_g),
    )


def kernel(x_prompt, x_sample, state_hgrn, state_conv, norm_in_g, w_in, lb_logits, hgrn_norm_g, b_glu, conv_w,
           conv_b, ln_g, ln_b, w_out, final_norm_g):
    depth = w_in.shape[0]
    assert depth == 1, "single mixer layer: the final norm is fused into the layer kernel"
    weights = _prepare_weights(norm_in_g, w_in, lb_logits, hgrn_norm_g, b_glu, conv_w, conv_b, ln_g, ln_b, w_out,
                               final_norm_g, 0)
    y_p, s_p, c_p, w_in_b, w_out_b = _prompt_call(x_prompt, weights, tb=1024, layer=0)
    weights = weights[:1] + (w_in_b,) + weights[2:9] + (w_out_b,) + weights[10:]
    y_s, s_s, c_s = _decode_call(x_sample, state_hgrn[0], jnp.transpose(state_conv[0], (1, 0, 2)), weights, layer=0)
    return (y_p, y_s, s_p[None], jnp.transpose(c_p, (1, 0, 2))[None], s_s[None], jnp.transpose(c_s, (1, 0, 2))[None])
```

```python
import functools
import math

import numpy as np
import jax
import jax.numpy as jnp
from jax import lax
from jax.experimental import pallas as pl
from jax.experimental.pallas import tpu as pltpu

D_MODEL = 1024
A_HEADS = 4
HEAD_DIM = 128
A_WIDTH = A_HEADS * HEAD_DIM
B_WIDTH = 512
CONV_WIDTH = 31
EPS = 1e-6
IN_WIDTH = 4 * A_WIDTH + 3 * B_WIDTH
GLU_OFF = 4 * A_WIDTH
ZB_OFF = GLU_OFF + 2 * B_WIDTH

LANES = 128
SUBLANES = 8
MXU_N = 256
CHUNK = SUBLANES * SUBLANES
ROW_TILE = 128
UNROLL = 8
CHUNK_UNROLL = 2
CHUNK_LAG = 4
CONV_BUFS = 3
DEC_GROUPS = 2
STATE_PIECE = 4
HIST = 32
FIRST = HIST - (CONV_WIDTH - 1)
VMEM_LIMIT_BYTES = 56 * 1024 * 1024
LOG2E = math.log2(math.e)

_F32 = jnp.float32
_BF16 = jnp.bfloat16


def _sigmoid(x):
    return 1.0 / (1.0 + jnp.exp(-x))


def _silu(x):
    return x * _sigmoid(x)


def _dot(a, b):
    return jnp.dot(a, b, preferred_element_type=_F32)


def _dot_nt(a, b):
    return lax.dot_general(a, b, (((1,), (1,)), ((), ())), preferred_element_type=_F32)


def _dot_tn(a, b):
    return lax.dot_general(a, b, (((0,), (0,)), ((), ())), preferred_element_type=_F32)


def _level_masks(levels, n_res, n_rows, same_group=None):
    r = np.arange(n_res * n_rows)
    t = n_res * (r % n_rows) + r // n_rows
    tt, ss = t[:, None], t[None, :]
    masks = [tt == ss]
    for b in levels:
        masks.append(((tt // b) == (ss // b) + 1) & ((ss // b) % 2 == 0))
    masks = np.stack(masks)
    want = (ss <= tt) if same_group is None else ((ss <= tt) & (tt // same_group == ss // same_group))
    assert (masks.sum(0) == want).all()
    return masks.astype(np.float32)


def _rows(x, r):
    return jnp.broadcast_to(x[r:r + 1, :], x.shape)


def _hgrn_stages(q, f, v, masks_ref, get_st, coarse, done):
    n_res = len(q)
    every = range(n_res)
    k = [1.0 - fi for fi in f]
    lf = [jnp.log(fi) * LOG2E for fi in f]
    g_in = [lf[0]]
    for i in range(1, n_res):
        g_in.append(g_in[-1] + lf[i])
    tot = g_in[-1]
    zero = jnp.zeros_like(tot)

    def cat(parts):
        return jnp.concatenate(parts, axis=0).astype(_BF16)

    def slab_rows(x, i):
        return x[i * SUBLANES:(i + 1) * SUBLANES, :]

    v_all = cat(v)
    s_rows = [None] * n_res
    pending = []

    def fold():
        level, rows, s_l = pending.pop(0)
        for n, i in enumerate(rows):
            term = masks_ref[level, i * SUBLANES:(i + 1) * SUBLANES, :] * slab_rows(s_l, n)
            s_rows[i] = term if s_rows[i] is None else s_rows[i] + term

    def step(rows, qt, kt):
        pending.append((step.level, rows, _dot_nt(cat(qt), cat(kt))))
        step.level += 1
    step.level = 0

    step(every, q, k)
    yield
    b = 1
    while b < n_res:
        odd = [i for i in every if (i // b) % 2 == 1]
        qt, kt = [], []
        for i in every:
            bs = (i // b) * b
            if i in odd:
                qt.append(q[i] * jnp.exp2(g_in[i] - g_in[bs - 1]))
                kt.append(zero)
            else:
                be = bs + b - 1
                kt.append(k[i] * jnp.exp2(g_in[be] - g_in[i]) if i != be else k[i])
        step(odd, qt, kt)
        yield
        fold()
        b *= 2
    qe = [q[i] * jnp.exp2(g_in[i]) for i in every]
    ke = [k[i] * jnp.exp2(tot - g_in[i]) for i in range(n_res - 1)] + [k[-1]]
    if coarse:
        row = lax.broadcasted_iota(jnp.int32, tot.shape, 0)
        pref = tot
        for sh in (1, 2, 4):
            pref = pref + jnp.where(row >= sh, pltpu.roll(pref, sh, 0), 0.0)
        before = pref - tot
        end_all = _rows(pref, SUBLANES - 1)
        qe_b, ke_b = cat(qe), cat(ke)
        tile = lambda x: jnp.concatenate([x] * n_res, axis=0).astype(_BF16)
        for gsz in (1, 2, 4):
            if gsz == 1:
                qt, kt = qe_b, ke_b
            else:
                if gsz == 2:
                    g_start = jnp.where(row % 2 == 0, before, pltpu.roll(before, 1, 0))
                    g_end = jnp.where(row % 2 == 1, pref, pltpu.roll(pref, SUBLANES - 1, 0))
                else:
                    g_start = jnp.where(row < 4, _rows(before, 0), _rows(before, 4))
                    g_end = jnp.where(row < 4, _rows(pref, 3), _rows(pref, 7))
                qt = qe_b * tile(jnp.exp2(before - g_start))
                kt = ke_b * tile(jnp.exp2(g_end - pref))
            pending.append((step.level, every, _dot_nt(qt, kt)))
            step.level += 1
            yield
            fold()
        qc = qe_b * tile(jnp.exp2(before))
        kc = ke_b * tile(jnp.exp2(end_all - pref))
        o_st = _dot_nt(qc, get_st().astype(_BF16))
        decay = jnp.exp2(pref[SUBLANES - 1:SUBLANES, :])
        yield
    else:
        qc, kc = cat(qe), cat(ke)
        o_st = None
        decay = jnp.exp2(tot)
    while pending:
        fold()
    o = _dot(cat(s_rows), v_all)
    yield
    done(o if o_st is None else o + o_st, qc, kc, v_all, decay)


def _round_robin(starts):
    waiting = list(starts)
    live = []
    rnd = 0
    while waiting or live:
        if waiting and rnd % CHUNK_LAG == 0:
            live += waiting.pop(0)
        for g in list(live):
            try:
                next(g)
            except StopIteration:
                live.remove(g)
        rnd += 1


def _aligned(x, m):
    return x if isinstance(x, int) else pl.multiple_of(x, m)


def _loop(trips, body):
    if trips == 1:
        body(0, 0)
    else:
        lax.fori_loop(0, trips, body, 0)


def _head_out(o, ag, gate):
    ms = jnp.mean(o * o, axis=-1, keepdims=True)
    return o * lax.rsqrt(ms + EPS) * ag * gate


def _forget_lower_bound(logits_ref, layer):
    rows = [logits_ref[i:i + 1, :] for i in range(logits_ref.shape[0])]
    top = functools.reduce(jnp.maximum, rows)
    e = [jnp.exp(r - top) for r in rows]
    total = functools.reduce(jnp.add, e)
    return functools.reduce(jnp.add, [v / total for v in e[:layer + 1]])


def _project_a(h, win_ref, lb_ref, store, cg):
    p = _dot(h(), win_ref[:, cg * MXU_N:(cg + 1) * MXU_N])
    arr = cg // 2
    if arr == 0 or arr == 3:
        p = _silu(p)
    elif arr == 1:
        lb = lb_ref[:, (cg % 2) * MXU_N:(cg % 2 + 1) * MXU_N]
        p = lb + (1.0 - lb) * _sigmoid(p)
    store(2 * cg, p[:, :LANES])
    store(2 * cg + 1, p[:, LANES:])


def _project_b(h, win_ref, bglu_ref, half):
    c0 = half * MXU_N
    glu_a = _dot(h(), win_ref[:, GLU_OFF + c0:GLU_OFF + c0 + MXU_N]) + bglu_ref[:, c0:c0 + MXU_N]
    glu_b = (_dot(h(), win_ref[:, GLU_OFF + B_WIDTH + c0:GLU_OFF + B_WIDTH + c0 + MXU_N])
             + bglu_ref[:, B_WIDTH + c0:B_WIDTH + c0 + MXU_N])
    zb = _dot(h(), win_ref[:, ZB_OFF + c0:ZB_OFF + c0 + MXU_N])
    return glu_a * _sigmoid(glu_b), _silu(zb)


def _rmsnorm(x, g):
    return x * lax.rsqrt(jnp.mean(x * x, axis=-1, keepdims=True) + EPS) * g


def _group_b_out(cv, lng_ref, lnb_ref, gate_b):
    mu = jnp.mean(cv, axis=-1, keepdims=True)
    d = cv - mu
    var = jnp.mean(d * d, axis=-1, keepdims=True)
    ln = d * lax.rsqrt(var + EPS) * lng_ref[...] + lnb_ref[...]
    return _silu(ln) * gate_b


def _cast_weight(src_hbm, dst_ref, slab_refs, sems):
    n = dst_ref.shape[1] // LANES
    slots = len(slab_refs)

    def copy(j):
        return pltpu.make_async_copy(src_hbm.at[:, pl.ds(j * LANES, LANES)], slab_refs[j % slots], sems.at[j % slots])

    for j in range(min(slots, n)):
        copy(j).start()
    for j in range(n):
        copy(j).wait()
        dst_ref[:, j * LANES:(j + 1) * LANES] = slab_refs[j % slots][...].astype(_BF16)
        if j + slots < n:
            copy(j + slots).start()


def _conv_stages(ua_ref, ub_ref, wb_ref, convb_ref, lng_ref, lnb_ref, gb_ref, cv_ref, cat_ref, base, n):
    reps = lambda w, rows: jnp.concatenate([w] * (rows // (2 * SUBLANES)), axis=0)
    for ls in range(B_WIDTH // LANES):
        lanes = slice(ls * LANES, (ls + 1) * LANES)
        acc = jnp.broadcast_to(convb_ref[:, lanes], (n, LANES))
        for off in range(SUBLANES):
            taps = [j for j in range(CONV_WIDTH) if (FIRST + j) % SUBLANES == off]
            rows = n + (2 * SUBLANES if off else 0)
            part = None
            for j in taps:
                rel = FIRST + j - off
                if rel % (2 * SUBLANES) == 0:
                    win = ua_ref[pl.ds(_aligned(base + rel, 2 * SUBLANES), rows), lanes]
                else:
                    win = ub_ref[pl.ds(_aligned(base + rel + SUBLANES, 2 * SUBLANES), rows), lanes]
                term = reps(wb_ref[j, :, lanes], rows) * win
                part = term if part is None else part + term
            part = part.astype(_F32)
            acc = acc + (part[off:off + n, :] if off else part)
            if off == SUBLANES // 2 - 1:
                yield
        cv_ref[:, lanes] = acc
        yield
    rows = pl.ds(_aligned(base, n), n)
    o_b = _group_b_out(cv_ref[...], lng_ref, lnb_ref, gb_ref[rows, :])
    cat_ref[rows, A_WIDTH:] = o_b.astype(_BF16)


def _prompt_kernel(x_ref, normg_ref, win_hbm, lbl_ref, ag_ref, bglu_ref, convw_ref, convb_ref,
                   lng_ref, lnb_ref, wout_hbm, fg_ref, masks_ref,
                   y_ref, st_out_ref, cs_out_ref, win_out_hbm, wout_out_hbm,
                   pa_ref, uh_ref, ua_ref, ub_ref, gb_ref, cv_ref, oa_ref, cat_ref, st_ref,
                   win_ref, wout_ref, wout_f32_ref, lb_ref, wb_ref, sems, *, tb, layer):
    seq = pl.program_id(0)
    t = pl.program_id(1)
    nt = pl.num_programs(1)

    first = (seq == 0) & (t == 0)
    n_slabs = pa_ref.shape[0]
    handoff = [pltpu.make_async_copy(win_ref, win_out_hbm, sems.at[n_slabs]),
               pltpu.make_async_copy(wout_ref, wout_out_hbm, sems.at[n_slabs + 1])]
    wout_fetch = pltpu.make_async_copy(wout_hbm, wout_f32_ref, sems.at[n_slabs + 2])

    @pl.when(first)
    def _():
        _cast_weight(win_hbm, win_ref, [pa_ref.at[s, pl.ds(0, win_ref.shape[0])] for s in range(n_slabs)], sems)
        wout_fetch.start()
        handoff[0].start()
        lb_ref[...] = _forget_lower_bound(lbl_ref, layer)
        for j in range(CONV_WIDTH):
            wb_ref[j] = jnp.broadcast_to(convw_ref[j:j + 1, :], (2 * SUBLANES, B_WIDTH)).astype(_BF16)
        cs_out_ref[...] = jnp.zeros_like(cs_out_ref)

    @pl.when(t == 0)
    def _():
        st_ref[...] = jnp.zeros_like(st_ref)
        uh_ref[...] = jnp.zeros_like(uh_ref)
        ua_ref[...] = jnp.zeros_like(ua_ref)
        ub_ref[...] = jnp.zeros_like(ub_ref)

    def front_tile(r0):
        rows = pl.ds(r0, ROW_TILE)
        h = _rmsnorm(x_ref[rows, :], normg_ref[...]).astype(_BF16)
        for half in range(B_WIDTH // MXU_N):
            u, gb = _project_b(lambda: h, win_ref, bglu_ref, half)
            cols = slice(half * MXU_N, (half + 1) * MXU_N)
            here = pl.ds(_aligned(HIST + r0, HIST), ROW_TILE)
            prev = uh_ref[pl.ds(_aligned(HIST + r0 - SUBLANES, SUBLANES), SUBLANES), cols]
            uh_ref[here, cols] = u
            ua_ref[here, cols] = u.astype(_BF16)
            ub_ref[here, cols] = jnp.concatenate([prev, u[:ROW_TILE - SUBLANES, :]], axis=0).astype(_BF16)
            gb_ref[rows, cols] = gb

        def store_a(slab, val):
            pa_ref[slab, rows, :] = val
        for cg in range(2 * A_HEADS):
            _project_a(lambda: h, win_ref, lb_ref, store_a, cg)

    def front_body(p, carry):
        for n in range(UNROLL):
            front_tile(_aligned((UNROLL * p + n) * ROW_TILE, ROW_TILE))
        return carry

    _loop(tb // (UNROLL * ROW_TILE), front_body)
    ub_ref[HIST + tb:HIST + tb + 2 * SUBLANES, :] = jnp.concatenate(
        [uh_ref[HIST + tb - SUBLANES:HIST + tb, :], jnp.zeros((SUBLANES, B_WIDTH), _F32)], axis=0).astype(_BF16)

    def chunk(base, slot):
        def head(hd):
            def slabs(arr):
                return [pa_ref[arr * A_HEADS + hd, pl.ds(base + i, SUBLANES, stride=SUBLANES), :]
                        for i in range(SUBLANES)]

            def done(o, qc, kc, v_all, decay):
                st_ref[hd] = st_ref[hd] * decay + _dot_tn(v_all, kc)
                for i in range(SUBLANES):
                    oa_ref[hd, pl.ds(base + i, SUBLANES, stride=SUBLANES), :] = o[i * SUBLANES:(i + 1) * SUBLANES, :]
            return _hgrn_stages(slabs(0), slabs(1), slabs(2), masks_ref, lambda: st_ref[hd], True, done)
        conv = _conv_stages(ua_ref, ub_ref, wb_ref, convb_ref, lng_ref, lnb_ref, gb_ref, cv_ref.at[slot], cat_ref, base, CHUNK)
        return [head(hd) for hd in range(A_HEADS)] + [conv]

    def chunk_body(p, carry):
        _round_robin([chunk(_aligned((CHUNK_UNROLL * p + n) * CHUNK, CHUNK), n % CONV_BUFS) for n in range(CHUNK_UNROLL)])
        return carry

    _loop(tb // (CHUNK_UNROLL * CHUNK), chunk_body)

    @pl.when(t == nt - 1)
    def _():
        grp = pl.ds(pl.multiple_of((seq // SUBLANES) * SUBLANES, SUBLANES), SUBLANES)
        mine = lax.broadcasted_iota(jnp.int32, (SUBLANES, B_WIDTH), 0) == seq % SUBLANES
        for kk in range(CONV_WIDTH - 1):
            row = uh_ref[HIST + tb - (CONV_WIDTH - 1) + kk:HIST + tb - (CONV_WIDTH - 1) + kk + 1, :]
            cs_out_ref[kk, grp, :] = jnp.where(mine, jnp.broadcast_to(row, (SUBLANES, B_WIDTH)), cs_out_ref[kk, grp, :])
        for hd in range(A_HEADS):
            st_out_ref[hd] = st_ref[hd].T

    uh_ref[0:HIST, :] = uh_ref[tb:tb + HIST, :]
    ua_ref[0:HIST, :] = ua_ref[tb:tb + HIST, :]
    ub_ref[0:HIST, :] = ub_ref[tb:tb + HIST, :]

    @pl.when(first)
    def _():
        wout_fetch.wait()
        for r0 in range(0, wout_ref.shape[0], ROW_TILE):
            wout_ref[r0:r0 + ROW_TILE, :] = wout_f32_ref[r0:r0 + ROW_TILE, :].astype(_BF16)
        handoff[1].start()

    def back_body(p, carry):
        tiles = [pl.ds(_aligned((UNROLL * p + n) * ROW_TILE, ROW_TILE), ROW_TILE) for n in range(UNROLL)]
        for rows in tiles:
            for hd in range(A_HEADS):
                on = _head_out(oa_ref[hd, rows, :], ag_ref[...], pa_ref[3 * A_HEADS + hd, rows, :])
                cat_ref[rows, hd * HEAD_DIM:(hd + 1) * HEAD_DIM] = on.astype(_BF16)
        outs = [_dot(cat_ref[rows, :], wout_ref[...]) for rows in tiles]
        for rows, out in zip(tiles, outs):
            y_ref[rows, :] = _rmsnorm(x_ref[rows, :] + out, fg_ref[...])
        return carry

    _loop(tb // (UNROLL * ROW_TILE), back_body)

    @pl.when(first)
    def _():
        for copy in handoff:
            copy.wait()


def _weight_specs(weights, layer, in_hbm=()):
    specs = [_full(w.shape) for w in weights]
    specs[5] = pl.BlockSpec((None,) + weights[5].shape[1:], lambda *_: (layer, 0, 0), pipeline_mode=pl.Buffered(1))
    for i in in_hbm:
        specs[i] = pl.BlockSpec(memory_space=pl.ANY)
    return specs


def _full(shape):
    nd = len(shape)
    return pl.BlockSpec(shape, lambda *_: (0,) * nd, pipeline_mode=pl.Buffered(1))


def _prompt_call(x, weights, tb, layer):
    n, t, _ = x.shape
    assert t % tb == 0 and tb % (UNROLL * ROW_TILE) == 0 and tb % (CHUNK_UNROLL * CHUNK) == 0
    masks = jnp.asarray(_level_masks((1, 2, 4, 8, 16, 32), SUBLANES, SUBLANES))
    assert n % SUBLANES == 0
    kern = functools.partial(_prompt_kernel, tb=tb, layer=layer)
    w_in, w_out = weights[1], weights[9]
    assert w_in.dtype == _F32 and w_out.dtype == _F32 and w_in.shape[0] <= tb and w_in.shape[1] % LANES == 0
    in_hbm = pl.BlockSpec(memory_space=pl.ANY)
    w_specs = _weight_specs(weights, layer, in_hbm=(1, 9))
    return pl.pallas_call(
        kern,
        grid=(n, t // tb),
        in_specs=[pl.BlockSpec((None, tb, D_MODEL), lambda i, j: (i, j, 0))] + w_specs + [_full(masks.shape)],
        out_specs=[
            pl.BlockSpec((None, tb, D_MODEL), lambda i, j: (i, j, 0)),
            pl.BlockSpec((None, A_HEADS, HEAD_DIM, HEAD_DIM), lambda i, j: (i, 0, 0, 0)),
            _full((CONV_WIDTH - 1, n, B_WIDTH)),
            in_hbm,
            in_hbm,
        ],
        out_shape=[
            jax.ShapeDtypeStruct((n, t, D_MODEL), _F32),
            jax.ShapeDtypeStruct((n, A_HEADS, HEAD_DIM, HEAD_DIM), _F32),
            jax.ShapeDtypeStruct((CONV_WIDTH - 1, n, B_WIDTH), _F32),
            jax.ShapeDtypeStruct(w_in.shape, _BF16),
            jax.ShapeDtypeStruct(w_out.shape, _BF16),
        ],
        scratch_shapes=[
            pltpu.VMEM((4 * A_HEADS, tb, LANES), _F32),
            pltpu.VMEM((HIST + tb, B_WIDTH), _F32),
            pltpu.VMEM((HIST + tb + 2 * SUBLANES, B_WIDTH), _BF16),
            pltpu.VMEM((HIST + tb + 2 * SUBLANES, B_WIDTH), _BF16),
            pltpu.VMEM((tb, B_WIDTH), _F32),
            pltpu.VMEM((CONV_BUFS, CHUNK, B_WIDTH), _F32),
            pltpu.VMEM((A_HEADS, tb, LANES), _F32),
            pltpu.VMEM((tb, A_WIDTH + B_WIDTH), _BF16),
            pltpu.VMEM((A_HEADS, HEAD_DIM, HEAD_DIM), _F32),
            pltpu.VMEM(w_in.shape, _BF16),
            pltpu.VMEM(w_out.shape, _BF16),
            pltpu.VMEM(w_out.shape, _F32),
            pltpu.VMEM((1, A_WIDTH), _F32),
            pltpu.VMEM((CONV_WIDTH, 2 * SUBLANES, B_WIDTH), _BF16),
            pltpu.SemaphoreType.DMA((4 * A_HEADS + 3,)),
        ],
        compiler_params=pltpu.CompilerParams(
            dimension_semantics=("arbitrary", "arbitrary"),
            vmem_limit_bytes=VMEM_LIMIT_BYTES,
        ),
        name="hymba_prompt",
    )(x, *weights, masks)


def _decode_kernel(x_ref, *refs, t_dec, layer):
    n_pieces = DEC_GROUPS * SUBLANES // STATE_PIECE
    s0_refs, refs = refs[:n_pieces], refs[n_pieces:]
    (c0_ref, normg_ref, win_ref, lbl_ref, ag_ref, bglu_ref, convw_ref, convb_ref, lng_ref, lnb_ref, wout_ref, fg_ref,
     masks_ref, y_ref, s_out_hbm, cs_out_ref, pa_ref, ub_ref, gb_ref, oa_ref, ob_ref, lb_ref, s_new_ref, sems) = refs
    g = pl.program_id(0)
    n_steps = pl.num_programs(0)
    slot = g % 2

    def writeback(step, slot, piece):
        return pltpu.make_async_copy(
            s_new_ref.at[slot, pl.ds(piece * STATE_PIECE, STATE_PIECE)],
            s_out_hbm.at[pl.ds((step * n_pieces + piece) * STATE_PIECE, STATE_PIECE)], sems.at[slot, piece])

    @pl.when(g >= 2)
    def _():
        for piece in range(n_pieces):
            writeback(g - 2, slot, piece).wait()
    n_seq = x_ref.shape[0]
    grp = SUBLANES * t_dec
    n_ls = B_WIDTH // LANES
    tiles = [(pl.ds(t * n_seq + s0, ROW_TILE), pl.ds(s0, ROW_TILE), t)
             for t in range(t_dec) for s0 in range(0, n_seq, ROW_TILE)]

    @pl.when(g == 0)
    def _():
        lb_ref[...] = _forget_lower_bound(lbl_ref, layer)
        for rows, seq_rows, t in tiles:
            h = _rmsnorm(x_ref[seq_rows, t, :], normg_ref[...]).astype(_BF16)
            for half in range(B_WIDTH // MXU_N):
                u, gb = _project_b(lambda: h, win_ref, bglu_ref, half)
                for c in range(MXU_N // LANES):
                    ls = half * (MXU_N // LANES) + c
                    ub_ref[ls, rows, :] = u[:, c * LANES:(c + 1) * LANES]
                    gb_ref[ls, rows, :] = gb[:, c * LANES:(c + 1) * LANES]

            def store_a(slab, val, rows=rows):
                pa_ref[slab, rows, :] = val
            for cg in range(2 * A_HEADS):
                _project_a(lambda: h, win_ref, lb_ref, store_a, cg)

    def group(sub):
        seqs = slice(sub * SUBLANES, (sub + 1) * SUBLANES)
        seq0 = pl.multiple_of((g * DEC_GROUPS + sub) * SUBLANES, SUBLANES)
        token_rows = lambda t: pl.ds(t * n_seq + seq0, SUBLANES)

        def token_slab(ref, idx, t):
            return ref[idx, token_rows(t), :]

        u_tok = [jnp.concatenate([token_slab(ub_ref, ls, t) for ls in range(n_ls)], axis=-1) for t in range(t_dec)]
        full = lambda kk: c0_ref[kk, seqs, :] if kk < CONV_WIDTH - 1 else u_tok[kk - (CONV_WIDTH - 1)]
        for kk in range(CONV_WIDTH - 1):
            cs_out_ref[kk, seqs, :] = full(kk + t_dec)
        for t in range(t_dec):
            cv = jnp.broadcast_to(convb_ref[...], (SUBLANES, B_WIDTH))
            for j in range(CONV_WIDTH):
                cv = cv + convw_ref[j:j + 1, :] * full(t + j)
            gate_b = jnp.concatenate([token_slab(gb_ref, ls, t) for ls in range(n_ls)], axis=-1)
            o_b = _group_b_out(cv, lng_ref, lnb_ref, gate_b)
            for ls in range(n_ls):
                ob_ref[ls, token_rows(t), :] = o_b[:, ls * LANES:(ls + 1) * LANES]

        row32 = lax.broadcasted_iota(jnp.int32, (grp, LANES), 0) % SUBLANES
        row64 = lax.broadcasted_iota(jnp.int32, (2 * grp, LANES), 0)
        for hd in range(A_HEADS):
            slabs = lambda arr: [token_slab(pa_ref, arr * A_HEADS + hd, t) for t in range(t_dec)]
            res = []
            for _ in _hgrn_stages(slabs(0), slabs(1), slabs(2), masks_ref, None, False, lambda *a: res.extend(a)):
                pass
            o, qc, kc, v_all, decay = res
            qf, kf, vf = qc.astype(_F32), kc.astype(_F32), v_all.astype(_F32)
            e_hi = decay.astype(_BF16).astype(_F32)
            e_mid = (decay - e_hi).astype(_BF16).astype(_F32)
            e_lo = (decay - e_hi - e_mid).astype(_BF16).astype(_F32)
            lhs = jnp.concatenate([kf, e_hi, e_mid, e_lo, jnp.zeros((SUBLANES, LANES), _F32)], axis=0).astype(_BF16)
            v_pad = jnp.concatenate([vf, jnp.zeros((grp, LANES), _F32)], axis=0)
            o_state = None
            for s in range(SUBLANES):
                s0 = s0_refs[(sub * SUBLANES + s) // STATE_PIECE][(sub * SUBLANES + s) % STATE_PIECE, hd]
                term = _dot(jnp.where(row32 == s, qf, 0.0).astype(_BF16), s0.astype(_BF16))
                o_state = term if o_state is None else o_state + term
                mine = row64 % SUBLANES == s
                rhs = jnp.concatenate([
                    jnp.where(mine & (row64 < grp), v_pad, 0.0),
                    jnp.where(mine & (row64 >= grp) & (row64 < grp + 3 * SUBLANES), 1.0, 0.0)], axis=-1).astype(_BF16)
                upd = _dot_tn(lhs, rhs)
                s_new_ref[slot, sub * SUBLANES + s, hd] = s0 * upd[:, HEAD_DIM:] + upd[:, :HEAD_DIM]
            gate = jnp.concatenate([token_slab(pa_ref, 3 * A_HEADS + hd, t) for t in range(t_dec)], axis=0)
            on = _head_out(o + o_state, ag_ref[...], gate)
            for t in range(t_dec):
                oa_ref[hd, token_rows(t), :] = on[t * SUBLANES:(t + 1) * SUBLANES, :]

    for sub in range(DEC_GROUPS):
        group(sub)
        for piece in range(sub * SUBLANES // STATE_PIECE, (sub + 1) * SUBLANES // STATE_PIECE):
            writeback(g, slot, piece).start()

    @pl.when(g == n_steps - 1)
    def _():
        for piece in range(n_pieces):
            writeback(g, slot, piece).wait()

    @pl.when((g == n_steps - 1) & (g >= 1))
    def _():
        for piece in range(n_pieces):
            writeback(g - 1, 1 - slot, piece).wait()

    @pl.when(g == pl.num_programs(0) - 1)
    def _():
        for rows, seq_rows, t in tiles:
            cat = jnp.concatenate([oa_ref[hd, rows, :] for hd in range(A_HEADS)]
                                  + [ob_ref[ls, rows, :] for ls in range(n_ls)], axis=-1).astype(_BF16)
            out = _dot(cat, wout_ref[...])
            y_ref[seq_rows, t, :] = _rmsnorm(x_ref[seq_rows, t, :] + out, fg_ref[...])


def _decode_call(x, s0, c0, weights, layer):
    n, t_dec, _ = x.shape
    assert n % ROW_TILE == 0 and t_dec == 4
    n_tok = n * t_dec
    masks = jnp.asarray(_level_masks((1, 2), t_dec, SUBLANES, same_group=t_dec))
    kern = functools.partial(_decode_kernel, t_dec=t_dec, layer=layer)
    w_specs = _weight_specs(weights, layer)
    per_step = DEC_GROUPS * SUBLANES
    assert SUBLANES % STATE_PIECE == 0
    n_pieces = per_step // STATE_PIECE
    state_specs = [pl.BlockSpec((STATE_PIECE, A_HEADS, HEAD_DIM, HEAD_DIM), lambda i, k=k: (i * n_pieces + k, 0, 0, 0))
                   for k in range(n_pieces)]
    conv_spec = pl.BlockSpec((CONV_WIDTH - 1, per_step, B_WIDTH), lambda i: (0, i, 0))
    y, s_new, c_new = pl.pallas_call(
        kern,
        grid=(n // per_step,),
        in_specs=[_full(x.shape)] + state_specs + [conv_spec] + w_specs + [_full(masks.shape)],
        out_specs=[_full(x.shape), pl.BlockSpec(memory_space=pl.ANY), conv_spec],
        out_shape=[
            jax.ShapeDtypeStruct(x.shape, _F32),
            jax.ShapeDtypeStruct((n, A_HEADS, HEAD_DIM, HEAD_DIM), _F32),
            jax.ShapeDtypeStruct((CONV_WIDTH - 1, n, B_WIDTH), _F32),
        ],
        scratch_shapes=[
            pltpu.VMEM((4 * A_HEADS, n_tok, LANES), _F32),
            pltpu.VMEM((B_WIDTH // LANES, n_tok, LANES), _F32),
            pltpu.VMEM((B_WIDTH // LANES, n_tok, LANES), _F32),
            pltpu.VMEM((A_HEADS, n_tok, LANES), _F32),
            pltpu.VMEM((B_WIDTH // LANES, n_tok, LANES), _F32),
            pltpu.VMEM((1, A_WIDTH), _F32),
            pltpu.VMEM((2, per_step, A_HEADS, HEAD_DIM, HEAD_DIM), _F32),
            pltpu.SemaphoreType.DMA((2, n_pieces)),
        ],
        compiler_params=pltpu.CompilerParams(
            dimension_semantics=("arbitrary",),
            vmem_limit_bytes=VMEM_LIMIT_BYTES,
        ),
        name="hymba_decode",
    )(x, *([s0] * n_pieces), c0, *weights, masks)
    return y, s_new, c_new


def _prepare_weights(norm_in_g, w_in, lb_logits, hgrn_norm_g, b_glu, conv_w, conv_b, ln_g, ln_b, w_out, final_norm_g, layer):
    row = lambda a: a.astype(_F32).reshape(1, -1)
    return (
        row(norm_in_g[layer]),
        w_in[layer].astype(_F32),
        lb_logits.astype(_F32),
        row(hgrn_norm_g[layer]),
        row(b_glu[layer]),
        conv_w.astype(_F32),
        row(conv_b[layer]),
        row(ln_g[layer]),
        row(ln_b[layer]),
        w_out[layer].astype(_F32),
        row(final_norm_g),
    )


def kernel(x_prompt, x_sample, state_hgrn, state_conv, norm_in_g, w_in, lb_logits, hgrn_norm_g, b_glu, conv_w,
           conv_b, ln_g, ln_b, w_out, final_norm_g):
    depth = w_in.shape[0]
    assert depth == 1, "single mixer layer: the final norm is fused into the layer kernel"
    weights = _prepare_weights(norm_in_g, w_in, lb_logits, hgrn_norm_g, b_glu, conv_w, conv_b, ln_g, ln_b, w_out,
                               final_norm_g, 0)
    y_p, s_p, c_p, w_in_b, w_out_b = _prompt_call(x_prompt, weights, tb=1024, layer=0)
    weights = weights[:1] + (w_in_b,) + weights[2:9] + (w_out_b,) + weights[10:]
    y_s, s_s, c_s = _decode_call(x_sample, state_hgrn[0], jnp.transpose(state_conv[0], (1, 0, 2)), weights, layer=0)
    return (y_p, y_s, s_p[None], jnp.transpose(c_p, (1, 0, 2))[None], s_s[None], jnp.transpose(c_s, (1, 0, 2))[None])
```

```python
import functools
import math

import numpy as np
import jax
import jax.numpy as jnp
from jax import lax
from jax.experimental import pallas as pl
from jax.experimental.pallas import tpu as pltpu

D_MODEL = 1024
A_HEADS = 4
HEAD_DIM = 128
A_WIDTH = A_HEADS * HEAD_DIM
B_WIDTH = 512
CONV_WIDTH = 31
EPS = 1e-6
IN_WIDTH = 4 * A_WIDTH + 3 * B_WIDTH
GLU_OFF = 4 * A_WIDTH
ZB_OFF = GLU_OFF + 2 * B_WIDTH

LANES = 128
SUBLANES = 8
MXU_N = 256
CHUNK = SUBLANES * SUBLANES
ROW_TILE = 128
UNROLL = 8
CHUNK_UNROLL = 2
CHUNK_LAG = 4
CONV_BUFS = 3
DEC_GROUPS = 2
HIST = 32
FIRST = HIST - (CONV_WIDTH - 1)
VMEM_LIMIT_BYTES = 56 * 1024 * 1024
LOG2E = math.log2(math.e)

_F32 = jnp.float32
_BF16 = jnp.bfloat16


def _sigmoid(x):
    return 1.0 / (1.0 + jnp.exp(-x))


def _silu(x):
    return x * _sigmoid(x)


def _dot(a, b):
    return jnp.dot(a, b, preferred_element_type=_F32)


def _dot_nt(a, b):
    return lax.dot_general(a, b, (((1,), (1,)), ((), ())), preferred_element_type=_F32)


def _dot_tn(a, b):
    return lax.dot_general(a, b, (((0,), (0,)), ((), ())), preferred_element_type=_F32)


def _level_masks(levels, n_res, n_rows, same_group=None):
    r = np.arange(n_res * n_rows)
    t = n_res * (r % n_rows) + r // n_rows
    tt, ss = t[:, None], t[None, :]
    masks = [tt == ss]
    for b in levels:
        masks.append(((tt // b) == (ss // b) + 1) & ((ss // b) % 2 == 0))
    masks = np.stack(masks)
    want = (ss <= tt) if same_group is None else ((ss <= tt) & (tt // same_group == ss // same_group))
    assert (masks.sum(0) == want).all()
    return masks.astype(np.float32)


def _rows(x, r):
    return jnp.broadcast_to(x[r:r + 1, :], x.shape)


def _hgrn_stages(q, f, v, masks_ref, get_st, coarse, done):
    n_res = len(q)
    every = range(n_res)
    k = [1.0 - fi for fi in f]
    lf = [jnp.log(fi) * LOG2E for fi in f]
    g_in = [lf[0]]
    for i in range(1, n_res):
        g_in.append(g_in[-1] + lf[i])
    tot = g_in[-1]
    zero = jnp.zeros_like(tot)

    def cat(parts):
        return jnp.concatenate(parts, axis=0).astype(_BF16)

    def slab_rows(x, i):
        return x[i * SUBLANES:(i + 1) * SUBLANES, :]

    v_all = cat(v)
    s_rows = [None] * n_res
    pending = []

    def fold():
        level, rows, s_l = pending.pop(0)
        for n, i in enumerate(rows):
            term = masks_ref[level, i * SUBLANES:(i + 1) * SUBLANES, :] * slab_rows(s_l, n)
            s_rows[i] = term if s_rows[i] is None else s_rows[i] + term

    def step(rows, qt, kt):
        pending.append((step.level, rows, _dot_nt(cat(qt), cat(kt))))
        step.level += 1
    step.level = 0

    step(every, q, k)
    yield
    b = 1
    while b < n_res:
        odd = [i for i in every if (i // b) % 2 == 1]
        qt, kt = [], []
        for i in every:
            bs = (i // b) * b
            if i in odd:
                qt.append(q[i] * jnp.exp2(g_in[i] - g_in[bs - 1]))
                kt.append(zero)
            else:
                be = bs + b - 1
                kt.append(k[i] * jnp.exp2(g_in[be] - g_in[i]) if i != be else k[i])
        step(odd, qt, kt)
        yield
        fold()
        b *= 2
    qe = [q[i] * jnp.exp2(g_in[i]) for i in every]
    ke = [k[i] * jnp.exp2(tot - g_in[i]) for i in range(n_res - 1)] + [k[-1]]
    if coarse:
        row = lax.broadcasted_iota(jnp.int32, tot.shape, 0)
        pref = tot
        for sh in (1, 2, 4):
            pref = pref + jnp.where(row >= sh, pltpu.roll(pref, sh, 0), 0.0)
        before = pref - tot
        end_all = _rows(pref, SUBLANES - 1)
        qe_b, ke_b = cat(qe), cat(ke)
        tile = lambda x: jnp.concatenate([x] * n_res, axis=0).astype(_BF16)
        for gsz in (1, 2, 4):
            if gsz == 1:
                qt, kt = qe_b, ke_b
            else:
                if gsz == 2:
                    g_start = jnp.where(row % 2 == 0, before, pltpu.roll(before, 1, 0))
                    g_end = jnp.where(row % 2 == 1, pref, pltpu.roll(pref, SUBLANES - 1, 0))
                else:
                    g_start = jnp.where(row < 4, _rows(before, 0), _rows(before, 4))
                    g_end = jnp.where(row < 4, _rows(pref, 3), _rows(pref, 7))
                qt = qe_b * tile(jnp.exp2(before - g_start))
                kt = ke_b * tile(jnp.exp2(g_end - pref))
            pending.append((step.level, every, _dot_nt(qt, kt)))
            step.level += 1
            yield
            fold()
        qc = qe_b * tile(jnp.exp2(before))
        kc = ke_b * tile(jnp.exp2(end_all - pref))
        o_st = _dot_nt(qc, get_st().astype(_BF16))
        decay = jnp.exp2(pref[SUBLANES - 1:SUBLANES, :])
        yield
    else:
        qc, kc = cat(qe), cat(ke)
        o_st = None
        decay = jnp.exp2(tot)
    while pending:
        fold()
    o = _dot(cat(s_rows), v_all)
    yield
    done(o if o_st is None else o + o_st, qc, kc, v_all, decay)


def _round_robin(starts):
    waiting = list(starts)
    live = []
    rnd = 0
    while waiting or live:
        if waiting and rnd % CHUNK_LAG == 0:
            live += waiting.pop(0)
        for g in list(live):
            try:
                next(g)
            except StopIteration:
                live.remove(g)
        rnd += 1


def _aligned(x, m):
    return x if isinstance(x, int) else pl.multiple_of(x, m)


def _loop(trips, body):
    if trips == 1:
        body(0, 0)
    else:
        lax.fori_loop(0, trips, body, 0)


def _head_out(o, ag, gate):
    ms = jnp.mean(o * o, axis=-1, keepdims=True)
    return o * lax.rsqrt(ms + EPS) * ag * gate


def _forget_lower_bound(logits_ref, layer):
    rows = [logits_ref[i:i + 1, :] for i in range(logits_ref.shape[0])]
    top = functools.reduce(jnp.maximum, rows)
    e = [jnp.exp(r - top) for r in rows]
    total = functools.reduce(jnp.add, e)
    return functools.reduce(jnp.add, [v / total for v in e[:layer + 1]])


def _project_a(h, win_ref, lb_ref, store, cg):
    p = _dot(h(), win_ref[:, cg * MXU_N:(cg + 1) * MXU_N])
    arr = cg // 2
    if arr == 0 or arr == 3:
        p = _silu(p)
    elif arr == 1:
        lb = lb_ref[:, (cg % 2) * MXU_N:(cg % 2 + 1) * MXU_N]
        p = lb + (1.0 - lb) * _sigmoid(p)
    store(2 * cg, p[:, :LANES])
    store(2 * cg + 1, p[:, LANES:])


def _project_b(h, win_ref, bglu_ref, half):
    c0 = half * MXU_N
    glu_a = _dot(h(), win_ref[:, GLU_OFF + c0:GLU_OFF + c0 + MXU_N]) + bglu_ref[:, c0:c0 + MXU_N]
    glu_b = (_dot(h(), win_ref[:, GLU_OFF + B_WIDTH + c0:GLU_OFF + B_WIDTH + c0 + MXU_N])
             + bglu_ref[:, B_WIDTH + c0:B_WIDTH + c0 + MXU_N])
    zb = _dot(h(), win_ref[:, ZB_OFF + c0:ZB_OFF + c0 + MXU_N])
    return glu_a * _sigmoid(glu_b), _silu(zb)


def _rmsnorm(x, g):
    return x * lax.rsqrt(jnp.mean(x * x, axis=-1, keepdims=True) + EPS) * g


def _group_b_out(cv, lng_ref, lnb_ref, gate_b):
    mu = jnp.mean(cv, axis=-1, keepdims=True)
    d = cv - mu
    var = jnp.mean(d * d, axis=-1, keepdims=True)
    ln = d * lax.rsqrt(var + EPS) * lng_ref[...] + lnb_ref[...]
    return _silu(ln) * gate_b


def _cast_weight(src_hbm, dst_ref, slab_refs, sems):
    n = dst_ref.shape[1] // LANES
    slots = len(slab_refs)

    def copy(j):
        return pltpu.make_async_copy(src_hbm.at[:, pl.ds(j * LANES, LANES)], slab_refs[j % slots], sems.at[j % slots])

    for j in range(min(slots, n)):
        copy(j).start()
    for j in range(n):
        copy(j).wait()
        dst_ref[:, j * LANES:(j + 1) * LANES] = slab_refs[j % slots][...].astype(_BF16)
        if j + slots < n:
            copy(j + slots).start()


def _conv_stages(ua_ref, ub_ref, wb_ref, convb_ref, lng_ref, lnb_ref, gb_ref, cv_ref, cat_ref, base, n):
    reps = lambda w, rows: jnp.concatenate([w] * (rows // (2 * SUBLANES)), axis=0)
    for ls in range(B_WIDTH // LANES):
        lanes = slice(ls * LANES, (ls + 1) * LANES)
        acc = jnp.broadcast_to(convb_ref[:, lanes], (n, LANES))
        for off in range(SUBLANES):
            taps = [j for j in range(CONV_WIDTH) if (FIRST + j) % SUBLANES == off]
            rows = n + (2 * SUBLANES if off else 0)
            part = None
            for j in taps:
                rel = FIRST + j - off
                if rel % (2 * SUBLANES) == 0:
                    win = ua_ref[pl.ds(_aligned(base + rel, 2 * SUBLANES), rows), lanes]
                else:
                    win = ub_ref[pl.ds(_aligned(base + rel + SUBLANES, 2 * SUBLANES), rows), lanes]
                term = reps(wb_ref[j, :, lanes], rows) * win
                part = term if part is None else part + term
            part = part.astype(_F32)
            acc = acc + (part[off:off + n, :] if off else part)
            if off == SUBLANES // 2 - 1:
                yield
        cv_ref[:, lanes] = acc
        yield
    rows = pl.ds(_aligned(base, n), n)
    o_b = _group_b_out(cv_ref[...], lng_ref, lnb_ref, gb_ref[rows, :])
    cat_ref[rows, A_WIDTH:] = o_b.astype(_BF16)


def _prompt_kernel(x_ref, normg_ref, win_hbm, lbl_ref, ag_ref, bglu_ref, convw_ref, convb_ref,
                   lng_ref, lnb_ref, wout_hbm, fg_ref, masks_ref,
                   y_ref, st_out_ref, cs_out_ref, win_out_hbm, wout_out_hbm,
                   pa_ref, uh_ref, ua_ref, ub_ref, gb_ref, cv_ref, oa_ref, cat_ref, st_ref,
                   win_ref, wout_ref, wout_f32_ref, lb_ref, wb_ref, sems, *, tb, layer):
    seq = pl.program_id(0)
    t = pl.program_id(1)
    nt = pl.num_programs(1)

    first = (seq == 0) & (t == 0)
    n_slabs = pa_ref.shape[0]
    handoff = [pltpu.make_async_copy(win_ref, win_out_hbm, sems.at[n_slabs]),
               pltpu.make_async_copy(wout_ref, wout_out_hbm, sems.at[n_slabs + 1])]
    wout_fetch = pltpu.make_async_copy(wout_hbm, wout_f32_ref, sems.at[n_slabs + 2])

    @pl.when(first)
    def _():
        _cast_weight(win_hbm, win_ref, [pa_ref.at[s, pl.ds(0, win_ref.shape[0])] for s in range(n_slabs)], sems)
        wout_fetch.start()
        handoff[0].start()
        lb_ref[...] = _forget_lower_bound(lbl_ref, layer)
        for j in range(CONV_WIDTH):
            wb_ref[j] = jnp.broadcast_to(convw_ref[j:j + 1, :], (2 * SUBLANES, B_WIDTH)).astype(_BF16)
        cs_out_ref[...] = jnp.zeros_like(cs_out_ref)

    @pl.when(t == 0)
    def _():
        st_ref[...] = jnp.zeros_like(st_ref)
        uh_ref[...] = jnp.zeros_like(uh_ref)
        ua_ref[...] = jnp.zeros_like(ua_ref)
        ub_ref[...] = jnp.zeros_like(ub_ref)

    def front_tile(r0):
        rows = pl.ds(r0, ROW_TILE)
        h = _rmsnorm(x_ref[rows, :], normg_ref[...]).astype(_BF16)
        for half in range(B_WIDTH // MXU_N):
            u, gb = _project_b(lambda: h, win_ref, bglu_ref, half)
            cols = slice(half * MXU_N, (half + 1) * MXU_N)
            here = pl.ds(_aligned(HIST + r0, HIST), ROW_TILE)
            prev = uh_ref[pl.ds(_aligned(HIST + r0 - SUBLANES, SUBLANES), SUBLANES), cols]
            uh_ref[here, cols] = u
            ua_ref[here, cols] = u.astype(_BF16)
            ub_ref[here, cols] = jnp.concatenate([prev, u[:ROW_TILE - SUBLANES, :]], axis=0).astype(_BF16)
            gb_ref[rows, cols] = gb

        def store_a(slab, val):
            pa_ref[slab, rows, :] = val
        for cg in range(2 * A_HEADS):
            _project_a(lambda: h, win_ref, lb_ref, store_a, cg)

    def front_body(p, carry):
        for n in range(UNROLL):
            front_tile(_aligned((UNROLL * p + n) * ROW_TILE, ROW_TILE))
        return carry

    _loop(tb // (UNROLL * ROW_TILE), front_body)
    ub_ref[HIST + tb:HIST + tb + 2 * SUBLANES, :] = jnp.concatenate(
        [uh_ref[HIST + tb - SUBLANES:HIST + tb, :], jnp.zeros((SUBLANES, B_WIDTH), _F32)], axis=0).astype(_BF16)

    def chunk(base, slot):
        def head(hd):
            def slabs(arr):
                return [pa_ref[arr * A_HEADS + hd, pl.ds(base + i, SUBLANES, stride=SUBLANES), :]
                        for i in range(SUBLANES)]

            def done(o, qc, kc, v_all, decay):
                st_ref[hd] = st_ref[hd] * decay + _dot_tn(v_all, kc)
                for i in range(SUBLANES):
                    oa_ref[hd, pl.ds(base + i, SUBLANES, stride=SUBLANES), :] = o[i * SUBLANES:(i + 1) * SUBLANES, :]
            return _hgrn_stages(slabs(0), slabs(1), slabs(2), masks_ref, lambda: st_ref[hd], True, done)
        conv = _conv_stages(ua_ref, ub_ref, wb_ref, convb_ref, lng_ref, lnb_ref, gb_ref, cv_ref.at[slot], cat_ref, base, CHUNK)
        return [head(hd) for hd in range(A_HEADS)] + [conv]

    def chunk_body(p, carry):
        _round_robin([chunk(_aligned((CHUNK_UNROLL * p + n) * CHUNK, CHUNK), n % CONV_BUFS) for n in range(CHUNK_UNROLL)])
        return carry

    _loop(tb // (CHUNK_UNROLL * CHUNK), chunk_body)

    @pl.when(t == nt - 1)
    def _():
        grp = pl.ds(pl.multiple_of((seq // SUBLANES) * SUBLANES, SUBLANES), SUBLANES)
        mine = lax.broadcasted_iota(jnp.int32, (SUBLANES, B_WIDTH), 0) == seq % SUBLANES
        for kk in range(CONV_WIDTH - 1):
            row = uh_ref[HIST + tb - (CONV_WIDTH - 1) + kk:HIST + tb - (CONV_WIDTH - 1) + kk + 1, :]
            cs_out_ref[kk, grp, :] = jnp.where(mine, jnp.broadcast_to(row, (SUBLANES, B_WIDTH)), cs_out_ref[kk, grp, :])
        for hd in range(A_HEADS):
            st_out_ref[hd] = st_ref[hd].T

    uh_ref[0:HIST, :] = uh_ref[tb:tb + HIST, :]
    ua_ref[0:HIST, :] = ua_ref[tb:tb + HIST, :]
    ub_ref[0:HIST, :] = ub_ref[tb:tb + HIST, :]

    @pl.when(first)
    def _():
        wout_fetch.wait()
        for r0 in range(0, wout_ref.shape[0], ROW_TILE):
            wout_ref[r0:r0 + ROW_TILE, :] = wout_f32_ref[r0:r0 + ROW_TILE, :].astype(_BF16)
        handoff[1].start()

    def back_body(p, carry):
        tiles = [pl.ds(_aligned((UNROLL * p + n) * ROW_TILE, ROW_TILE), ROW_TILE) for n in range(UNROLL)]
        for rows in tiles:
            for hd in range(A_HEADS):
                on = _head_out(oa_ref[hd, rows, :], ag_ref[...], pa_ref[3 * A_HEADS + hd, rows, :])
                cat_ref[rows, hd * HEAD_DIM:(hd + 1) * HEAD_DIM] = on.astype(_BF16)
        outs = [_dot(cat_ref[rows, :], wout_ref[...]) for rows in tiles]
        for rows, out in zip(tiles, outs):
            y_ref[rows, :] = _rmsnorm(x_ref[rows, :] + out, fg_ref[...])
        return carry

    _loop(tb // (UNROLL * ROW_TILE), back_body)

    @pl.when(first)
    def _():
        for copy in handoff:
            copy.wait()


def _weight_specs(weights, layer, in_hbm=()):
    specs = [_full(w.shape) for w in weights]
    specs[5] = pl.BlockSpec((None,) + weights[5].shape[1:], lambda *_: (layer, 0, 0), pipeline_mode=pl.Buffered(1))
    for i in in_hbm:
        specs[i] = pl.BlockSpec(memory_space=pl.ANY)
    return specs


def _full(shape):
    nd = len(shape)
    return pl.BlockSpec(shape, lambda *_: (0,) * nd, pipeline_mode=pl.Buffered(1))


def _prompt_call(x, weights, tb, layer):
    n, t, _ = x.shape
    assert t % tb == 0 and tb % (UNROLL * ROW_TILE) == 0 and tb % (CHUNK_UNROLL * CHUNK) == 0
    masks = jnp.asarray(_level_masks((1, 2, 4, 8, 16, 32), SUBLANES, SUBLANES))
    assert n % SUBLANES == 0
    kern = functools.partial(_prompt_kernel, tb=tb, layer=layer)
    w_in, w_out = weights[1], weights[9]
    assert w_in.dtype == _F32 and w_out.dtype == _F32 and w_in.shape[0] <= tb and w_in.shape[1] % LANES == 0
    in_hbm = pl.BlockSpec(memory_space=pl.ANY)
    w_specs = _weight_specs(weights, layer, in_hbm=(1, 9))
    return pl.pallas_call(
        kern,
        grid=(n, t // tb),
        in_specs=[pl.BlockSpec((None, tb, D_MODEL), lambda i, j: (i, j, 0))] + w_specs + [_full(masks.shape)],
        out_specs=[
            pl.BlockSpec((None, tb, D_MODEL), lambda i, j: (i, j, 0)),
            pl.BlockSpec((None, A_HEADS, HEAD_DIM, HEAD_DIM), lambda i, j: (i, 0, 0, 0)),
            _full((CONV_WIDTH - 1, n, B_WIDTH)),
            in_hbm,
            in_hbm,
        ],
        out_shape=[
            jax.ShapeDtypeStruct((n, t, D_MODEL), _F32),
            jax.ShapeDtypeStruct((n, A_HEADS, HEAD_DIM, HEAD_DIM), _F32),
            jax.ShapeDtypeStruct((CONV_WIDTH - 1, n, B_WIDTH), _F32),
            jax.ShapeDtypeStruct(w_in.shape, _BF16),
            jax.ShapeDtypeStruct(w_out.shape, _BF16),
        ],
        scratch_shapes=[
            pltpu.VMEM((4 * A_HEADS, tb, LANES), _F32),
            pltpu.VMEM((HIST + tb, B_WIDTH), _F32),
            pltpu.VMEM((HIST + tb + 2 * SUBLANES, B_WIDTH), _BF16),
            pltpu.VMEM((HIST + tb + 2 * SUBLANES, B_WIDTH), _BF16),
            pltpu.VMEM((tb, B_WIDTH), _F32),
            pltpu.VMEM((CONV_BUFS, CHUNK, B_WIDTH), _F32),
            pltpu.VMEM((A_HEADS, tb, LANES), _F32),
            pltpu.VMEM((tb, A_WIDTH + B_WIDTH), _BF16),
            pltpu.VMEM((A_HEADS, HEAD_DIM, HEAD_DIM), _F32),
            pltpu.VMEM(w_in.shape, _BF16),
            pltpu.VMEM(w_out.shape, _BF16),
            pltpu.VMEM(w_out.shape, _F32),
            pltpu.VMEM((1, A_WIDTH), _F32),
            pltpu.VMEM((CONV_WIDTH, 2 * SUBLANES, B_WIDTH), _BF16),
            pltpu.SemaphoreType.DMA((4 * A_HEADS + 3,)),
        ],
        compiler_params=pltpu.CompilerParams(
            dimension_semantics=("arbitrary", "arbitrary"),
            vmem_limit_bytes=VMEM_LIMIT_BYTES,
        ),
        name="hymba_prompt",
    )(x, *weights, masks)


def _decode_kernel(x_ref, s0_ref, c0_ref, normg_ref, win_ref, lbl_ref, ag_ref, bglu_ref, convw_ref, convb_ref,
                   lng_ref, lnb_ref, wout_ref, fg_ref, masks_ref,
                   y_ref, s_out_ref, cs_out_ref,
                   pa_ref, ub_ref, gb_ref, oa_ref, ob_ref, lb_ref, xt_ref, *, t_dec, layer):
    g = pl.program_id(0)
    n_seq = y_ref.shape[0]
    lane_tiles = D_MODEL // LANES
    grp = SUBLANES * t_dec
    n_ls = B_WIDTH // LANES
    tiles = [(pl.ds(t * n_seq + s0, ROW_TILE), pl.ds(s0, ROW_TILE), t)
             for t in range(t_dec) for s0 in range(0, n_seq, ROW_TILE)]

    @pl.when(g == 0)
    def _():
        lb_ref[...] = _forget_lower_bound(lbl_ref, layer)
        for rows, seq_rows, t in tiles:
            for j in range(lane_tiles):
                start = (seq_rows.start * lane_tiles + j) * t_dec + t
                xt_ref[rows, j * LANES:(j + 1) * LANES] = x_ref[pl.ds(start, ROW_TILE, stride=lane_tiles * t_dec), :]
            h = _rmsnorm(xt_ref[rows, :], normg_ref[...]).astype(_BF16)
            for half in range(B_WIDTH // MXU_N):
                u, gb = _project_b(lambda: h, win_ref, bglu_ref, half)
                for c in range(MXU_N // LANES):
                    ls = half * (MXU_N // LANES) + c
                    ub_ref[ls, rows, :] = u[:, c * LANES:(c + 1) * LANES]
                    gb_ref[ls, rows, :] = gb[:, c * LANES:(c + 1) * LANES]

            def store_a(slab, val, rows=rows):
                pa_ref[slab, rows, :] = val
            for cg in range(2 * A_HEADS):
                _project_a(lambda: h, win_ref, lb_ref, store_a, cg)

    def group(sub):
        seqs = slice(sub * SUBLANES, (sub + 1) * SUBLANES)
        seq0 = pl.multiple_of((g * DEC_GROUPS + sub) * SUBLANES, SUBLANES)
        token_rows = lambda t: pl.ds(t * n_seq + seq0, SUBLANES)

        def token_slab(ref, idx, t):
            return ref[idx, token_rows(t), :]

        u_tok = [jnp.concatenate([token_slab(ub_ref, ls, t) for ls in range(n_ls)], axis=-1) for t in range(t_dec)]
        full = lambda kk: c0_ref[kk, seqs, :] if kk < CONV_WIDTH - 1 else u_tok[kk - (CONV_WIDTH - 1)]
        for kk in range(CONV_WIDTH - 1):
            cs_out_ref[kk, seqs, :] = full(kk + t_dec)
        for t in range(t_dec):
            cv = jnp.broadcast_to(convb_ref[...], (SUBLANES, B_WIDTH))
            for j in range(CONV_WIDTH):
                cv = cv + convw_ref[j:j + 1, :] * full(t + j)
            gate_b = jnp.concatenate([token_slab(gb_ref, ls, t) for ls in range(n_ls)], axis=-1)
            o_b = _group_b_out(cv, lng_ref, lnb_ref, gate_b)
            for ls in range(n_ls):
                ob_ref[ls, token_rows(t), :] = o_b[:, ls * LANES:(ls + 1) * LANES]

        row32 = lax.broadcasted_iota(jnp.int32, (grp, LANES), 0) % SUBLANES
        row64 = lax.broadcasted_iota(jnp.int32, (2 * grp, LANES), 0)
        for hd in range(A_HEADS):
            slabs = lambda arr: [token_slab(pa_ref, arr * A_HEADS + hd, t) for t in range(t_dec)]
            res = []
            for _ in _hgrn_stages(slabs(0), slabs(1), slabs(2), masks_ref, None, False, lambda *a: res.extend(a)):
                pass
            o, qc, kc, v_all, decay = res
            qf, kf, vf = qc.astype(_F32), kc.astype(_F32), v_all.astype(_F32)
            e_hi = decay.astype(_BF16).astype(_F32)
            e_mid = (decay - e_hi).astype(_BF16).astype(_F32)
            e_lo = (decay - e_hi - e_mid).astype(_BF16).astype(_F32)
            lhs = jnp.concatenate([kf, e_hi, e_mid, e_lo, jnp.zeros((SUBLANES, LANES), _F32)], axis=0).astype(_BF16)
            v_pad = jnp.concatenate([vf, jnp.zeros((grp, LANES), _F32)], axis=0)
            o_state = None
            for s in range(SUBLANES):
                s0 = s0_ref[sub * SUBLANES + s, hd]
                term = _dot(jnp.where(row32 == s, qf, 0.0).astype(_BF16), s0.astype(_BF16))
                o_state = term if o_state is None else o_state + term
                mine = row64 % SUBLANES == s
                rhs = jnp.concatenate([
                    jnp.where(mine & (row64 < grp), v_pad, 0.0),
                    jnp.where(mine & (row64 >= grp) & (row64 < grp + 3 * SUBLANES), 1.0, 0.0)], axis=-1).astype(_BF16)
                upd = _dot_tn(lhs, rhs)
                s_out_ref[sub * SUBLANES + s, hd] = s0 * upd[:, HEAD_DIM:] + upd[:, :HEAD_DIM]
            gate = jnp.concatenate([token_slab(pa_ref, 3 * A_HEADS + hd, t) for t in range(t_dec)], axis=0)
            on = _head_out(o + o_state, ag_ref[...], gate)
            for t in range(t_dec):
                oa_ref[hd, token_rows(t), :] = on[t * SUBLANES:(t + 1) * SUBLANES, :]

    for sub in range(DEC_GROUPS):
        group(sub)

    @pl.when(g == pl.num_programs(0) - 1)
    def _():
        for rows, seq_rows, t in tiles:
            cat = jnp.concatenate([oa_ref[hd, rows, :] for hd in range(A_HEADS)]
                                  + [ob_ref[ls, rows, :] for ls in range(n_ls)], axis=-1).astype(_BF16)
            out = _dot(cat, wout_ref[...])
            y_ref[seq_rows, t, :] = _rmsnorm(xt_ref[rows, :] + out, fg_ref[...])


def _decode_call(x, s0, c0, weights, layer):
    n, t_dec, _ = x.shape
    assert n % ROW_TILE == 0 and t_dec == 4
    n_tok = n * t_dec
    x_rows = x.reshape(n, t_dec, D_MODEL // LANES, LANES).transpose(0, 2, 1, 3).reshape(n_tok * D_MODEL // LANES, LANES)
    masks = jnp.asarray(_level_masks((1, 2), t_dec, SUBLANES, same_group=t_dec))
    kern = functools.partial(_decode_kernel, t_dec=t_dec, layer=layer)
    w_specs = _weight_specs(weights, layer)
    per_step = DEC_GROUPS * SUBLANES
    state_spec = pl.BlockSpec((per_step, A_HEADS, HEAD_DIM, HEAD_DIM), lambda i: (i, 0, 0, 0))
    conv_spec = pl.BlockSpec((CONV_WIDTH - 1, per_step, B_WIDTH), lambda i: (0, i, 0))
    y, s_new, c_new = pl.pallas_call(
        kern,
        grid=(n // per_step,),
        in_specs=[_full(x_rows.shape), state_spec, conv_spec] + w_specs + [_full(masks.shape)],
        out_specs=[_full(x.shape), state_spec, conv_spec],
        out_shape=[
            jax.ShapeDtypeStruct(x.shape, _F32),
            jax.ShapeDtypeStruct((n, A_HEADS, HEAD_DIM, HEAD_DIM), _F32),
            jax.ShapeDtypeStruct((CONV_WIDTH - 1, n, B_WIDTH), _F32),
        ],
        scratch_shapes=[
            pltpu.VMEM((4 * A_HEADS, n_tok, LANES), _F32),
            pltpu.VMEM((B_WIDTH // LANES, n_tok, LANES), _F32),
            pltpu.VMEM((B_WIDTH // LANES, n_tok, LANES), _F32),
            pltpu.VMEM((A_HEADS, n_tok, LANES), _F32),
            pltpu.VMEM((B_WIDTH // LANES, n_tok, LANES), _F32),
            pltpu.VMEM((1, A_WIDTH), _F32),
            pltpu.VMEM((n_tok, D_MODEL), _F32),
        ],
        compiler_params=pltpu.CompilerParams(
            dimension_semantics=("arbitrary",),
            vmem_limit_bytes=VMEM_LIMIT_BYTES,
        ),
        name="hymba_decode",
    )(x_rows, s0, c0, *weights, masks)
    return y, s_new, c_new


def _prepare_weights(norm_in_g, w_in, lb_logits, hgrn_norm_g, b_glu, conv_w, conv_b, ln_g, ln_b, w_out, final_norm_g, layer):
    row = lambda a: a.astype(_F32).reshape(1, -1)
    return (
        row(norm_in_g[layer]),
        w_in[layer].astype(_F32),
        lb_logits.astype(_F32),
        row(hgrn_norm_g[layer]),
        row(b_glu[layer]),
        conv_w.astype(_F32),
        row(conv_b[layer]),
        row(ln_g[layer]),
        row(ln_b[layer]),
        w_out[layer].astype(_F32),
        row(final_norm_g),
    )


def kernel(x_prompt, x_sample, state_hgrn, state_conv, norm_in_g, w_in, lb_logits, hgrn_norm_g, b_glu, conv_w,
           conv_b, ln_g, ln_b, w_out, final_norm_g):
    depth = w_in.shape[0]
    assert depth == 1, "single mixer layer: the final norm is fused into the layer kernel"
    weights = _prepare_weights(norm_in_g, w_in, lb_logits, hgrn_norm_g, b_glu, conv_w, conv_b, ln_g, ln_b, w_out,
                               final_norm_g, 0)
    y_p, s_p, c_p, w_in_b, w_out_b = _prompt_call(x_prompt, weights, tb=1024, layer=0)
    weights = weights[:1] + (w_in_b,) + weights[2:9] + (w_out_b,) + weights[10:]
    y_s, s_s, c_s = _decode_call(x_sample, state_hgrn[0], jnp.transpose(state_conv[0], (1, 0, 2)), weights, layer=0)
    return (y_p, y_s, s_p[None], jnp.transpose(c_p, (1, 0, 2))[None], s_s[None], jnp.transpose(c_s, (1, 0, 2))[None])
```

```python
import functools
import math

import numpy as np
import jax
import jax.numpy as jnp
from jax import lax
from jax.experimental import pallas as pl
from jax.experimental.pallas import tpu as pltpu

D_MODEL = 1024
A_HEADS = 4
HEAD_DIM = 128
A_WIDTH = A_HEADS * HEAD_DIM
B_WIDTH = 512
CONV_WIDTH = 31
EPS = 1e-6
IN_WIDTH = 4 * A_WIDTH + 3 * B_WIDTH
GLU_OFF = 4 * A_WIDTH
ZB_OFF = GLU_OFF + 2 * B_WIDTH

LANES = 128
SUBLANES = 8
MXU_N = 256
CHUNK = SUBLANES * SUBLANES
ROW_TILE = 128
UNROLL = 8
CHUNK_UNROLL = 2
CHUNK_LAG = 4
CONV_BUFS = 3
DEC_GROUPS = 2
HIST = 32
FIRST = HIST - (CONV_WIDTH - 1)
VMEM_LIMIT_BYTES = 56 * 1024 * 1024
LOG2E = math.log2(math.e)

_F32 = jnp.float32
_BF16 = jnp.bfloat16


def _sigmoid(x):
    return 1.0 / (1.0 + jnp.exp(-x))


def _silu(x):
    return x * _sigmoid(x)


def _dot(a, b):
    return jnp.dot(a, b, preferred_element_type=_F32)


def _dot_nt(a, b):
    return lax.dot_general(a, b, (((1,), (1,)), ((), ())), preferred_element_type=_F32)


def _dot_tn(a, b):
    return lax.dot_general(a, b, (((0,), (0,)), ((), ())), preferred_element_type=_F32)


def _level_masks(levels, n_res, n_rows, same_group=None):
    r = np.arange(n_res * n_rows)
    t = n_res * (r % n_rows) + r // n_rows
    tt, ss = t[:, None], t[None, :]
    masks = [tt == ss]
    for b in levels:
        masks.append(((tt // b) == (ss // b) + 1) & ((ss // b) % 2 == 0))
    masks = np.stack(masks)
    want = (ss <= tt) if same_group is None else ((ss <= tt) & (tt // same_group == ss // same_group))
    assert (masks.sum(0) == want).all()
    return masks.astype(np.float32)


def _rows(x, r):
    return jnp.broadcast_to(x[r:r + 1, :], x.shape)


def _hgrn_stages(q, f, v, masks_ref, get_st, coarse, done):
    n_res = len(q)
    every = range(n_res)
    k = [1.0 - fi for fi in f]
    lf = [jnp.log(fi) * LOG2E for fi in f]
    g_in = [lf[0]]
    for i in range(1, n_res):
        g_in.append(g_in[-1] + lf[i])
    tot = g_in[-1]
    zero = jnp.zeros_like(tot)

    def cat(parts):
        return jnp.concatenate(parts, axis=0).astype(_BF16)

    def slab_rows(x, i):
        return x[i * SUBLANES:(i + 1) * SUBLANES, :]

    v_all = cat(v)
    s_rows = [None] * n_res
    pending = []

    def fold():
        level, rows, s_l = pending.pop(0)
        for n, i in enumerate(rows):
            term = masks_ref[level, i * SUBLANES:(i + 1) * SUBLANES, :] * slab_rows(s_l, n)
            s_rows[i] = term if s_rows[i] is None else s_rows[i] + term

    def step(rows, qt, kt):
        pending.append((step.level, rows, _dot_nt(cat(qt), cat(kt))))
        step.level += 1
    step.level = 0

    step(every, q, k)
    yield
    b = 1
    while b < n_res:
        odd = [i for i in every if (i // b) % 2 == 1]
        qt, kt = [], []
        for i in every:
            bs = (i // b) * b
            if i in odd:
                qt.append(q[i] * jnp.exp2(g_in[i] - g_in[bs - 1]))
                kt.append(zero)
            else:
                be = bs + b - 1
                kt.append(k[i] * jnp.exp2(g_in[be] - g_in[i]) if i != be else k[i])
        step(odd, qt, kt)
        yield
        fold()
        b *= 2
    qe = [q[i] * jnp.exp2(g_in[i]) for i in every]
    ke = [k[i] * jnp.exp2(tot - g_in[i]) for i in range(n_res - 1)] + [k[-1]]
    if coarse:
        row = lax.broadcasted_iota(jnp.int32, tot.shape, 0)
        pref = tot
        for sh in (1, 2, 4):
            pref = pref + jnp.where(row >= sh, pltpu.roll(pref, sh, 0), 0.0)
        before = pref - tot
        end_all = _rows(pref, SUBLANES - 1)
        qe_b, ke_b = cat(qe), cat(ke)
        tile = lambda x: jnp.concatenate([x] * n_res, axis=0).astype(_BF16)
        for gsz in (1, 2, 4):
            if gsz == 1:
                qt, kt = qe_b, ke_b
            else:
                if gsz == 2:
                    g_start = jnp.where(row % 2 == 0, before, pltpu.roll(before, 1, 0))
                    g_end = jnp.where(row % 2 == 1, pref, pltpu.roll(pref, SUBLANES - 1, 0))
                else:
                    g_start = jnp.where(row < 4, _rows(before, 0), _rows(before, 4))
                    g_end = jnp.where(row < 4, _rows(pref, 3), _rows(pref, 7))
                qt = qe_b * tile(jnp.exp2(before - g_start))
                kt = ke_b * tile(jnp.exp2(g_end - pref))
            pending.append((step.level, every, _dot_nt(qt, kt)))
            step.level += 1
            yield
            fold()
        qc = qe_b * tile(jnp.exp2(before))
        kc = ke_b * tile(jnp.exp2(end_all - pref))
        o_st = _dot_nt(qc, get_st().astype(_BF16))
        decay = jnp.exp2(pref[SUBLANES - 1:SUBLANES, :])
        yield
    else:
        qc, kc = cat(qe), cat(ke)
        o_st = None
        decay = jnp.exp2(tot)
    while pending:
        fold()
    o = _dot(cat(s_rows), v_all)
    yield
    done(o if o_st is None else o + o_st, qc, kc, v_all, decay)


def _round_robin(starts):
    waiting = list(starts)
    live = []
    rnd = 0
    while waiting or live:
        if waiting and rnd % CHUNK_LAG == 0:
            live += waiting.pop(0)
        for g in list(live):
            try:
                next(g)
            except StopIteration:
                live.remove(g)
        rnd += 1


def _aligned(x, m):
    return x if isinstance(x, int) else pl.multiple_of(x, m)


def _loop(trips, body):
    if trips == 1:
        body(0, 0)
    else:
        lax.fori_loop(0, trips, body, 0)


def _head_out(o, ag, gate):
    ms = jnp.mean(o * o, axis=-1, keepdims=True)
    return o * lax.rsqrt(ms + EPS) * ag * gate


def _forget_lower_bound(logits_ref, layer):
    rows = [logits_ref[i:i + 1, :] for i in range(logits_ref.shape[0])]
    top = functools.reduce(jnp.maximum, rows)
    e = [jnp.exp(r - top) for r in rows]
    total = functools.reduce(jnp.add, e)
    return functools.reduce(jnp.add, [v / total for v in e[:layer + 1]])


def _project_a(h, win_ref, lb_ref, store, cg):
    p = _dot(h(), win_ref[:, cg * MXU_N:(cg + 1) * MXU_N])
    arr = cg // 2
    if arr == 0 or arr == 3:
        p = _silu(p)
    elif arr == 1:
        lb = lb_ref[:, (cg % 2) * MXU_N:(cg % 2 + 1) * MXU_N]
        p = lb + (1.0 - lb) * _sigmoid(p)
    store(2 * cg, p[:, :LANES])
    store(2 * cg + 1, p[:, LANES:])


def _project_b(h, win_ref, bglu_ref, half):
    c0 = half * MXU_N
    glu_a = _dot(h(), win_ref[:, GLU_OFF + c0:GLU_OFF + c0 + MXU_N]) + bglu_ref[:, c0:c0 + MXU_N]
    glu_b = (_dot(h(), win_ref[:, GLU_OFF + B_WIDTH + c0:GLU_OFF + B_WIDTH + c0 + MXU_N])
             + bglu_ref[:, B_WIDTH + c0:B_WIDTH + c0 + MXU_N])
    zb = _dot(h(), win_ref[:, ZB_OFF + c0:ZB_OFF + c0 + MXU_N])
    return glu_a * _sigmoid(glu_b), _silu(zb)


def _rmsnorm(x, g):
    return x * lax.rsqrt(jnp.mean(x * x, axis=-1, keepdims=True) + EPS) * g


def _group_b_out(cv, lng_ref, lnb_ref, gate_b):
    mu = jnp.mean(cv, axis=-1, keepdims=True)
    d = cv - mu
    var = jnp.mean(d * d, axis=-1, keepdims=True)
    ln = d * lax.rsqrt(var + EPS) * lng_ref[...] + lnb_ref[...]
    return _silu(ln) * gate_b


def _cast_weight(src_hbm, dst_ref, slab_refs, sems):
    n = dst_ref.shape[1] // LANES
    slots = len(slab_refs)

    def copy(j):
        return pltpu.make_async_copy(src_hbm.at[:, pl.ds(j * LANES, LANES)], slab_refs[j % slots], sems.at[j % slots])

    for j in range(min(slots, n)):
        copy(j).start()
    for j in range(n):
        copy(j).wait()
        dst_ref[:, j * LANES:(j + 1) * LANES] = slab_refs[j % slots][...].astype(_BF16)
        if j + slots < n:
            copy(j + slots).start()


def _conv_stages(ua_ref, ub_ref, wb_ref, convb_ref, lng_ref, lnb_ref, gb_ref, cv_ref, cat_ref, base, n):
    reps = lambda w, rows: jnp.concatenate([w] * (rows // (2 * SUBLANES)), axis=0)
    for ls in range(B_WIDTH // LANES):
        lanes = slice(ls * LANES, (ls + 1) * LANES)
        acc = jnp.broadcast_to(convb_ref[:, lanes], (n, LANES))
        for off in range(SUBLANES):
            taps = [j for j in range(CONV_WIDTH) if (FIRST + j) % SUBLANES == off]
            rows = n + (2 * SUBLANES if off else 0)
            part = None
            for j in taps:
                rel = FIRST + j - off
                if rel % (2 * SUBLANES) == 0:
                    win = ua_ref[pl.ds(_aligned(base + rel, 2 * SUBLANES), rows), lanes]
                else:
                    win = ub_ref[pl.ds(_aligned(base + rel + SUBLANES, 2 * SUBLANES), rows), lanes]
                term = reps(wb_ref[j, :, lanes], rows) * win
                part = term if part is None else part + term
            part = part.astype(_F32)
            acc = acc + (part[off:off + n, :] if off else part)
            if off == SUBLANES // 2 - 1:
                yield
        cv_ref[:, lanes] = acc
        yield
    rows = pl.ds(_aligned(base, n), n)
    o_b = _group_b_out(cv_ref[...], lng_ref, lnb_ref, gb_ref[rows, :])
    cat_ref[rows, A_WIDTH:] = o_b.astype(_BF16)


def _prompt_kernel(x_ref, normg_ref, win_hbm, lbl_ref, ag_ref, bglu_ref, convw_ref, convb_ref,
                   lng_ref, lnb_ref, wout_hbm, fg_ref, masks_ref,
                   y_ref, st_out_ref, cs_out_ref, win_out_hbm, wout_out_hbm,
                   pa_ref, uh_ref, ua_ref, ub_ref, gb_ref, cv_ref, oa_ref, cat_ref, st_ref,
                   win_ref, wout_ref, wout_f32_ref, lb_ref, wb_ref, sems, *, tb, layer):
    seq = pl.program_id(0)
    t = pl.program_id(1)
    nt = pl.num_programs(1)

    first = (seq == 0) & (t == 0)
    n_slabs = pa_ref.shape[0]
    handoff = [pltpu.make_async_copy(win_ref, win_out_hbm, sems.at[n_slabs]),
               pltpu.make_async_copy(wout_ref, wout_out_hbm, sems.at[n_slabs + 1])]
    wout_fetch = pltpu.make_async_copy(wout_hbm, wout_f32_ref, sems.at[n_slabs + 2])

    @pl.when(first)
    def _():
        _cast_weight(win_hbm, win_ref, [pa_ref.at[s, pl.ds(0, win_ref.shape[0])] for s in range(n_slabs)], sems)
        wout_fetch.start()
        handoff[0].start()
        lb_ref[...] = _forget_lower_bound(lbl_ref, layer)
        for j in range(CONV_WIDTH):
            wb_ref[j] = jnp.broadcast_to(convw_ref[j:j + 1, :], (2 * SUBLANES, B_WIDTH)).astype(_BF16)
        cs_out_ref[...] = jnp.zeros_like(cs_out_ref)

    @pl.when(t == 0)
    def _():
        st_ref[...] = jnp.zeros_like(st_ref)
        uh_ref[...] = jnp.zeros_like(uh_ref)
        ua_ref[...] = jnp.zeros_like(ua_ref)
        ub_ref[...] = jnp.zeros_like(ub_ref)

    def front_tile(r0):
        rows = pl.ds(r0, ROW_TILE)
        h = _rmsnorm(x_ref[rows, :], normg_ref[...]).astype(_BF16)
        for half in range(B_WIDTH // MXU_N):
            u, gb = _project_b(lambda: h, win_ref, bglu_ref, half)
            cols = slice(half * MXU_N, (half + 1) * MXU_N)
            here = pl.ds(_aligned(HIST + r0, HIST), ROW_TILE)
            prev = uh_ref[pl.ds(_aligned(HIST + r0 - SUBLANES, SUBLANES), SUBLANES), cols]
            uh_ref[here, cols] = u
            ua_ref[here, cols] = u.astype(_BF16)
            ub_ref[here, cols] = jnp.concatenate([prev, u[:ROW_TILE - SUBLANES, :]], axis=0).astype(_BF16)
            gb_ref[rows, cols] = gb

        def store_a(slab, val):
            pa_ref[slab, rows, :] = val
        for cg in range(2 * A_HEADS):
            _project_a(lambda: h, win_ref, lb_ref, store_a, cg)

    def front_body(p, carry):
        for n in range(UNROLL):
            front_tile(_aligned((UNROLL * p + n) * ROW_TILE, ROW_TILE))
        return carry

    _loop(tb // (UNROLL * ROW_TILE), front_body)
    ub_ref[HIST + tb:HIST + tb + 2 * SUBLANES, :] = jnp.concatenate(
        [uh_ref[HIST + tb - SUBLANES:HIST + tb, :], jnp.zeros((SUBLANES, B_WIDTH), _F32)], axis=0).astype(_BF16)

    def chunk(base, slot):
        def head(hd):
            def slabs(arr):
                return [pa_ref[arr * A_HEADS + hd, pl.ds(base + i, SUBLANES, stride=SUBLANES), :]
                        for i in range(SUBLANES)]

            def done(o, qc, kc, v_all, decay):
                st_ref[hd] = st_ref[hd] * decay + _dot_tn(v_all, kc)
                for i in range(SUBLANES):
                    oa_ref[hd, pl.ds(base + i, SUBLANES, stride=SUBLANES), :] = o[i * SUBLANES:(i + 1) * SUBLANES, :]
            return _hgrn_stages(slabs(0), slabs(1), slabs(2), masks_ref, lambda: st_ref[hd], True, done)
        conv = _conv_stages(ua_ref, ub_ref, wb_ref, convb_ref, lng_ref, lnb_ref, gb_ref, cv_ref.at[slot], cat_ref, base, CHUNK)
        return [head(hd) for hd in range(A_HEADS)] + [conv]

    def chunk_body(p, carry):
        _round_robin([chunk(_aligned((CHUNK_UNROLL * p + n) * CHUNK, CHUNK), n % CONV_BUFS) for n in range(CHUNK_UNROLL)])
        return carry

    _loop(tb // (CHUNK_UNROLL * CHUNK), chunk_body)

    @pl.when(t == nt - 1)
    def _():
        grp = pl.ds(pl.multiple_of((seq // SUBLANES) * SUBLANES, SUBLANES), SUBLANES)
        mine = lax.broadcasted_iota(jnp.int32, (SUBLANES, B_WIDTH), 0) == seq % SUBLANES
        for kk in range(CONV_WIDTH - 1):
            row = uh_ref[HIST + tb - (CONV_WIDTH - 1) + kk:HIST + tb - (CONV_WIDTH - 1) + kk + 1, :]
            cs_out_ref[kk, grp, :] = jnp.where(mine, jnp.broadcast_to(row, (SUBLANES, B_WIDTH)), cs_out_ref[kk, grp, :])
        for hd in range(A_HEADS):
            st_out_ref[hd] = st_ref[hd].T

    uh_ref[0:HIST, :] = uh_ref[tb:tb + HIST, :]
    ua_ref[0:HIST, :] = ua_ref[tb:tb + HIST, :]
    ub_ref[0:HIST, :] = ub_ref[tb:tb + HIST, :]

    @pl.when(first)
    def _():
        wout_fetch.wait()
        for r0 in range(0, wout_ref.shape[0], ROW_TILE):
            wout_ref[r0:r0 + ROW_TILE, :] = wout_f32_ref[r0:r0 + ROW_TILE, :].astype(_BF16)
        handoff[1].start()

    def back_body(p, carry):
        tiles = [pl.ds(_aligned((UNROLL * p + n) * ROW_TILE, ROW_TILE), ROW_TILE) for n in range(UNROLL)]
        for rows in tiles:
            for hd in range(A_HEADS):
                on = _head_out(oa_ref[hd, rows, :], ag_ref[...], pa_ref[3 * A_HEADS + hd, rows, :])
                cat_ref[rows, hd * HEAD_DIM:(hd + 1) * HEAD_DIM] = on.astype(_BF16)
        outs = [_dot(cat_ref[rows, :], wout_ref[...]) for rows in tiles]
        for rows, out in zip(tiles, outs):
            y_ref[rows, :] = _rmsnorm(x_ref[rows, :] + out, fg_ref[...])
        return carry

    _loop(tb // (UNROLL * ROW_TILE), back_body)

    @pl.when(first)
    def _():
        for copy in handoff:
            copy.wait()


def _weight_specs(weights, layer, in_hbm=()):
    specs = [_full(w.shape) for w in weights]
    specs[5] = pl.BlockSpec((None,) + weights[5].shape[1:], lambda *_: (layer, 0, 0), pipeline_mode=pl.Buffered(1))
    for i in in_hbm:
        specs[i] = pl.BlockSpec(memory_space=pl.ANY)
    return specs


def _full(shape):
    nd = len(shape)
    return pl.BlockSpec(shape, lambda *_: (0,) * nd, pipeline_mode=pl.Buffered(1))


def _prompt_call(x, weights, tb, layer):
    n, t, _ = x.shape
    assert t % tb == 0 and tb % (UNROLL * ROW_TILE) == 0 and tb % (CHUNK_UNROLL * CHUNK) == 0
    masks = jnp.asarray(_level_masks((1, 2, 4, 8, 16, 32), SUBLANES, SUBLANES))
    assert n % SUBLANES == 0
    kern = functools.partial(_prompt_kernel, tb=tb, layer=layer)
    w_in, w_out = weights[1], weights[9]
    assert w_in.dtype == _F32 and w_out.dtype == _F32 and w_in.shape[0] <= tb and w_in.shape[1] % LANES == 0
    in_hbm = pl.BlockSpec(memory_space=pl.ANY)
    w_specs = _weight_specs(weights, layer, in_hbm=(1, 9))
    return pl.pallas_call(
        kern,
        grid=(n, t // tb),
        in_specs=[pl.BlockSpec((None, tb, D_MODEL), lambda i, j: (i, j, 0))] + w_specs + [_full(masks.shape)],
        out_specs=[
            pl.BlockSpec((None, tb, D_MODEL), lambda i, j: (i, j, 0)),
            pl.BlockSpec((None, A_HEADS, HEAD_DIM, HEAD_DIM), lambda i, j: (i, 0, 0, 0)),
            _full((CONV_WIDTH - 1, n, B_WIDTH)),
            in_hbm,
            in_hbm,
        ],
        out_shape=[
            jax.ShapeDtypeStruct((n, t, D_MODEL), _F32),
            jax.ShapeDtypeStruct((n, A_HEADS, HEAD_DIM, HEAD_DIM), _F32),
            jax.ShapeDtypeStruct((CONV_WIDTH - 1, n, B_WIDTH), _F32),
            jax.ShapeDtypeStruct(w_in.shape, _BF16),
            jax.ShapeDtypeStruct(w_out.shape, _BF16),
        ],
        scratch_shapes=[
            pltpu.VMEM((4 * A_HEADS, tb, LANES), _F32),
            pltpu.VMEM((HIST + tb, B_WIDTH), _F32),
            pltpu.VMEM((HIST + tb + 2 * SUBLANES, B_WIDTH), _BF16),
            pltpu.VMEM((HIST + tb + 2 * SUBLANES, B_WIDTH), _BF16),
            pltpu.VMEM((tb, B_WIDTH), _F32),
            pltpu.VMEM((CONV_BUFS, CHUNK, B_WIDTH), _F32),
            pltpu.VMEM((A_HEADS, tb, LANES), _F32),
            pltpu.VMEM((tb, A_WIDTH + B_WIDTH), _BF16),
            pltpu.VMEM((A_HEADS, HEAD_DIM, HEAD_DIM), _F32),
            pltpu.VMEM(w_in.shape, _BF16),
            pltpu.VMEM(w_out.shape, _BF16),
            pltpu.VMEM(w_out.shape, _F32),
            pltpu.VMEM((1, A_WIDTH), _F32),
            pltpu.VMEM((CONV_WIDTH, 2 * SUBLANES, B_WIDTH), _BF16),
            pltpu.SemaphoreType.DMA((4 * A_HEADS + 3,)),
        ],
        compiler_params=pltpu.CompilerParams(
            dimension_semantics=("arbitrary", "arbitrary"),
            vmem_limit_bytes=VMEM_LIMIT_BYTES,
        ),
        name="hymba_prompt",
    )(x, *weights, masks)


def _decode_kernel(x_ref, s0_ref, c0_ref, normg_ref, win_ref, lbl_ref, ag_ref, bglu_ref, convw_ref, convb_ref,
                   lng_ref, lnb_ref, wout_ref, fg_ref, masks_ref,
                   y_ref, s_out_ref, cs_out_ref,
                   pa_ref, ub_ref, gb_ref, oa_ref, ob_ref, lb_ref, xt_ref, *, t_dec, layer):
    g = pl.program_id(0)
    n_seq = y_ref.shape[0]
    lane_tiles = D_MODEL // LANES
    grp = SUBLANES * t_dec
    n_ls = B_WIDTH // LANES
    tiles = [(pl.ds(t * n_seq + s0, ROW_TILE), pl.ds(s0, ROW_TILE), t)
             for t in range(t_dec) for s0 in range(0, n_seq, ROW_TILE)]

    @pl.when(g == 0)
    def _():
        lb_ref[...] = _forget_lower_bound(lbl_ref, layer)
        for rows, seq_rows, t in tiles:
            for j in range(lane_tiles):
                start = (seq_rows.start * lane_tiles + j) * t_dec + t
                xt_ref[rows, j * LANES:(j + 1) * LANES] = x_ref[pl.ds(start, ROW_TILE, stride=lane_tiles * t_dec), :]
            h = _rmsnorm(xt_ref[rows, :], normg_ref[...]).astype(_BF16)
            for half in range(B_WIDTH // MXU_N):
                u, gb = _project_b(lambda: h, win_ref, bglu_ref, half)
                for c in range(MXU_N // LANES):
                    ls = half * (MXU_N // LANES) + c
                    ub_ref[ls, rows, :] = u[:, c * LANES:(c + 1) * LANES]
                    gb_ref[ls, rows, :] = gb[:, c * LANES:(c + 1) * LANES]

            def store_a(slab, val, rows=rows):
                pa_ref[slab, rows, :] = val
            for cg in range(2 * A_HEADS):
                _project_a(lambda: h, win_ref, lb_ref, store_a, cg)

    def group(sub):
        seqs = slice(sub * SUBLANES, (sub + 1) * SUBLANES)
        seq0 = pl.multiple_of((g * DEC_GROUPS + sub) * SUBLANES, SUBLANES)
        token_rows = lambda t: pl.ds(t * n_seq + seq0, SUBLANES)

        def token_slab(ref, idx, t):
            return ref[idx, token_rows(t), :]

        u_tok = [jnp.concatenate([token_slab(ub_ref, ls, t) for ls in range(n_ls)], axis=-1) for t in range(t_dec)]
        full = lambda kk: c0_ref[kk, seqs, :] if kk < CONV_WIDTH - 1 else u_tok[kk - (CONV_WIDTH - 1)]
        for kk in range(CONV_WIDTH - 1):
            cs_out_ref[kk, seqs, :] = full(kk + t_dec)
        for t in range(t_dec):
            cv = jnp.broadcast_to(convb_ref[...], (SUBLANES, B_WIDTH))
            for j in range(CONV_WIDTH):
                cv = cv + convw_ref[j:j + 1, :] * full(t + j)
            gate_b = jnp.concatenate([token_slab(gb_ref, ls, t) for ls in range(n_ls)], axis=-1)
            o_b = _group_b_out(cv, lng_ref, lnb_ref, gate_b)
            for ls in range(n_ls):
                ob_ref[ls, token_rows(t), :] = o_b[:, ls * LANES:(ls + 1) * LANES]

        row32 = lax.broadcasted_iota(jnp.int32, (grp, LANES), 0) % SUBLANES
        row64 = lax.broadcasted_iota(jnp.int32, (2 * grp, LANES), 0)
        for hd in range(A_HEADS):
            slabs = lambda arr: [token_slab(pa_ref, arr * A_HEADS + hd, t) for t in range(t_dec)]
            res = []
            for _ in _hgrn_stages(slabs(0), slabs(1), slabs(2), masks_ref, None, False, lambda *a: res.extend(a)):
                pass
            o, qc, kc, v_all, decay = res
            qf, kf, vf = qc.astype(_F32), kc.astype(_F32), v_all.astype(_F32)
            e_hi = decay.astype(_BF16).astype(_F32)
            e_mid = (decay - e_hi).astype(_BF16).astype(_F32)
            e_lo = (decay - e_hi - e_mid).astype(_BF16).astype(_F32)
            lhs = jnp.concatenate([kf, e_hi, e_mid, e_lo, jnp.zeros((SUBLANES, LANES), _F32)], axis=0).astype(_BF16)
            v_pad = jnp.concatenate([vf, jnp.zeros((grp, LANES), _F32)], axis=0)
            o_state = None
            for s in range(SUBLANES):
                s0 = s0_ref[sub * SUBLANES + s, hd]
                term = _dot(jnp.where(row32 == s, qf, 0.0).astype(_BF16), s0.astype(_BF16))
                o_state = term if o_state is None else o_state + term
                mine = row64 % SUBLANES == s
                rhs = jnp.concatenate([
                    jnp.where(mine & (row64 < grp), v_pad, 0.0),
                    jnp.where(mine & (row64 >= grp) & (row64 < grp + 3 * SUBLANES), 1.0, 0.0)], axis=-1).astype(_BF16)
                upd = _dot_tn(lhs, rhs)
                s_out_ref[sub * SUBLANES + s, hd] = s0 * upd[:, HEAD_DIM:] + upd[:, :HEAD_DIM]
            gate = jnp.concatenate([token_slab(pa_ref, 3 * A_HEADS + hd, t) for t in range(t_dec)], axis=0)
            on = _head_out(o + o_state, ag_ref[...], gate)
            for t in range(t_dec):
                oa_ref[hd, token_rows(t), :] = on[t * SUBLANES:(t + 1) * SUBLANES, :]

    for sub in range(DEC_GROUPS):
        group(sub)

    per_step = DEC_GROUPS * SUBLANES
    step_seqs = pl.ds(pl.multiple_of(g * per_step, per_step), per_step)
    step_rows = [pl.ds(pl.multiple_of(t * n_seq + g * per_step, per_step), per_step) for t in range(t_dec)]
    cat = jnp.concatenate(
        [jnp.concatenate([oa_ref[hd, rows, :] for hd in range(A_HEADS)] + [ob_ref[ls, rows, :] for ls in range(n_ls)],
                         axis=-1) for rows in step_rows], axis=0).astype(_BF16)
    out = _dot(cat, wout_ref[...])
    for t, rows in enumerate(step_rows):
        y_ref[step_seqs, t, :] = _rmsnorm(xt_ref[rows, :] + out[t * per_step:(t + 1) * per_step, :], fg_ref[...])


def _decode_call(x, s0, c0, weights, layer):
    n, t_dec, _ = x.shape
    assert n % ROW_TILE == 0 and t_dec == 4
    n_tok = n * t_dec
    x_rows = x.reshape(n, t_dec, D_MODEL // LANES, LANES).transpose(0, 2, 1, 3).reshape(n_tok * D_MODEL // LANES, LANES)
    masks = jnp.asarray(_level_masks((1, 2), t_dec, SUBLANES, same_group=t_dec))
    kern = functools.partial(_decode_kernel, t_dec=t_dec, layer=layer)
    w_specs = _weight_specs(weights, layer)
    per_step = DEC_GROUPS * SUBLANES
    state_spec = pl.BlockSpec((per_step, A_HEADS, HEAD_DIM, HEAD_DIM), lambda i: (i, 0, 0, 0))
    conv_spec = pl.BlockSpec((CONV_WIDTH - 1, per_step, B_WIDTH), lambda i: (0, i, 0))
    y, s_new, c_new = pl.pallas_call(
        kern,
        grid=(n // per_step,),
        in_specs=[_full(x_rows.shape), state_spec, conv_spec] + w_specs + [_full(masks.shape)],
        out_specs=[_full(x.shape), state_spec, conv_spec],
        out_shape=[
            jax.ShapeDtypeStruct(x.shape, _F32),
            jax.ShapeDtypeStruct((n, A_HEADS, HEAD_DIM, HEAD_DIM), _F32),
            jax.ShapeDtypeStruct((CONV_WIDTH - 1, n, B_WIDTH), _F32),
        ],
        scratch_shapes=[
            pltpu.VMEM((4 * A_HEADS, n_tok, LANES), _F32),
            pltpu.VMEM((B_WIDTH // LANES, n_tok, LANES), _F32),
            pltpu.VMEM((B_WIDTH // LANES, n_tok, LANES), _F32),
            pltpu.VMEM((A_HEADS, n_tok, LANES), _F32),
            pltpu.VMEM((B_WIDTH // LANES, n_tok, LANES), _F32),
            pltpu.VMEM((1, A_WIDTH), _F32),
            pltpu.VMEM((n_tok, D_MODEL), _F32),
        ],
        compiler_params=pltpu.CompilerParams(
            dimension_semantics=("arbitrary",),
            vmem_limit_bytes=VMEM_LIMIT_BYTES,
        ),
        name="hymba_decode",
    )(x_rows, s0, c0, *weights, masks)
    return y, s_new, c_new


def _prepare_weights(norm_in_g, w_in, lb_logits, hgrn_norm_g, b_glu, conv_w, conv_b, ln_g, ln_b, w_out, final_norm_g, layer):
    row = lambda a: a.astype(_F32).reshape(1, -1)
    return (
        row(norm_in_g[layer]),
        w_in[layer].astype(_F32),
        lb_logits.astype(_F32),
        row(hgrn_norm_g[layer]),
        row(b_glu[layer]),
        conv_w.astype(_F32),
        row(conv_b[layer]),
        row(ln_g[layer]),
        row(ln_b[layer]),
        w_out[layer].astype(_F32),
        row(final_norm_g),
    )


def kernel(x_prompt, x_sample, state_hgrn, state_conv, norm_in_g, w_in, lb_logits, hgrn_norm_g, b_glu, conv_w,
           conv_b, ln_g, ln_b, w_out, final_norm_g):
    depth = w_in.shape[0]
    assert depth == 1, "single mixer layer: the final norm is fused into the layer kernel"
    weights = _prepare_weights(norm_in_g, w_in, lb_logits, hgrn_norm_g, b_glu, conv_w, conv_b, ln_g, ln_b, w_out,
                               final_norm_g, 0)
    y_p, s_p, c_p, w_in_b, w_out_b = _prompt_call(x_prompt, weights, tb=1024, layer=0)
    weights = weights[:1] + (w_in_b,) + weights[2:9] + (w_out_b,) + weights[10:]
    y_s, s_s, c_s = _decode_call(x_sample, state_hgrn[0], jnp.transpose(state_conv[0], (1, 0, 2)), weights, layer=0)
    return (y_p, y_s, s_p[None], jnp.transpose(c_p, (1, 0, 2))[None], s_s[None], jnp.transpose(c_s, (1, 0, 2))[None])
```

```python
import functools
import math

import numpy as np
import jax
import jax.numpy as jnp
from jax import lax
from jax.experimental import pallas as pl
from jax.experimental.pallas import tpu as pltpu

D_MODEL = 1024
A_HEADS = 4
HEAD_DIM = 128
A_WIDTH = A_HEADS * HEAD_DIM
B_WIDTH = 512
CONV_WIDTH = 31
EPS = 1e-6
IN_WIDTH = 4 * A_WIDTH + 3 * B_WIDTH
GLU_OFF = 4 * A_WIDTH
ZB_OFF = GLU_OFF + 2 * B_WIDTH

LANES = 128
SUBLANES = 8
MXU_N = 256
CHUNK = SUBLANES * SUBLANES
ROW_TILE = 128
UNROLL = 8
CHUNK_UNROLL = 2
CHUNK_LAG = 4
CONV_BUFS = 3
DEC_GROUPS = 2
STATE_BUFS = 3
HIST = 32
FIRST = HIST - (CONV_WIDTH - 1)
VMEM_LIMIT_BYTES = 56 * 1024 * 1024
LOG2E = math.log2(math.e)

_F32 = jnp.float32
_BF16 = jnp.bfloat16


def _sigmoid(x):
    return 1.0 / (1.0 + jnp.exp(-x))


def _silu(x):
    return x * _sigmoid(x)


def _dot(a, b):
    return jnp.dot(a, b, preferred_element_type=_F32)


def _dot_nt(a, b):
    return lax.dot_general(a, b, (((1,), (1,)), ((), ())), preferred_element_type=_F32)


def _dot_tn(a, b):
    return lax.dot_general(a, b, (((0,), (0,)), ((), ())), preferred_element_type=_F32)


def _level_masks(levels, n_res, n_rows, same_group=None):
    r = np.arange(n_res * n_rows)
    t = n_res * (r % n_rows) + r // n_rows
    tt, ss = t[:, None], t[None, :]
    masks = [tt == ss]
    for b in levels:
        masks.append(((tt // b) == (ss // b) + 1) & ((ss // b) % 2 == 0))
    masks = np.stack(masks)
    want = (ss <= tt) if same_group is None else ((ss <= tt) & (tt // same_group == ss // same_group))
    assert (masks.sum(0) == want).all()
    return masks.astype(np.float32)


def _rows(x, r):
    return jnp.broadcast_to(x[r:r + 1, :], x.shape)


def _hgrn_stages(q, f, v, masks_ref, get_st, coarse, done):
    n_res = len(q)
    every = range(n_res)
    k = [1.0 - fi for fi in f]
    lf = [jnp.log(fi) * LOG2E for fi in f]
    g_in = [lf[0]]
    for i in range(1, n_res):
        g_in.append(g_in[-1] + lf[i])
    tot = g_in[-1]
    zero = jnp.zeros_like(tot)

    def cat(parts):
        return jnp.concatenate(parts, axis=0).astype(_BF16)

    def slab_rows(x, i):
        return x[i * SUBLANES:(i + 1) * SUBLANES, :]

    v_all = cat(v)
    s_rows = [None] * n_res
    pending = []

    def fold():
        level, rows, s_l = pending.pop(0)
        for n, i in enumerate(rows):
            term = masks_ref[level, i * SUBLANES:(i + 1) * SUBLANES, :] * slab_rows(s_l, n)
            s_rows[i] = term if s_rows[i] is None else s_rows[i] + term

    def step(rows, qt, kt):
        pending.append((step.level, rows, _dot_nt(cat(qt), cat(kt))))
        step.level += 1
    step.level = 0

    step(every, q, k)
    yield
    b = 1
    while b < n_res:
        odd = [i for i in every if (i // b) % 2 == 1]
        qt, kt = [], []
        for i in every:
            bs = (i // b) * b
            if i in odd:
                qt.append(q[i] * jnp.exp2(g_in[i] - g_in[bs - 1]))
                kt.append(zero)
            else:
                be = bs + b - 1
                kt.append(k[i] * jnp.exp2(g_in[be] - g_in[i]) if i != be else k[i])
        step(odd, qt, kt)
        yield
        fold()
        b *= 2
    qe = [q[i] * jnp.exp2(g_in[i]) for i in every]
    ke = [k[i] * jnp.exp2(tot - g_in[i]) for i in range(n_res - 1)] + [k[-1]]
    if coarse:
        row = lax.broadcasted_iota(jnp.int32, tot.shape, 0)
        pref = tot
        for sh in (1, 2, 4):
            pref = pref + jnp.where(row >= sh, pltpu.roll(pref, sh, 0), 0.0)
        before = pref - tot
        end_all = _rows(pref, SUBLANES - 1)
        qe_b, ke_b = cat(qe), cat(ke)
        tile = lambda x: jnp.concatenate([x] * n_res, axis=0).astype(_BF16)
        for gsz in (1, 2, 4):
            if gsz == 1:
                qt, kt = qe_b, ke_b
            else:
                if gsz == 2:
                    g_start = jnp.where(row % 2 == 0, before, pltpu.roll(before, 1, 0))
                    g_end = jnp.where(row % 2 == 1, pref, pltpu.roll(pref, SUBLANES - 1, 0))
                else:
                    g_start = jnp.where(row < 4, _rows(before, 0), _rows(before, 4))
                    g_end = jnp.where(row < 4, _rows(pref, 3), _rows(pref, 7))
                qt = qe_b * tile(jnp.exp2(before - g_start))
                kt = ke_b * tile(jnp.exp2(g_end - pref))
            pending.append((step.level, every, _dot_nt(qt, kt)))
            step.level += 1
            yield
            fold()
        qc = qe_b * tile(jnp.exp2(before))
        kc = ke_b * tile(jnp.exp2(end_all - pref))
        o_st = _dot_nt(qc, get_st().astype(_BF16))
        decay = jnp.exp2(pref[SUBLANES - 1:SUBLANES, :])
        yield
    else:
        qc, kc = cat(qe), cat(ke)
        o_st = None
        decay = jnp.exp2(tot)
    while pending:
        fold()
    o = _dot(cat(s_rows), v_all)
    yield
    done(o if o_st is None else o + o_st, qc, kc, v_all, decay)


def _round_robin(starts):
    waiting = list(starts)
    live = []
    rnd = 0
    while waiting or live:
        if waiting and rnd % CHUNK_LAG == 0:
            live += waiting.pop(0)
        for g in list(live):
            try:
                next(g)
            except StopIteration:
                live.remove(g)
        rnd += 1


def _aligned(x, m):
    return x if isinstance(x, int) else pl.multiple_of(x, m)


def _loop(trips, body):
    if trips == 1:
        body(0, 0)
    else:
        lax.fori_loop(0, trips, body, 0)


def _head_out(o, ag, gate):
    ms = jnp.mean(o * o, axis=-1, keepdims=True)
    return o * lax.rsqrt(ms + EPS) * ag * gate


def _forget_lower_bound(logits_ref, layer):
    rows = [logits_ref[i:i + 1, :] for i in range(logits_ref.shape[0])]
    top = functools.reduce(jnp.maximum, rows)
    e = [jnp.exp(r - top) for r in rows]
    total = functools.reduce(jnp.add, e)
    return functools.reduce(jnp.add, [v / total for v in e[:layer + 1]])


def _project_a(h, win_ref, lb_ref, store, cg):
    p = _dot(h(), win_ref[:, cg * MXU_N:(cg + 1) * MXU_N])
    arr = cg // 2
    if arr == 0 or arr == 3:
        p = _silu(p)
    elif arr == 1:
        lb = lb_ref[:, (cg % 2) * MXU_N:(cg % 2 + 1) * MXU_N]
        p = lb + (1.0 - lb) * _sigmoid(p)
    store(2 * cg, p[:, :LANES])
    store(2 * cg + 1, p[:, LANES:])


def _project_b(h, win_ref, bglu_ref, half):
    c0 = half * MXU_N
    glu_a = _dot(h(), win_ref[:, GLU_OFF + c0:GLU_OFF + c0 + MXU_N]) + bglu_ref[:, c0:c0 + MXU_N]
    glu_b = (_dot(h(), win_ref[:, GLU_OFF + B_WIDTH + c0:GLU_OFF + B_WIDTH + c0 + MXU_N])
             + bglu_ref[:, B_WIDTH + c0:B_WIDTH + c0 + MXU_N])
    zb = _dot(h(), win_ref[:, ZB_OFF + c0:ZB_OFF + c0 + MXU_N])
    return glu_a * _sigmoid(glu_b), _silu(zb)


def _rmsnorm(x, g):
    return x * lax.rsqrt(jnp.mean(x * x, axis=-1, keepdims=True) + EPS) * g


def _group_b_out(cv, lng_ref, lnb_ref, gate_b):
    mu = jnp.mean(cv, axis=-1, keepdims=True)
    d = cv - mu
    var = jnp.mean(d * d, axis=-1, keepdims=True)
    ln = d * lax.rsqrt(var + EPS) * lng_ref[...] + lnb_ref[...]
    return _silu(ln) * gate_b


def _cast_weight(src_hbm, dst_ref, slab_refs, sems):
    n = dst_ref.shape[1] // LANES
    slots = len(slab_refs)

    def copy(j):
        return pltpu.make_async_copy(src_hbm.at[:, pl.ds(j * LANES, LANES)], slab_refs[j % slots], sems.at[j % slots])

    for j in range(min(slots, n)):
        copy(j).start()
    for j in range(n):
        copy(j).wait()
        dst_ref[:, j * LANES:(j + 1) * LANES] = slab_refs[j % slots][...].astype(_BF16)
        if j + slots < n:
            copy(j + slots).start()


def _conv_stages(ua_ref, ub_ref, wb_ref, convb_ref, lng_ref, lnb_ref, gb_ref, cv_ref, cat_ref, base, n):
    reps = lambda w, rows: jnp.concatenate([w] * (rows // (2 * SUBLANES)), axis=0)
    for ls in range(B_WIDTH // LANES):
        lanes = slice(ls * LANES, (ls + 1) * LANES)
        acc = jnp.broadcast_to(convb_ref[:, lanes], (n, LANES))
        for off in range(SUBLANES):
            taps = [j for j in range(CONV_WIDTH) if (FIRST + j) % SUBLANES == off]
            rows = n + (2 * SUBLANES if off else 0)
            part = None
            for j in taps:
                rel = FIRST + j - off
                if rel % (2 * SUBLANES) == 0:
                    win = ua_ref[pl.ds(_aligned(base + rel, 2 * SUBLANES), rows), lanes]
                else:
                    win = ub_ref[pl.ds(_aligned(base + rel + SUBLANES, 2 * SUBLANES), rows), lanes]
                term = reps(wb_ref[j, :, lanes], rows) * win
                part = term if part is None else part + term
            part = part.astype(_F32)
            acc = acc + (part[off:off + n, :] if off else part)
            if off == SUBLANES // 2 - 1:
                yield
        cv_ref[:, lanes] = acc
        yield
    rows = pl.ds(_aligned(base, n), n)
    o_b = _group_b_out(cv_ref[...], lng_ref, lnb_ref, gb_ref[rows, :])
    cat_ref[rows, A_WIDTH:] = o_b.astype(_BF16)


def _prompt_kernel(x_ref, normg_ref, win_hbm, lbl_ref, ag_ref, bglu_ref, convw_ref, convb_ref,
                   lng_ref, lnb_ref, wout_hbm, fg_ref, masks_ref,
                   y_ref, st_out_ref, cs_out_ref, win_out_hbm, wout_out_hbm,
                   pa_ref, uh_ref, ua_ref, ub_ref, gb_ref, cv_ref, oa_ref, cat_ref, st_ref,
                   win_ref, wout_ref, wout_f32_ref, lb_ref, wb_ref, sems, *, tb, layer):
    seq = pl.program_id(0)
    t = pl.program_id(1)
    nt = pl.num_programs(1)

    first = (seq == 0) & (t == 0)
    n_slabs = pa_ref.shape[0]
    handoff = [pltpu.make_async_copy(win_ref, win_out_hbm, sems.at[n_slabs]),
               pltpu.make_async_copy(wout_ref, wout_out_hbm, sems.at[n_slabs + 1])]
    wout_fetch = pltpu.make_async_copy(wout_hbm, wout_f32_ref, sems.at[n_slabs + 2])

    @pl.when(first)
    def _():
        _cast_weight(win_hbm, win_ref, [pa_ref.at[s, pl.ds(0, win_ref.shape[0])] for s in range(n_slabs)], sems)
        wout_fetch.start()
        handoff[0].start()
        lb_ref[...] = _forget_lower_bound(lbl_ref, layer)
        for j in range(CONV_WIDTH):
            wb_ref[j] = jnp.broadcast_to(convw_ref[j:j + 1, :], (2 * SUBLANES, B_WIDTH)).astype(_BF16)
        cs_out_ref[...] = jnp.zeros_like(cs_out_ref)

    @pl.when(t == 0)
    def _():
        st_ref[...] = jnp.zeros_like(st_ref)
        uh_ref[...] = jnp.zeros_like(uh_ref)
        ua_ref[...] = jnp.zeros_like(ua_ref)
        ub_ref[...] = jnp.zeros_like(ub_ref)

    def front_tile(r0):
        rows = pl.ds(r0, ROW_TILE)
        h = _rmsnorm(x_ref[rows, :], normg_ref[...]).astype(_BF16)
        for half in range(B_WIDTH // MXU_N):
            u, gb = _project_b(lambda: h, win_ref, bglu_ref, half)
            cols = slice(half * MXU_N, (half + 1) * MXU_N)
            here = pl.ds(_aligned(HIST + r0, HIST), ROW_TILE)
            prev = uh_ref[pl.ds(_aligned(HIST + r0 - SUBLANES, SUBLANES), SUBLANES), cols]
            uh_ref[here, cols] = u
            ua_ref[here, cols] = u.astype(_BF16)
            ub_ref[here, cols] = jnp.concatenate([prev, u[:ROW_TILE - SUBLANES, :]], axis=0).astype(_BF16)
            gb_ref[rows, cols] = gb

        def store_a(slab, val):
            pa_ref[slab, rows, :] = val
        for cg in range(2 * A_HEADS):
            _project_a(lambda: h, win_ref, lb_ref, store_a, cg)

    def front_body(p, carry):
        for n in range(UNROLL):
            front_tile(_aligned((UNROLL * p + n) * ROW_TILE, ROW_TILE))
        return carry

    _loop(tb // (UNROLL * ROW_TILE), front_body)
    ub_ref[HIST + tb:HIST + tb + 2 * SUBLANES, :] = jnp.concatenate(
        [uh_ref[HIST + tb - SUBLANES:HIST + tb, :], jnp.zeros((SUBLANES, B_WIDTH), _F32)], axis=0).astype(_BF16)

    def chunk(base, slot):
        def head(hd):
            def slabs(arr):
                return [pa_ref[arr * A_HEADS + hd, pl.ds(base + i, SUBLANES, stride=SUBLANES), :]
                        for i in range(SUBLANES)]

            def done(o, qc, kc, v_all, decay):
                st_ref[hd] = st_ref[hd] * decay + _dot_tn(v_all, kc)
                for i in range(SUBLANES):
                    oa_ref[hd, pl.ds(base + i, SUBLANES, stride=SUBLANES), :] = o[i * SUBLANES:(i + 1) * SUBLANES, :]
            return _hgrn_stages(slabs(0), slabs(1), slabs(2), masks_ref, lambda: st_ref[hd], True, done)
        conv = _conv_stages(ua_ref, ub_ref, wb_ref, convb_ref, lng_ref, lnb_ref, gb_ref, cv_ref.at[slot], cat_ref, base, CHUNK)
        return [head(hd) for hd in range(A_HEADS)] + [conv]

    def chunk_body(p, carry):
        _round_robin([chunk(_aligned((CHUNK_UNROLL * p + n) * CHUNK, CHUNK), n % CONV_BUFS) for n in range(CHUNK_UNROLL)])
        return carry

    _loop(tb // (CHUNK_UNROLL * CHUNK), chunk_body)

    @pl.when(t == nt - 1)
    def _():
        grp = pl.ds(pl.multiple_of((seq // SUBLANES) * SUBLANES, SUBLANES), SUBLANES)
        mine = lax.broadcasted_iota(jnp.int32, (SUBLANES, B_WIDTH), 0) == seq % SUBLANES
        for kk in range(CONV_WIDTH - 1):
            row = uh_ref[HIST + tb - (CONV_WIDTH - 1) + kk:HIST + tb - (CONV_WIDTH - 1) + kk + 1, :]
            cs_out_ref[kk, grp, :] = jnp.where(mine, jnp.broadcast_to(row, (SUBLANES, B_WIDTH)), cs_out_ref[kk, grp, :])
        for hd in range(A_HEADS):
            st_out_ref[hd] = st_ref[hd].T

    uh_ref[0:HIST, :] = uh_ref[tb:tb + HIST, :]
    ua_ref[0:HIST, :] = ua_ref[tb:tb + HIST, :]
    ub_ref[0:HIST, :] = ub_ref[tb:tb + HIST, :]

    @pl.when(first)
    def _():
        wout_fetch.wait()
        for r0 in range(0, wout_ref.shape[0], ROW_TILE):
            wout_ref[r0:r0 + ROW_TILE, :] = wout_f32_ref[r0:r0 + ROW_TILE, :].astype(_BF16)
        handoff[1].start()

    def back_body(p, carry):
        tiles = [pl.ds(_aligned((UNROLL * p + n) * ROW_TILE, ROW_TILE), ROW_TILE) for n in range(UNROLL)]
        for rows in tiles:
            for hd in range(A_HEADS):
                on = _head_out(oa_ref[hd, rows, :], ag_ref[...], pa_ref[3 * A_HEADS + hd, rows, :])
                cat_ref[rows, hd * HEAD_DIM:(hd + 1) * HEAD_DIM] = on.astype(_BF16)
        outs = [_dot(cat_ref[rows, :], wout_ref[...]) for rows in tiles]
        for rows, out in zip(tiles, outs):
            y_ref[rows, :] = _rmsnorm(x_ref[rows, :] + out, fg_ref[...])
        return carry

    _loop(tb // (UNROLL * ROW_TILE), back_body)

    @pl.when(first)
    def _():
        for copy in handoff:
            copy.wait()


def _weight_specs(weights, layer, in_hbm=()):
    specs = [_full(w.shape) for w in weights]
    specs[5] = pl.BlockSpec((None,) + weights[5].shape[1:], lambda *_: (layer, 0, 0), pipeline_mode=pl.Buffered(1))
    for i in in_hbm:
        specs[i] = pl.BlockSpec(memory_space=pl.ANY)
    return specs


def _full(shape):
    nd = len(shape)
    return pl.BlockSpec(shape, lambda *_: (0,) * nd, pipeline_mode=pl.Buffered(1))


def _prompt_call(x, weights, tb, layer):
    n, t, _ = x.shape
    assert t % tb == 0 and tb % (UNROLL * ROW_TILE) == 0 and tb % (CHUNK_UNROLL * CHUNK) == 0
    masks = jnp.asarray(_level_masks((1, 2, 4, 8, 16, 32), SUBLANES, SUBLANES))
    assert n % SUBLANES == 0
    kern = functools.partial(_prompt_kernel, tb=tb, layer=layer)
    w_in, w_out = weights[1], weights[9]
    assert w_in.dtype == _F32 and w_out.dtype == _F32 and w_in.shape[0] <= tb and w_in.shape[1] % LANES == 0
    in_hbm = pl.BlockSpec(memory_space=pl.ANY)
    w_specs = _weight_specs(weights, layer, in_hbm=(1, 9))
    return pl.pallas_call(
        kern,
        grid=(n, t // tb),
        in_specs=[pl.BlockSpec((None, tb, D_MODEL), lambda i, j: (i, j, 0))] + w_specs + [_full(masks.shape)],
        out_specs=[
            pl.BlockSpec((None, tb, D_MODEL), lambda i, j: (i, j, 0)),
            pl.BlockSpec((None, A_HEADS, HEAD_DIM, HEAD_DIM), lambda i, j: (i, 0, 0, 0)),
            _full((CONV_WIDTH - 1, n, B_WIDTH)),
            in_hbm,
            in_hbm,
        ],
        out_shape=[
            jax.ShapeDtypeStruct((n, t, D_MODEL), _F32),
            jax.ShapeDtypeStruct((n, A_HEADS, HEAD_DIM, HEAD_DIM), _F32),
            jax.ShapeDtypeStruct((CONV_WIDTH - 1, n, B_WIDTH), _F32),
            jax.ShapeDtypeStruct(w_in.shape, _BF16),
            jax.ShapeDtypeStruct(w_out.shape, _BF16),
        ],
        scratch_shapes=[
            pltpu.VMEM((4 * A_HEADS, tb, LANES), _F32),
            pltpu.VMEM((HIST + tb, B_WIDTH), _F32),
            pltpu.VMEM((HIST + tb + 2 * SUBLANES, B_WIDTH), _BF16),
            pltpu.VMEM((HIST + tb + 2 * SUBLANES, B_WIDTH), _BF16),
            pltpu.VMEM((tb, B_WIDTH), _F32),
            pltpu.VMEM((CONV_BUFS, CHUNK, B_WIDTH), _F32),
            pltpu.VMEM((A_HEADS, tb, LANES), _F32),
            pltpu.VMEM((tb, A_WIDTH + B_WIDTH), _BF16),
            pltpu.VMEM((A_HEADS, HEAD_DIM, HEAD_DIM), _F32),
            pltpu.VMEM(w_in.shape, _BF16),
            pltpu.VMEM(w_out.shape, _BF16),
            pltpu.VMEM(w_out.shape, _F32),
            pltpu.VMEM((1, A_WIDTH), _F32),
            pltpu.VMEM((CONV_WIDTH, 2 * SUBLANES, B_WIDTH), _BF16),
            pltpu.SemaphoreType.DMA((4 * A_HEADS + 3,)),
        ],
        compiler_params=pltpu.CompilerParams(
            dimension_semantics=("arbitrary", "arbitrary"),
            vmem_limit_bytes=VMEM_LIMIT_BYTES,
        ),
        name="hymba_prompt",
    )(x, *weights, masks)


def _decode_kernel(x_ref, s0_hbm, c0_ref, normg_ref, win_ref, lbl_ref, ag_ref, bglu_ref, convw_ref, convb_ref,
                   lng_ref, lnb_ref, wout_ref, fg_ref, masks_ref,
                   y_ref, s_out_ref, cs_out_ref,
                   pa_ref, ub_ref, gb_ref, oa_ref, ob_ref, lb_ref, xt_ref, s0_ref, sems, *, t_dec, layer, steps):
    g = pl.program_id(0)
    per_step = DEC_GROUPS * SUBLANES
    slot = g % STATE_BUFS

    def fetch(step, slot):
        return pltpu.make_async_copy(s0_hbm.at[pl.ds(step * per_step, per_step)], s0_ref.at[slot], sems.at[slot])
    n_seq = y_ref.shape[0]
    lane_tiles = D_MODEL // LANES
    grp = SUBLANES * t_dec
    n_ls = B_WIDTH // LANES
    tiles = [(pl.ds(t * n_seq + s0, ROW_TILE), pl.ds(s0, ROW_TILE), t)
             for t in range(t_dec) for s0 in range(0, n_seq, ROW_TILE)]

    @pl.when(g == 0)
    def _():
        for k in range(min(STATE_BUFS, steps)):
            fetch(k, k).start()
        lb_ref[...] = _forget_lower_bound(lbl_ref, layer)
        for rows, seq_rows, t in tiles:
            for j in range(lane_tiles):
                start = (seq_rows.start * lane_tiles + j) * t_dec + t
                xt_ref[rows, j * LANES:(j + 1) * LANES] = x_ref[pl.ds(start, ROW_TILE, stride=lane_tiles * t_dec), :]
            h = _rmsnorm(xt_ref[rows, :], normg_ref[...]).astype(_BF16)
            for half in range(B_WIDTH // MXU_N):
                u, gb = _project_b(lambda: h, win_ref, bglu_ref, half)
                for c in range(MXU_N // LANES):
                    ls = half * (MXU_N // LANES) + c
                    ub_ref[ls, rows, :] = u[:, c * LANES:(c + 1) * LANES]
                    gb_ref[ls, rows, :] = gb[:, c * LANES:(c + 1) * LANES]

            def store_a(slab, val, rows=rows):
                pa_ref[slab, rows, :] = val
            for cg in range(2 * A_HEADS):
                _project_a(lambda: h, win_ref, lb_ref, store_a, cg)

    fetch(g, slot).wait()

    def group(sub):
        seqs = slice(sub * SUBLANES, (sub + 1) * SUBLANES)
        seq0 = pl.multiple_of((g * DEC_GROUPS + sub) * SUBLANES, SUBLANES)
        token_rows = lambda t: pl.ds(t * n_seq + seq0, SUBLANES)

        def token_slab(ref, idx, t):
            return ref[idx, token_rows(t), :]

        u_tok = [jnp.concatenate([token_slab(ub_ref, ls, t) for ls in range(n_ls)], axis=-1) for t in range(t_dec)]
        full = lambda kk: c0_ref[kk, seqs, :] if kk < CONV_WIDTH - 1 else u_tok[kk - (CONV_WIDTH - 1)]
        for kk in range(CONV_WIDTH - 1):
            cs_out_ref[kk, seqs, :] = full(kk + t_dec)
        for t in range(t_dec):
            cv = jnp.broadcast_to(convb_ref[...], (SUBLANES, B_WIDTH))
            for j in range(CONV_WIDTH):
                cv = cv + convw_ref[j:j + 1, :] * full(t + j)
            gate_b = jnp.concatenate([token_slab(gb_ref, ls, t) for ls in range(n_ls)], axis=-1)
            o_b = _group_b_out(cv, lng_ref, lnb_ref, gate_b)
            for ls in range(n_ls):
                ob_ref[ls, token_rows(t), :] = o_b[:, ls * LANES:(ls + 1) * LANES]

        row32 = lax.broadcasted_iota(jnp.int32, (grp, LANES), 0) % SUBLANES
        row64 = lax.broadcasted_iota(jnp.int32, (2 * grp, LANES), 0)
        for hd in range(A_HEADS):
            slabs = lambda arr: [token_slab(pa_ref, arr * A_HEADS + hd, t) for t in range(t_dec)]
            res = []
            for _ in _hgrn_stages(slabs(0), slabs(1), slabs(2), masks_ref, None, False, lambda *a: res.extend(a)):
                pass
            o, qc, kc, v_all, decay = res
            qf, kf, vf = qc.astype(_F32), kc.astype(_F32), v_all.astype(_F32)
            e_hi = decay.astype(_BF16).astype(_F32)
            e_mid = (decay - e_hi).astype(_BF16).astype(_F32)
            e_lo = (decay - e_hi - e_mid).astype(_BF16).astype(_F32)
            lhs = jnp.concatenate([kf, e_hi, e_mid, e_lo, jnp.zeros((SUBLANES, LANES), _F32)], axis=0).astype(_BF16)
            v_pad = jnp.concatenate([vf, jnp.zeros((grp, LANES), _F32)], axis=0)
            o_state = None
            for s in range(SUBLANES):
                s0 = s0_ref[slot, sub * SUBLANES + s, hd]
                term = _dot(jnp.where(row32 == s, qf, 0.0).astype(_BF16), s0.astype(_BF16))
                o_state = term if o_state is None else o_state + term
                mine = row64 % SUBLANES == s
                rhs = jnp.concatenate([
                    jnp.where(mine & (row64 < grp), v_pad, 0.0),
                    jnp.where(mine & (row64 >= grp) & (row64 < grp + 3 * SUBLANES), 1.0, 0.0)], axis=-1).astype(_BF16)
                upd = _dot_tn(lhs, rhs)
                s_out_ref[sub * SUBLANES + s, hd] = s0 * upd[:, HEAD_DIM:] + upd[:, :HEAD_DIM]
            gate = jnp.concatenate([token_slab(pa_ref, 3 * A_HEADS + hd, t) for t in range(t_dec)], axis=0)
            on = _head_out(o + o_state, ag_ref[...], gate)
            for t in range(t_dec):
                oa_ref[hd, token_rows(t), :] = on[t * SUBLANES:(t + 1) * SUBLANES, :]

    for sub in range(DEC_GROUPS):
        group(sub)

    @pl.when(g + STATE_BUFS < steps)
    def _():
        fetch(g + STATE_BUFS, slot).start()

    step_seqs = pl.ds(pl.multiple_of(g * per_step, per_step), per_step)
    step_rows = [pl.ds(pl.multiple_of(t * n_seq + g * per_step, per_step), per_step) for t in range(t_dec)]
    cat = jnp.concatenate(
        [jnp.concatenate([oa_ref[hd, rows, :] for hd in range(A_HEADS)] + [ob_ref[ls, rows, :] for ls in range(n_ls)],
                         axis=-1) for rows in step_rows], axis=0).astype(_BF16)
    out = _dot(cat, wout_ref[...])
    for t, rows in enumerate(step_rows):
        y_ref[step_seqs, t, :] = _rmsnorm(xt_ref[rows, :] + out[t * per_step:(t + 1) * per_step, :], fg_ref[...])


def _decode_call(x, s0, c0, weights, layer):
    n, t_dec, _ = x.shape
    assert n % ROW_TILE == 0 and t_dec == 4
    n_tok = n * t_dec
    x_rows = x.reshape(n, t_dec, D_MODEL // LANES, LANES).transpose(0, 2, 1, 3).reshape(n_tok * D_MODEL // LANES, LANES)
    masks = jnp.asarray(_level_masks((1, 2), t_dec, SUBLANES, same_group=t_dec))
    per_step = DEC_GROUPS * SUBLANES
    kern = functools.partial(_decode_kernel, t_dec=t_dec, layer=layer, steps=n // per_step)
    w_specs = _weight_specs(weights, layer)
    state_spec = pl.BlockSpec((per_step, A_HEADS, HEAD_DIM, HEAD_DIM), lambda i: (i, 0, 0, 0))
    conv_spec = pl.BlockSpec((CONV_WIDTH - 1, per_step, B_WIDTH), lambda i: (0, i, 0))
    y, s_new, c_new = pl.pallas_call(
        kern,
        grid=(n // per_step,),
        in_specs=[_full(x_rows.shape), pl.BlockSpec(memory_space=pl.ANY), conv_spec] + w_specs + [_full(masks.shape)],
        out_specs=[_full(x.shape), state_spec, conv_spec],
        out_shape=[
            jax.ShapeDtypeStruct(x.shape, _F32),
            jax.ShapeDtypeStruct((n, A_HEADS, HEAD_DIM, HEAD_DIM), _F32),
            jax.ShapeDtypeStruct((CONV_WIDTH - 1, n, B_WIDTH), _F32),
        ],
        scratch_shapes=[
            pltpu.VMEM((4 * A_HEADS, n_tok, LANES), _F32),
            pltpu.VMEM((B_WIDTH // LANES, n_tok, LANES), _F32),
            pltpu.VMEM((B_WIDTH // LANES, n_tok, LANES), _F32),
            pltpu.VMEM((A_HEADS, n_tok, LANES), _F32),
            pltpu.VMEM((B_WIDTH // LANES, n_tok, LANES), _F32),
            pltpu.VMEM((1, A_WIDTH), _F32),
            pltpu.VMEM((n_tok, D_MODEL), _F32),
            pltpu.VMEM((STATE_BUFS, per_step, A_HEADS, HEAD_DIM, HEAD_DIM), _F32),
            pltpu.SemaphoreType.DMA((STATE_BUFS,)),
        ],
        compiler_params=pltpu.CompilerParams(
            dimension_semantics=("arbitrary",),
            vmem_limit_bytes=VMEM_LIMIT_BYTES,
        ),
        name="hymba_decode",
    )(x_rows, s0, c0, *weights, masks)
    return y, s_new, c_new


def _prepare_weights(norm_in_g, w_in, lb_logits, hgrn_norm_g, b_glu, conv_w, conv_b, ln_g, ln_b, w_out, final_norm_g, layer):
    row = lambda a: a.astype(_F32).reshape(1, -1)
    return (
        row(norm_in_g[layer]),
        w_in[layer].astype(_F32),
        lb_logits.astype(_F32),
        row(hgrn_norm_g[layer]),
        row(b_glu[layer]),
        conv_w.astype(_F32),
        row(conv_b[layer]),
        row(ln_g[layer]),
        row(ln_b[layer]),
        w_out[layer].astype(_F32),
        row(final_norm_g),
    )


def kernel(x_prompt, x_sample, state_hgrn, state_conv, norm_in_g, w_in, lb_logits, hgrn_norm_g, b_glu, conv_w,
           conv_b, ln_g, ln_b, w_out, final_norm_g):
    depth = w_in.shape[0]
    assert depth == 1, "single mixer layer: the final norm is fused into the layer kernel"
    weights = _prepare_weights(norm_in_g, w_in, lb_logits, hgrn_norm_g, b_glu, conv_w, conv_b, ln_g, ln_b, w_out,
                               final_norm_g, 0)
    y_p, s_p, c_p, w_in_b, w_out_b = _prompt_call(x_prompt, weights, tb=1024, layer=0)
    weights = weights[:1] + (w_in_b,) + weights[2:9] + (w_out_b,) + weights[10:]
    y_s, s_s, c_s = _decode_call(x_sample, state_hgrn[0], jnp.transpose(state_conv[0], (1, 0, 2)), weights, layer=0)
    return (y_p, y_s, s_p[None], jnp.transpose(c_p, (1, 0, 2))[None], s_s[None], jnp.transpose(c_s, (1, 0, 2))[None])
```

```python
import functools
import math

import numpy as np
import jax
import jax.numpy as jnp
from jax import lax
from jax.experimental import pallas as pl
from jax.experimental.pallas import tpu as pltpu

D_MODEL = 1024
A_HEADS = 4
HEAD_DIM = 128
A_WIDTH = A_HEADS * HEAD_DIM
B_WIDTH = 512
CONV_WIDTH = 31
EPS = 1e-6
IN_WIDTH = 4 * A_WIDTH + 3 * B_WIDTH
GLU_OFF = 4 * A_WIDTH
ZB_OFF = GLU_OFF + 2 * B_WIDTH

LANES = 128
SUBLANES = 8
MXU_N = 256
CHUNK = SUBLANES * SUBLANES
ROW_TILE = 128
UNROLL = 8
CHUNK_UNROLL = 2
CHUNK_LAG = 4
CONV_BUFS = 3
DEC_GROUPS = 2
STATE_BUFS = 4
HIST = 32
FIRST = HIST - (CONV_WIDTH - 1)
VMEM_LIMIT_BYTES = 56 * 1024 * 1024
LOG2E = math.log2(math.e)

_F32 = jnp.float32
_BF16 = jnp.bfloat16


def _sigmoid(x):
    return 1.0 / (1.0 + jnp.exp(-x))


def _silu(x):
    return x * _sigmoid(x)


def _dot(a, b):
    return jnp.dot(a, b, preferred_element_type=_F32)


def _dot_nt(a, b):
    return lax.dot_general(a, b, (((1,), (1,)), ((), ())), preferred_element_type=_F32)


def _dot_tn(a, b):
    return lax.dot_general(a, b, (((0,), (0,)), ((), ())), preferred_element_type=_F32)


def _level_masks(levels, n_res, n_rows, same_group=None):
    r = np.arange(n_res * n_rows)
    t = n_res * (r % n_rows) + r // n_rows
    tt, ss = t[:, None], t[None, :]
    masks = [tt == ss]
    for b in levels:
        masks.append(((tt // b) == (ss // b) + 1) & ((ss // b) % 2 == 0))
    masks = np.stack(masks)
    want = (ss <= tt) if same_group is None else ((ss <= tt) & (tt // same_group == ss // same_group))
    assert (masks.sum(0) == want).all()
    return masks.astype(np.float32)


def _rows(x, r):
    return jnp.broadcast_to(x[r:r + 1, :], x.shape)


def _hgrn_stages(q, f, v, masks_ref, get_st, coarse, done):
    n_res = len(q)
    every = range(n_res)
    k = [1.0 - fi for fi in f]
    lf = [jnp.log(fi) * LOG2E for fi in f]
    g_in = [lf[0]]
    for i in range(1, n_res):
        g_in.append(g_in[-1] + lf[i])
    tot = g_in[-1]
    zero = jnp.zeros_like(tot)

    def cat(parts):
        return jnp.concatenate(parts, axis=0).astype(_BF16)

    def slab_rows(x, i):
        return x[i * SUBLANES:(i + 1) * SUBLANES, :]

    v_all = cat(v)
    s_rows = [None] * n_res
    pending = []

    def fold():
        level, rows, s_l = pending.pop(0)
        for n, i in enumerate(rows):
            term = masks_ref[level, i * SUBLANES:(i + 1) * SUBLANES, :] * slab_rows(s_l, n)
            s_rows[i] = term if s_rows[i] is None else s_rows[i] + term

    def step(rows, qt, kt):
        pending.append((step.level, rows, _dot_nt(cat(qt), cat(kt))))
        step.level += 1
    step.level = 0

    step(every, q, k)
    yield
    b = 1
    while b < n_res:
        odd = [i for i in every if (i // b) % 2 == 1]
        qt, kt = [], []
        for i in every:
            bs = (i // b) * b
            if i in odd:
                qt.append(q[i] * jnp.exp2(g_in[i] - g_in[bs - 1]))
                kt.append(zero)
            else:
                be = bs + b - 1
                kt.append(k[i] * jnp.exp2(g_in[be] - g_in[i]) if i != be else k[i])
        step(odd, qt, kt)
        yield
        fold()
        b *= 2
    qe = [q[i] * jnp.exp2(g_in[i]) for i in every]
    ke = [k[i] * jnp.exp2(tot - g_in[i]) for i in range(n_res - 1)] + [k[-1]]
    if coarse:
        row = lax.broadcasted_iota(jnp.int32, tot.shape, 0)
        pref = tot
        for sh in (1, 2, 4):
            pref = pref + jnp.where(row >= sh, pltpu.roll(pref, sh, 0), 0.0)
        before = pref - tot
        end_all = _rows(pref, SUBLANES - 1)
        qe_b, ke_b = cat(qe), cat(ke)
        tile = lambda x: jnp.concatenate([x] * n_res, axis=0).astype(_BF16)
        for gsz in (1, 2, 4):
            if gsz == 1:
                qt, kt = qe_b, ke_b
            else:
                if gsz == 2:
                    g_start = jnp.where(row % 2 == 0, before, pltpu.roll(before, 1, 0))
                    g_end = jnp.where(row % 2 == 1, pref, pltpu.roll(pref, SUBLANES - 1, 0))
                else:
                    g_start = jnp.where(row < 4, _rows(before, 0), _rows(before, 4))
                    g_end = jnp.where(row < 4, _rows(pref, 3), _rows(pref, 7))
                qt = qe_b * tile(jnp.exp2(before - g_start))
                kt = ke_b * tile(jnp.exp2(g_end - pref))
            pending.append((step.level, every, _dot_nt(qt, kt)))
            step.level += 1
            yield
            fold()
        qc = qe_b * tile(jnp.exp2(before))
        kc = ke_b * tile(jnp.exp2(end_all - pref))
        o_st = _dot_nt(qc, get_st().astype(_BF16))
        decay = jnp.exp2(pref[SUBLANES - 1:SUBLANES, :])
        yield
    else:
        qc, kc = cat(qe), cat(ke)
        o_st = None
        decay = jnp.exp2(tot)
    while pending:
        fold()
    o = _dot(cat(s_rows), v_all)
    yield
    done(o if o_st is None else o + o_st, qc, kc, v_all, decay)


def _round_robin(starts):
    waiting = list(starts)
    live = []
    rnd = 0
    while waiting or live:
        if waiting and rnd % CHUNK_LAG == 0:
            live += waiting.pop(0)
        for g in list(live):
            try:
                next(g)
            except StopIteration:
                live.remove(g)
        rnd += 1


def _aligned(x, m):
    return x if isinstance(x, int) else pl.multiple_of(x, m)


def _loop(trips, body):
    if trips == 1:
        body(0, 0)
    else:
        lax.fori_loop(0, trips, body, 0)


def _head_out(o, ag, gate):
    ms = jnp.mean(o * o, axis=-1, keepdims=True)
    return o * lax.rsqrt(ms + EPS) * ag * gate


def _forget_lower_bound(logits_ref, layer):
    rows = [logits_ref[i:i + 1, :] for i in range(logits_ref.shape[0])]
    top = functools.reduce(jnp.maximum, rows)
    e = [jnp.exp(r - top) for r in rows]
    total = functools.reduce(jnp.add, e)
    return functools.reduce(jnp.add, [v / total for v in e[:layer + 1]])


def _project_a(h, win_ref, lb_ref, store, cg):
    p = _dot(h(), win_ref[:, cg * MXU_N:(cg + 1) * MXU_N])
    arr = cg // 2
    if arr == 0 or arr == 3:
        p = _silu(p)
    elif arr == 1:
        lb = lb_ref[:, (cg % 2) * MXU_N:(cg % 2 + 1) * MXU_N]
        p = lb + (1.0 - lb) * _sigmoid(p)
    store(2 * cg, p[:, :LANES])
    store(2 * cg + 1, p[:, LANES:])


def _project_b(h, win_ref, bglu_ref, half):
    c0 = half * MXU_N
    glu_a = _dot(h(), win_ref[:, GLU_OFF + c0:GLU_OFF + c0 + MXU_N]) + bglu_ref[:, c0:c0 + MXU_N]
    glu_b = (_dot(h(), win_ref[:, GLU_OFF + B_WIDTH + c0:GLU_OFF + B_WIDTH + c0 + MXU_N])
             + bglu_ref[:, B_WIDTH + c0:B_WIDTH + c0 + MXU_N])
    zb = _dot(h(), win_ref[:, ZB_OFF + c0:ZB_OFF + c0 + MXU_N])
    return glu_a * _sigmoid(glu_b), _silu(zb)


def _rmsnorm(x, g):
    return x * lax.rsqrt(jnp.mean(x * x, axis=-1, keepdims=True) + EPS) * g


def _group_b_out(cv, lng_ref, lnb_ref, gate_b):
    mu = jnp.mean(cv, axis=-1, keepdims=True)
    d = cv - mu
    var = jnp.mean(d * d, axis=-1, keepdims=True)
    ln = d * lax.rsqrt(var + EPS) * lng_ref[...] + lnb_ref[...]
    return _silu(ln) * gate_b


def _cast_weight(src_hbm, dst_ref, slab_refs, sems):
    n = dst_ref.shape[1] // LANES
    slots = len(slab_refs)

    def copy(j):
        return pltpu.make_async_copy(src_hbm.at[:, pl.ds(j * LANES, LANES)], slab_refs[j % slots], sems.at[j % slots])

    for j in range(min(slots, n)):
        copy(j).start()
    for j in range(n):
        copy(j).wait()
        dst_ref[:, j * LANES:(j + 1) * LANES] = slab_refs[j % slots][...].astype(_BF16)
        if j + slots < n:
            copy(j + slots).start()


def _conv_stages(ua_ref, ub_ref, wb_ref, convb_ref, lng_ref, lnb_ref, gb_ref, cv_ref, cat_ref, base, n):
    reps = lambda w, rows: jnp.concatenate([w] * (rows // (2 * SUBLANES)), axis=0)
    for ls in range(B_WIDTH // LANES):
        lanes = slice(ls * LANES, (ls + 1) * LANES)
        acc = jnp.broadcast_to(convb_ref[:, lanes], (n, LANES))
        for off in range(SUBLANES):
            taps = [j for j in range(CONV_WIDTH) if (FIRST + j) % SUBLANES == off]
            rows = n + (2 * SUBLANES if off else 0)
            part = None
            for j in taps:
                rel = FIRST + j - off
                if rel % (2 * SUBLANES) == 0:
                    win = ua_ref[pl.ds(_aligned(base + rel, 2 * SUBLANES), rows), lanes]
                else:
                    win = ub_ref[pl.ds(_aligned(base + rel + SUBLANES, 2 * SUBLANES), rows), lanes]
                term = reps(wb_ref[j, :, lanes], rows) * win
                part = term if part is None else part + term
            part = part.astype(_F32)
            acc = acc + (part[off:off + n, :] if off else part)
            if off == SUBLANES // 2 - 1:
                yield
        cv_ref[:, lanes] = acc
        yield
    rows = pl.ds(_aligned(base, n), n)
    o_b = _group_b_out(cv_ref[...], lng_ref, lnb_ref, gb_ref[rows, :])
    cat_ref[rows, A_WIDTH:] = o_b.astype(_BF16)


def _prompt_kernel(x_ref, normg_ref, win_hbm, lbl_ref, ag_ref, bglu_ref, convw_ref, convb_ref,
                   lng_ref, lnb_ref, wout_hbm, fg_ref, masks_ref,
                   y_ref, st_out_ref, cs_out_ref, win_out_hbm, wout_out_hbm,
                   pa_ref, uh_ref, ua_ref, ub_ref, gb_ref, cv_ref, oa_ref, cat_ref, st_ref,
                   win_ref, wout_ref, wout_f32_ref, lb_ref, wb_ref, sems, *, tb, layer):
    seq = pl.program_id(0)
    t = pl.program_id(1)
    nt = pl.num_programs(1)

    first = (seq == 0) & (t == 0)
    n_slabs = pa_ref.shape[0]
    handoff = [pltpu.make_async_copy(win_ref, win_out_hbm, sems.at[n_slabs]),
               pltpu.make_async_copy(wout_ref, wout_out_hbm, sems.at[n_slabs + 1])]
    wout_fetch = pltpu.make_async_copy(wout_hbm, wout_f32_ref, sems.at[n_slabs + 2])

    @pl.when(first)
    def _():
        _cast_weight(win_hbm, win_ref, [pa_ref.at[s, pl.ds(0, win_ref.shape[0])] for s in range(n_slabs)], sems)
        wout_fetch.start()
        handoff[0].start()
        lb_ref[...] = _forget_lower_bound(lbl_ref, layer)
        for j in range(CONV_WIDTH):
            wb_ref[j] = jnp.broadcast_to(convw_ref[j:j + 1, :], (2 * SUBLANES, B_WIDTH)).astype(_BF16)
        cs_out_ref[...] = jnp.zeros_like(cs_out_ref)

    @pl.when(t == 0)
    def _():
        st_ref[...] = jnp.zeros_like(st_ref)
        uh_ref[...] = jnp.zeros_like(uh_ref)
        ua_ref[...] = jnp.zeros_like(ua_ref)
        ub_ref[...] = jnp.zeros_like(ub_ref)

    def front_tile(r0):
        rows = pl.ds(r0, ROW_TILE)
        h = _rmsnorm(x_ref[rows, :], normg_ref[...]).astype(_BF16)
        for half in range(B_WIDTH // MXU_N):
            u, gb = _project_b(lambda: h, win_ref, bglu_ref, half)
            cols = slice(half * MXU_N, (half + 1) * MXU_N)
            here = pl.ds(_aligned(HIST + r0, HIST), ROW_TILE)
            prev = uh_ref[pl.ds(_aligned(HIST + r0 - SUBLANES, SUBLANES), SUBLANES), cols]
            uh_ref[here, cols] = u
            ua_ref[here, cols] = u.astype(_BF16)
            ub_ref[here, cols] = jnp.concatenate([prev, u[:ROW_TILE - SUBLANES, :]], axis=0).astype(_BF16)
            gb_ref[rows, cols] = gb

        def store_a(slab, val):
            pa_ref[slab, rows, :] = val
        for cg in range(2 * A_HEADS):
            _project_a(lambda: h, win_ref, lb_ref, store_a, cg)

    def front_body(p, carry):
        for n in range(UNROLL):
            front_tile(_aligned((UNROLL * p + n) * ROW_TILE, ROW_TILE))
        return carry

    _loop(tb // (UNROLL * ROW_TILE), front_body)
    ub_ref[HIST + tb:HIST + tb + 2 * SUBLANES, :] = jnp.concatenate(
        [uh_ref[HIST + tb - SUBLANES:HIST + tb, :], jnp.zeros((SUBLANES, B_WIDTH), _F32)], axis=0).astype(_BF16)

    def chunk(base, slot):
        def head(hd):
            def slabs(arr):
                return [pa_ref[arr * A_HEADS + hd, pl.ds(base + i, SUBLANES, stride=SUBLANES), :]
                        for i in range(SUBLANES)]

            def done(o, qc, kc, v_all, decay):
                st_ref[hd] = st_ref[hd] * decay + _dot_tn(v_all, kc)
                for i in range(SUBLANES):
                    oa_ref[hd, pl.ds(base + i, SUBLANES, stride=SUBLANES), :] = o[i * SUBLANES:(i + 1) * SUBLANES, :]
            return _hgrn_stages(slabs(0), slabs(1), slabs(2), masks_ref, lambda: st_ref[hd], True, done)
        conv = _conv_stages(ua_ref, ub_ref, wb_ref, convb_ref, lng_ref, lnb_ref, gb_ref, cv_ref.at[slot], cat_ref, base, CHUNK)
        return [head(hd) for hd in range(A_HEADS)] + [conv]

    def chunk_body(p, carry):
        _round_robin([chunk(_aligned((CHUNK_UNROLL * p + n) * CHUNK, CHUNK), n % CONV_BUFS) for n in range(CHUNK_UNROLL)])
        return carry

    _loop(tb // (CHUNK_UNROLL * CHUNK), chunk_body)

    @pl.when(t == nt - 1)
    def _():
        grp = pl.ds(pl.multiple_of((seq // SUBLANES) * SUBLANES, SUBLANES), SUBLANES)
        mine = lax.broadcasted_iota(jnp.int32, (SUBLANES, B_WIDTH), 0) == seq % SUBLANES
        for kk in range(CONV_WIDTH - 1):
            row = uh_ref[HIST + tb - (CONV_WIDTH - 1) + kk:HIST + tb - (CONV_WIDTH - 1) + kk + 1, :]
            cs_out_ref[kk, grp, :] = jnp.where(mine, jnp.broadcast_to(row, (SUBLANES, B_WIDTH)), cs_out_ref[kk, grp, :])
        for hd in range(A_HEADS):
            st_out_ref[hd] = st_ref[hd].T

    uh_ref[0:HIST, :] = uh_ref[tb:tb + HIST, :]
    ua_ref[0:HIST, :] = ua_ref[tb:tb + HIST, :]
    ub_ref[0:HIST, :] = ub_ref[tb:tb + HIST, :]

    @pl.when(first)
    def _():
        wout_fetch.wait()
        for r0 in range(0, wout_ref.shape[0], ROW_TILE):
            wout_ref[r0:r0 + ROW_TILE, :] = wout_f32_ref[r0:r0 + ROW_TILE, :].astype(_BF16)
        handoff[1].start()

    def back_body(p, carry):
        tiles = [pl.ds(_aligned((UNROLL * p + n) * ROW_TILE, ROW_TILE), ROW_TILE) for n in range(UNROLL)]
        for rows in tiles:
            for hd in range(A_HEADS):
                on = _head_out(oa_ref[hd, rows, :], ag_ref[...], pa_ref[3 * A_HEADS + hd, rows, :])
                cat_ref[rows, hd * HEAD_DIM:(hd + 1) * HEAD_DIM] = on.astype(_BF16)
        outs = [_dot(cat_ref[rows, :], wout_ref[...]) for rows in tiles]
        for rows, out in zip(tiles, outs):
            y_ref[rows, :] = _rmsnorm(x_ref[rows, :] + out, fg_ref[...])
        return carry

    _loop(tb // (UNROLL * ROW_TILE), back_body)

    @pl.when(first)
    def _():
        for copy in handoff:
            copy.wait()


def _weight_specs(weights, layer, in_hbm=()):
    specs = [_full(w.shape) for w in weights]
    specs[5] = pl.BlockSpec((None,) + weights[5].shape[1:], lambda *_: (layer, 0, 0), pipeline_mode=pl.Buffered(1))
    for i in in_hbm:
        specs[i] = pl.BlockSpec(memory_space=pl.ANY)
    return specs


def _full(shape):
    nd = len(shape)
    return pl.BlockSpec(shape, lambda *_: (0,) * nd, pipeline_mode=pl.Buffered(1))


def _prompt_call(x, weights, tb, layer):
    n, t, _ = x.shape
    assert t % tb == 0 and tb % (UNROLL * ROW_TILE) == 0 and tb % (CHUNK_UNROLL * CHUNK) == 0
    masks = jnp.asarray(_level_masks((1, 2, 4, 8, 16, 32), SUBLANES, SUBLANES))
    assert n % SUBLANES == 0
    kern = functools.partial(_prompt_kernel, tb=tb, layer=layer)
    w_in, w_out = weights[1], weights[9]
    assert w_in.dtype == _F32 and w_out.dtype == _F32 and w_in.shape[0] <= tb and w_in.shape[1] % LANES == 0
    in_hbm = pl.BlockSpec(memory_space=pl.ANY)
    w_specs = _weight_specs(weights, layer, in_hbm=(1, 9))
    return pl.pallas_call(
        kern,
        grid=(n, t // tb),
        in_specs=[pl.BlockSpec((None, tb, D_MODEL), lambda i, j: (i, j, 0))] + w_specs + [_full(masks.shape)],
        out_specs=[
            pl.BlockSpec((None, tb, D_MODEL), lambda i, j: (i, j, 0)),
            pl.BlockSpec((None, A_HEADS, HEAD_DIM, HEAD_DIM), lambda i, j: (i, 0, 0, 0)),
            _full((CONV_WIDTH - 1, n, B_WIDTH)),
            in_hbm,
            in_hbm,
        ],
        out_shape=[
            jax.ShapeDtypeStruct((n, t, D_MODEL), _F32),
            jax.ShapeDtypeStruct((n, A_HEADS, HEAD_DIM, HEAD_DIM), _F32),
            jax.ShapeDtypeStruct((CONV_WIDTH - 1, n, B_WIDTH), _F32),
            jax.ShapeDtypeStruct(w_in.shape, _BF16),
            jax.ShapeDtypeStruct(w_out.shape, _BF16),
        ],
        scratch_shapes=[
            pltpu.VMEM((4 * A_HEADS, tb, LANES), _F32),
            pltpu.VMEM((HIST + tb, B_WIDTH), _F32),
            pltpu.VMEM((HIST + tb + 2 * SUBLANES, B_WIDTH), _BF16),
            pltpu.VMEM((HIST + tb + 2 * SUBLANES, B_WIDTH), _BF16),
            pltpu.VMEM((tb, B_WIDTH), _F32),
            pltpu.VMEM((CONV_BUFS, CHUNK, B_WIDTH), _F32),
            pltpu.VMEM((A_HEADS, tb, LANES), _F32),
            pltpu.VMEM((tb, A_WIDTH + B_WIDTH), _BF16),
            pltpu.VMEM((A_HEADS, HEAD_DIM, HEAD_DIM), _F32),
            pltpu.VMEM(w_in.shape, _BF16),
            pltpu.VMEM(w_out.shape, _BF16),
            pltpu.VMEM(w_out.shape, _F32),
            pltpu.VMEM((1, A_WIDTH), _F32),
            pltpu.VMEM((CONV_WIDTH, 2 * SUBLANES, B_WIDTH), _BF16),
            pltpu.SemaphoreType.DMA((4 * A_HEADS + 3,)),
        ],
        compiler_params=pltpu.CompilerParams(
            dimension_semantics=("arbitrary", "arbitrary"),
            vmem_limit_bytes=VMEM_LIMIT_BYTES,
        ),
        name="hymba_prompt",
    )(x, *weights, masks)


def _decode_kernel(x_ref, s0_hbm, c0_ref, normg_ref, win_ref, lbl_ref, ag_ref, bglu_ref, convw_ref, convb_ref,
                   lng_ref, lnb_ref, wout_ref, fg_ref, masks_ref,
                   y_ref, s_out_ref, cs_out_ref,
                   pa_ref, ub_ref, gb_ref, oa_ref, ob_ref, lb_ref, xt_ref, s0_ref, sems, *, t_dec, layer, steps):
    g = pl.program_id(0)
    per_step = DEC_GROUPS * SUBLANES
    slot = g % STATE_BUFS

    def fetch(step, slot):
        return pltpu.make_async_copy(s0_hbm.at[pl.ds(step * per_step, per_step)], s0_ref.at[slot], sems.at[slot])
    n_seq = pa_ref.shape[1] // t_dec
    lane_tiles = D_MODEL // LANES
    grp = SUBLANES * t_dec
    n_ls = B_WIDTH // LANES
    tiles = [(pl.ds(t * n_seq + s0, ROW_TILE), pl.ds(s0, ROW_TILE), t)
             for t in range(t_dec) for s0 in range(0, n_seq, ROW_TILE)]

    @pl.when(g == 0)
    def _():
        for k in range(min(STATE_BUFS, steps)):
            fetch(k, k).start()
        lb_ref[...] = _forget_lower_bound(lbl_ref, layer)
        for rows, seq_rows, t in tiles:
            for j in range(lane_tiles):
                start = (seq_rows.start * lane_tiles + j) * t_dec + t
                xt_ref[rows, j * LANES:(j + 1) * LANES] = x_ref[pl.ds(start, ROW_TILE, stride=lane_tiles * t_dec), :]
            h = _rmsnorm(xt_ref[rows, :], normg_ref[...]).astype(_BF16)
            for half in range(B_WIDTH // MXU_N):
                u, gb = _project_b(lambda: h, win_ref, bglu_ref, half)
                for c in range(MXU_N // LANES):
                    ls = half * (MXU_N // LANES) + c
                    ub_ref[ls, rows, :] = u[:, c * LANES:(c + 1) * LANES]
                    gb_ref[ls, rows, :] = gb[:, c * LANES:(c + 1) * LANES]

            def store_a(slab, val, rows=rows):
                pa_ref[slab, rows, :] = val
            for cg in range(2 * A_HEADS):
                _project_a(lambda: h, win_ref, lb_ref, store_a, cg)

    fetch(g, slot).wait()

    def group(sub):
        seqs = slice(sub * SUBLANES, (sub + 1) * SUBLANES)
        seq0 = pl.multiple_of((g * DEC_GROUPS + sub) * SUBLANES, SUBLANES)
        token_rows = lambda t: pl.ds(t * n_seq + seq0, SUBLANES)

        def token_slab(ref, idx, t):
            return ref[idx, token_rows(t), :]

        u_tok = [jnp.concatenate([token_slab(ub_ref, ls, t) for ls in range(n_ls)], axis=-1) for t in range(t_dec)]
        full = lambda kk: c0_ref[kk, seqs, :] if kk < CONV_WIDTH - 1 else u_tok[kk - (CONV_WIDTH - 1)]
        for kk in range(CONV_WIDTH - 1):
            cs_out_ref[kk, seqs, :] = full(kk + t_dec)
        for t in range(t_dec):
            cv = jnp.broadcast_to(convb_ref[...], (SUBLANES, B_WIDTH))
            for j in range(CONV_WIDTH):
                cv = cv + convw_ref[j:j + 1, :] * full(t + j)
            gate_b = jnp.concatenate([token_slab(gb_ref, ls, t) for ls in range(n_ls)], axis=-1)
            o_b = _group_b_out(cv, lng_ref, lnb_ref, gate_b)
            for ls in range(n_ls):
                ob_ref[ls, token_rows(t), :] = o_b[:, ls * LANES:(ls + 1) * LANES]

        row32 = lax.broadcasted_iota(jnp.int32, (grp, LANES), 0) % SUBLANES
        row64 = lax.broadcasted_iota(jnp.int32, (2 * grp, LANES), 0)
        for hd in range(A_HEADS):
            slabs = lambda arr: [token_slab(pa_ref, arr * A_HEADS + hd, t) for t in range(t_dec)]
            res = []
            for _ in _hgrn_stages(slabs(0), slabs(1), slabs(2), masks_ref, None, False, lambda *a: res.extend(a)):
                pass
            o, qc, kc, v_all, decay = res
            qf, kf, vf = qc.astype(_F32), kc.astype(_F32), v_all.astype(_F32)
            e_hi = decay.astype(_BF16).astype(_F32)
            e_mid = (decay - e_hi).astype(_BF16).astype(_F32)
            e_lo = (decay - e_hi - e_mid).astype(_BF16).astype(_F32)
            lhs = jnp.concatenate([kf, e_hi, e_mid, e_lo, jnp.zeros((SUBLANES, LANES), _F32)], axis=0).astype(_BF16)
            v_pad = jnp.concatenate([vf, jnp.zeros((grp, LANES), _F32)], axis=0)
            o_state = None
            for s in range(SUBLANES):
                s0 = s0_ref[slot, sub * SUBLANES + s, hd]
                term = _dot(jnp.where(row32 == s, qf, 0.0).astype(_BF16), s0.astype(_BF16))
                o_state = term if o_state is None else o_state + term
                mine = row64 % SUBLANES == s
                rhs = jnp.concatenate([
                    jnp.where(mine & (row64 < grp), v_pad, 0.0),
                    jnp.where(mine & (row64 >= grp) & (row64 < grp + 3 * SUBLANES), 1.0, 0.0)], axis=-1).astype(_BF16)
                upd = _dot_tn(lhs, rhs)
                s_out_ref[sub * SUBLANES + s, hd] = s0 * upd[:, HEAD_DIM:] + upd[:, :HEAD_DIM]
            gate = jnp.concatenate([token_slab(pa_ref, 3 * A_HEADS + hd, t) for t in range(t_dec)], axis=0)
            on = _head_out(o + o_state, ag_ref[...], gate)
            for t in range(t_dec):
                oa_ref[hd, token_rows(t), :] = on[t * SUBLANES:(t + 1) * SUBLANES, :]

    for sub in range(DEC_GROUPS):
        group(sub)

    @pl.when(g + STATE_BUFS < steps)
    def _():
        fetch(g + STATE_BUFS, slot).start()

    step_rows = [pl.ds(pl.multiple_of(t * n_seq + g * per_step, per_step), per_step) for t in range(t_dec)]
    cat = jnp.concatenate(
        [jnp.concatenate([oa_ref[hd, rows, :] for hd in range(A_HEADS)] + [ob_ref[ls, rows, :] for ls in range(n_ls)],
                         axis=-1) for rows in step_rows], axis=0).astype(_BF16)
    out = _dot(cat, wout_ref[...])
    for t, rows in enumerate(step_rows):
        y_ref[:, t, :] = _rmsnorm(xt_ref[rows, :] + out[t * per_step:(t + 1) * per_step, :], fg_ref[...])


def _decode_call(x, s0, c0, weights, layer):
    n, t_dec, _ = x.shape
    assert n % ROW_TILE == 0 and t_dec == 4
    n_tok = n * t_dec
    x_rows = x.reshape(n, t_dec, D_MODEL // LANES, LANES).transpose(0, 2, 1, 3).reshape(n_tok * D_MODEL // LANES, LANES)
    masks = jnp.asarray(_level_masks((1, 2), t_dec, SUBLANES, same_group=t_dec))
    per_step = DEC_GROUPS * SUBLANES
    kern = functools.partial(_decode_kernel, t_dec=t_dec, layer=layer, steps=n // per_step)
    w_specs = _weight_specs(weights, layer)
    state_spec = pl.BlockSpec((per_step, A_HEADS, HEAD_DIM, HEAD_DIM), lambda i: (i, 0, 0, 0))
    conv_spec = pl.BlockSpec((CONV_WIDTH - 1, per_step, B_WIDTH), lambda i: (0, i, 0))
    y, s_new, c_new = pl.pallas_call(
        kern,
        grid=(n // per_step,),
        in_specs=[_full(x_rows.shape), pl.BlockSpec(memory_space=pl.ANY), conv_spec] + w_specs + [_full(masks.shape)],
        out_specs=[pl.BlockSpec((per_step, t_dec, D_MODEL), lambda i: (i, 0, 0)), state_spec, conv_spec],
        out_shape=[
            jax.ShapeDtypeStruct(x.shape, _F32),
            jax.ShapeDtypeStruct((n, A_HEADS, HEAD_DIM, HEAD_DIM), _F32),
            jax.ShapeDtypeStruct((CONV_WIDTH - 1, n, B_WIDTH), _F32),
        ],
        scratch_shapes=[
            pltpu.VMEM((4 * A_HEADS, n_tok, LANES), _F32),
            pltpu.VMEM((B_WIDTH // LANES, n_tok, LANES), _F32),
            pltpu.VMEM((B_WIDTH // LANES, n_tok, LANES), _F32),
            pltpu.VMEM((A_HEADS, n_tok, LANES), _F32),
            pltpu.VMEM((B_WIDTH // LANES, n_tok, LANES), _F32),
            pltpu.VMEM((1, A_WIDTH), _F32),
            pltpu.VMEM((n_tok, D_MODEL), _F32),
            pltpu.VMEM((STATE_BUFS, per_step, A_HEADS, HEAD_DIM, HEAD_DIM), _F32),
            pltpu.SemaphoreType.DMA((STATE_BUFS,)),
        ],
        compiler_params=pltpu.CompilerParams(
            dimension_semantics=("arbitrary",),
            vmem_limit_bytes=VMEM_LIMIT_BYTES,
        ),
        name="hymba_decode",
    )(x_rows, s0, c0, *weights, masks)
    return y, s_new, c_new


def _prepare_weights(norm_in_g, w_in, lb_logits, hgrn_norm_g, b_glu, conv_w, conv_b, ln_g, ln_b, w_out, final_norm_g, layer):
    row = lambda a: a.astype(_F32).reshape(1, -1)
    return (
        row(norm_in_g[layer]),
        w_in[layer].astype(_F32),
        lb_logits.astype(_F32),
        row(hgrn_norm_g[layer]),
        row(b_glu[layer]),
        conv_w.astype(_F32),
        row(conv_b[layer]),
        row(ln_g[layer]),
        row(ln_b[layer]),
        w_out[layer].astype(_F32),
        row(final_norm_g),
    )


def kernel(x_prompt, x_sample, state_hgrn, state_conv, norm_in_g, w_in, lb_logits, hgrn_norm_g, b_glu, conv_w,
           conv_b, ln_g, ln_b, w_out, final_norm_g):
    depth = w_in.shape[0]
    assert depth == 1, "single mixer layer: the final norm is fused into the layer kernel"
    weights = _prepare_weights(norm_in_g, w_in, lb_logits, hgrn_norm_g, b_glu, conv_w, conv_b, ln_g, ln_b, w_out,
                               final_norm_g, 0)
    y_p, s_p, c_p, w_in_b, w_out_b = _prompt_call(x_prompt, weights, tb=1024, layer=0)
    weights = weights[:1] + (w_in_b,) + weights[2:9] + (w_out_b,) + weights[10:]
    y_s, s_s, c_s = _decode_call(x_sample, state_hgrn[0], jnp.transpose(state_conv[0], (1, 0, 2)), weights, layer=0)
    return (y_p, y_s, s_p[None], jnp.transpose(c_p, (1, 0, 2))[None], s_s[None], jnp.transpose(c_s, (1, 0, 2))[None])
```

```python
import functools
import math

import numpy as np
import jax
import jax.numpy as jnp
from jax import lax
from jax.experimental import pallas as pl
from jax.experimental.pallas import tpu as pltpu

D_MODEL = 1024
A_HEADS = 4
HEAD_DIM = 128
A_WIDTH = A_HEADS * HEAD_DIM
B_WIDTH = 512
CONV_WIDTH = 31
EPS = 1e-6
IN_WIDTH = 4 * A_WIDTH + 3 * B_WIDTH
GLU_OFF = 4 * A_WIDTH
ZB_OFF = GLU_OFF + 2 * B_WIDTH

LANES = 128
SUBLANES = 8
MXU_N = 256
CHUNK = SUBLANES * SUBLANES
ROW_TILE = 128
UNROLL = 8
CHUNK_UNROLL = 2
CHUNK_LAG = 4
CONV_BUFS = 3
DEC_GROUPS = 2
STATE_BUFS = 4
HIST = 32
FIRST = HIST - (CONV_WIDTH - 1)
VMEM_LIMIT_BYTES = 56 * 1024 * 1024
LOG2E = math.log2(math.e)

_F32 = jnp.float32
_BF16 = jnp.bfloat16


def _sigmoid(x):
    return 1.0 / (1.0 + jnp.exp(-x))


def _silu(x):
    return x * _sigmoid(x)


def _dot(a, b):
    return jnp.dot(a, b, preferred_element_type=_F32)


def _dot_nt(a, b):
    return lax.dot_general(a, b, (((1,), (1,)), ((), ())), preferred_element_type=_F32)


def _dot_tn(a, b):
    return lax.dot_general(a, b, (((0,), (0,)), ((), ())), preferred_element_type=_F32)


def _level_masks(levels, n_res, n_rows, same_group=None):
    r = np.arange(n_res * n_rows)
    t = n_res * (r % n_rows) + r // n_rows
    tt, ss = t[:, None], t[None, :]
    masks = [tt == ss]
    for b in levels:
        masks.append(((tt // b) == (ss // b) + 1) & ((ss // b) % 2 == 0))
    masks = np.stack(masks)
    want = (ss <= tt) if same_group is None else ((ss <= tt) & (tt // same_group == ss // same_group))
    assert (masks.sum(0) == want).all()
    return masks.astype(np.float32)


def _rows(x, r):
    return jnp.broadcast_to(x[r:r + 1, :], x.shape)


def _hgrn_stages(q, f, v, masks_ref, get_st, coarse, done):
    n_res = len(q)
    every = range(n_res)
    k = [1.0 - fi for fi in f]
    lf = [jnp.log(fi) * LOG2E for fi in f]
    g_in = [lf[0]]
    for i in range(1, n_res):
        g_in.append(g_in[-1] + lf[i])
    tot = g_in[-1]
    zero = jnp.zeros_like(tot)

    def cat(parts):
        return jnp.concatenate(parts, axis=0).astype(_BF16)

    def slab_rows(x, i):
        return x[i * SUBLANES:(i + 1) * SUBLANES, :]

    v_all = cat(v)
    s_rows = [None] * n_res
    pending = []

    def fold():
        level, rows, s_l = pending.pop(0)
        for n, i in enumerate(rows):
            term = masks_ref[level, i * SUBLANES:(i + 1) * SUBLANES, :] * slab_rows(s_l, n)
            s_rows[i] = term if s_rows[i] is None else s_rows[i] + term

    def step(rows, qt, kt):
        pending.append((step.level, rows, _dot_nt(cat(qt), cat(kt))))
        step.level += 1
    step.level = 0

    step(every, q, k)
    yield
    b = 1
    while b < n_res:
        odd = [i for i in every if (i // b) % 2 == 1]
        qt, kt = [], []
        for i in every:
            bs = (i // b) * b
            if i in odd:
                qt.append(q[i] * jnp.exp2(g_in[i] - g_in[bs - 1]))
                kt.append(zero)
            else:
                be = bs + b - 1
                kt.append(k[i] * jnp.exp2(g_in[be] - g_in[i]) if i != be else k[i])
        step(odd, qt, kt)
        yield
        fold()
        b *= 2
    qe = [q[i] * jnp.exp2(g_in[i]) for i in every]
    ke = [k[i] * jnp.exp2(tot - g_in[i]) for i in range(n_res - 1)] + [k[-1]]
    if coarse:
        row = lax.broadcasted_iota(jnp.int32, tot.shape, 0)
        pref = tot
        for sh in (1, 2, 4):
            pref = pref + jnp.where(row >= sh, pltpu.roll(pref, sh, 0), 0.0)
        before = pref - tot
        end_all = _rows(pref, SUBLANES - 1)
        qe_b, ke_b = cat(qe), cat(ke)
        tile = lambda x: jnp.concatenate([x] * n_res, axis=0).astype(_BF16)
        for gsz in (1, 2, 4):
            if gsz == 1:
                qt, kt = qe_b, ke_b
            else:
                if gsz == 2:
                    g_start = jnp.where(row % 2 == 0, before, pltpu.roll(before, 1, 0))
                    g_end = jnp.where(row % 2 == 1, pref, pltpu.roll(pref, SUBLANES - 1, 0))
                else:
                    g_start = jnp.where(row < 4, _rows(before, 0), _rows(before, 4))
                    g_end = jnp.where(row < 4, _rows(pref, 3), _rows(pref, 7))
                qt = qe_b * tile(jnp.exp2(before - g_start))
                kt = ke_b * tile(jnp.exp2(g_end - pref))
            pending.append((step.level, every, _dot_nt(qt, kt)))
            step.level += 1
            yield
            fold()
        qc = qe_b * tile(jnp.exp2(before))
        kc = ke_b * tile(jnp.exp2(end_all - pref))
        o_st = _dot_nt(qc, get_st().astype(_BF16))
        decay = jnp.exp2(pref[SUBLANES - 1:SUBLANES, :])
        yield
    else:
        qc, kc = cat(qe), cat(ke)
        o_st = None
        decay = jnp.exp2(tot)
    while pending:
        fold()
    o = _dot(cat(s_rows), v_all)
    yield
    done(o if o_st is None else o + o_st, qc, kc, v_all, decay)


def _round_robin(starts):
    waiting = list(starts)
    live = []
    rnd = 0
    while waiting or live:
        if waiting and rnd % CHUNK_LAG == 0:
            live += waiting.pop(0)
        for g in list(live):
            try:
                next(g)
            except StopIteration:
                live.remove(g)
        rnd += 1


def _aligned(x, m):
    return x if isinstance(x, int) else pl.multiple_of(x, m)


def _loop(trips, body):
    if trips == 1:
        body(0, 0)
    else:
        lax.fori_loop(0, trips, body, 0)


def _head_out(o, ag, gate):
    ms = jnp.mean(o * o, axis=-1, keepdims=True)
    return o * lax.rsqrt(ms + EPS) * ag * gate


def _forget_lower_bound(logits_ref, layer):
    rows = [logits_ref[i:i + 1, :] for i in range(logits_ref.shape[0])]
    top = functools.reduce(jnp.maximum, rows)
    e = [jnp.exp(r - top) for r in rows]
    total = functools.reduce(jnp.add, e)
    return functools.reduce(jnp.add, [v / total for v in e[:layer + 1]])


def _project_a(h, win_ref, lb_ref, store, cg):
    p = _dot(h(), win_ref[:, cg * MXU_N:(cg + 1) * MXU_N])
    arr = cg // 2
    if arr == 0 or arr == 3:
        p = _silu(p)
    elif arr == 1:
        lb = lb_ref[:, (cg % 2) * MXU_N:(cg % 2 + 1) * MXU_N]
        p = lb + (1.0 - lb) * _sigmoid(p)
    store(2 * cg, p[:, :LANES])
    store(2 * cg + 1, p[:, LANES:])


def _project_b(h, win_ref, bglu_ref, half):
    c0 = half * MXU_N
    glu_a = _dot(h(), win_ref[:, GLU_OFF + c0:GLU_OFF + c0 + MXU_N]) + bglu_ref[:, c0:c0 + MXU_N]
    glu_b = (_dot(h(), win_ref[:, GLU_OFF + B_WIDTH + c0:GLU_OFF + B_WIDTH + c0 + MXU_N])
             + bglu_ref[:, B_WIDTH + c0:B_WIDTH + c0 + MXU_N])
    zb = _dot(h(), win_ref[:, ZB_OFF + c0:ZB_OFF + c0 + MXU_N])
    return glu_a * _sigmoid(glu_b), _silu(zb)


def _rmsnorm(x, g):
    return x * lax.rsqrt(jnp.mean(x * x, axis=-1, keepdims=True) + EPS) * g


def _group_b_out(cv, lng_ref, lnb_ref, gate_b):
    mu = jnp.mean(cv, axis=-1, keepdims=True)
    d = cv - mu
    var = jnp.mean(d * d, axis=-1, keepdims=True)
    ln = d * lax.rsqrt(var + EPS) * lng_ref[...] + lnb_ref[...]
    return _silu(ln) * gate_b


def _cast_weight(src_hbm, dst_ref, slab_refs, sems):
    n = dst_ref.shape[1] // LANES
    slots = len(slab_refs)

    def copy(j):
        return pltpu.make_async_copy(src_hbm.at[:, pl.ds(j * LANES, LANES)], slab_refs[j % slots], sems.at[j % slots])

    for j in range(min(slots, n)):
        copy(j).start()
    for j in range(n):
        copy(j).wait()
        dst_ref[:, j * LANES:(j + 1) * LANES] = slab_refs[j % slots][...].astype(_BF16)
        if j + slots < n:
            copy(j + slots).start()


def _conv_stages(ua_ref, ub_ref, wb_ref, convb_ref, lng_ref, lnb_ref, gb_ref, cv_ref, cat_ref, base, n):
    reps = lambda w, rows: jnp.concatenate([w] * (rows // (2 * SUBLANES)), axis=0)
    for ls in range(B_WIDTH // LANES):
        lanes = slice(ls * LANES, (ls + 1) * LANES)
        acc = jnp.broadcast_to(convb_ref[:, lanes], (n, LANES))
        for off in range(SUBLANES):
            taps = [j for j in range(CONV_WIDTH) if (FIRST + j) % SUBLANES == off]
            rows = n + (2 * SUBLANES if off else 0)
            part = None
            for j in taps:
                rel = FIRST + j - off
                if rel % (2 * SUBLANES) == 0:
                    win = ua_ref[pl.ds(_aligned(base + rel, 2 * SUBLANES), rows), lanes]
                else:
                    win = ub_ref[pl.ds(_aligned(base + rel + SUBLANES, 2 * SUBLANES), rows), lanes]
                term = reps(wb_ref[j, :, lanes], rows) * win
                part = term if part is None else part + term
            part = part.astype(_F32)
            acc = acc + (part[off:off + n, :] if off else part)
            if off == SUBLANES // 2 - 1:
                yield
        cv_ref[:, lanes] = acc
        yield
    rows = pl.ds(_aligned(base, n), n)
    o_b = _group_b_out(cv_ref[...], lng_ref, lnb_ref, gb_ref[rows, :])
    cat_ref[rows, A_WIDTH:] = o_b.astype(_BF16)


def _prompt_kernel(x_ref, normg_ref, win_hbm, lbl_ref, ag_ref, bglu_ref, convw_ref, convb_ref,
                   lng_ref, lnb_ref, wout_hbm, fg_ref, masks_ref,
                   y_ref, st_out_ref, cs_out_ref, win_out_hbm, wout_out_hbm,
                   pa_ref, uh_ref, ua_ref, ub_ref, gb_ref, cv_ref, oa_ref, cat_ref, st_ref,
                   win_ref, wout_ref, wout_f32_ref, lb_ref, wb_ref, sems, *, tb, layer):
    seq = pl.program_id(0)
    t = pl.program_id(1)
    nt = pl.num_programs(1)

    first = (seq == 0) & (t == 0)
    n_slabs = pa_ref.shape[0]
    handoff = [pltpu.make_async_copy(win_ref, win_out_hbm, sems.at[n_slabs]),
               pltpu.make_async_copy(wout_ref, wout_out_hbm, sems.at[n_slabs + 1])]
    wout_fetch = pltpu.make_async_copy(wout_hbm, wout_f32_ref, sems.at[n_slabs + 2])

    @pl.when(first)
    def _():
        _cast_weight(win_hbm, win_ref, [pa_ref.at[s, pl.ds(0, win_ref.shape[0])] for s in range(n_slabs)], sems)
        wout_fetch.start()
        handoff[0].start()
        lb_ref[...] = _forget_lower_bound(lbl_ref, layer)
        for j in range(CONV_WIDTH):
            wb_ref[j] = jnp.broadcast_to(convw_ref[j:j + 1, :], (2 * SUBLANES, B_WIDTH)).astype(_BF16)
        cs_out_ref[...] = jnp.zeros_like(cs_out_ref)

    @pl.when(t == 0)
    def _():
        st_ref[...] = jnp.zeros_like(st_ref)
        uh_ref[...] = jnp.zeros_like(uh_ref)
        ua_ref[...] = jnp.zeros_like(ua_ref)
        ub_ref[...] = jnp.zeros_like(ub_ref)

    def front_tile(r0):
        rows = pl.ds(r0, ROW_TILE)
        h = _rmsnorm(x_ref[rows, :], normg_ref[...]).astype(_BF16)
        for half in range(B_WIDTH // MXU_N):
            u, gb = _project_b(lambda: h, win_ref, bglu_ref, half)
            cols = slice(half * MXU_N, (half + 1) * MXU_N)
            here = pl.ds(_aligned(HIST + r0, HIST), ROW_TILE)
            prev = uh_ref[pl.ds(_aligned(HIST + r0 - SUBLANES, SUBLANES), SUBLANES), cols]
            uh_ref[here, cols] = u
            ua_ref[here, cols] = u.astype(_BF16)
            ub_ref[here, cols] = jnp.concatenate([prev, u[:ROW_TILE - SUBLANES, :]], axis=0).astype(_BF16)
            gb_ref[rows, cols] = gb

        def store_a(slab, val):
            pa_ref[slab, rows, :] = val
        for cg in range(2 * A_HEADS):
            _project_a(lambda: h, win_ref, lb_ref, store_a, cg)

    def front_body(p, carry):
        for n in range(UNROLL):
            front_tile(_aligned((UNROLL * p + n) * ROW_TILE, ROW_TILE))
        return carry

    _loop(tb // (UNROLL * ROW_TILE), front_body)
    ub_ref[HIST + tb:HIST + tb + 2 * SUBLANES, :] = jnp.concatenate(
        [uh_ref[HIST + tb - SUBLANES:HIST + tb, :], jnp.zeros((SUBLANES, B_WIDTH), _F32)], axis=0).astype(_BF16)

    def chunk(base, slot):
        def head(hd):
            def slabs(arr):
                return [pa_ref[arr * A_HEADS + hd, pl.ds(base + i, SUBLANES, stride=SUBLANES), :]
                        for i in range(SUBLANES)]

            def done(o, qc, kc, v_all, decay):
                st_ref[hd] = st_ref[hd] * decay + _dot_tn(v_all, kc)
                for i in range(SUBLANES):
                    oa_ref[hd, pl.ds(base + i, SUBLANES, stride=SUBLANES), :] = o[i * SUBLANES:(i + 1) * SUBLANES, :]
            return _hgrn_stages(slabs(0), slabs(1), slabs(2), masks_ref, lambda: st_ref[hd], True, done)
        conv = _conv_stages(ua_ref, ub_ref, wb_ref, convb_ref, lng_ref, lnb_ref, gb_ref, cv_ref.at[slot], cat_ref, base, CHUNK)
        return [head(hd) for hd in range(A_HEADS)] + [conv]

    def chunk_body(p, carry):
        _round_robin([chunk(_aligned((CHUNK_UNROLL * p + n) * CHUNK, CHUNK), n % CONV_BUFS) for n in range(CHUNK_UNROLL)])
        return carry

    _loop(tb // (CHUNK_UNROLL * CHUNK), chunk_body)

    @pl.when(t == nt - 1)
    def _():
        grp = pl.ds(pl.multiple_of((seq // SUBLANES) * SUBLANES, SUBLANES), SUBLANES)
        mine = lax.broadcasted_iota(jnp.int32, (SUBLANES, B_WIDTH), 0) == seq % SUBLANES
        for kk in range(CONV_WIDTH - 1):
            row = uh_ref[HIST + tb - (CONV_WIDTH - 1) + kk:HIST + tb - (CONV_WIDTH - 1) + kk + 1, :]
            cs_out_ref[kk, grp, :] = jnp.where(mine, jnp.broadcast_to(row, (SUBLANES, B_WIDTH)), cs_out_ref[kk, grp, :])
        for hd in range(A_HEADS):
            st_out_ref[hd] = st_ref[hd].T

    uh_ref[0:HIST, :] = uh_ref[tb:tb + HIST, :]
    ua_ref[0:HIST, :] = ua_ref[tb:tb + HIST, :]
    ub_ref[0:HIST, :] = ub_ref[tb:tb + HIST, :]

    @pl.when(first)
    def _():
        wout_fetch.wait()
        for r0 in range(0, wout_ref.shape[0], ROW_TILE):
            wout_ref[r0:r0 + ROW_TILE, :] = wout_f32_ref[r0:r0 + ROW_TILE, :].astype(_BF16)
        handoff[1].start()

    def back_body(p, carry):
        tiles = [pl.ds(_aligned((UNROLL * p + n) * ROW_TILE, ROW_TILE), ROW_TILE) for n in range(UNROLL)]
        for rows in tiles:
            for hd in range(A_HEADS):
                on = _head_out(oa_ref[hd, rows, :], ag_ref[...], pa_ref[3 * A_HEADS + hd, rows, :])
                cat_ref[rows, hd * HEAD_DIM:(hd + 1) * HEAD_DIM] = on.astype(_BF16)
        outs = [_dot(cat_ref[rows, :], wout_ref[...]) for rows in tiles]
        for rows, out in zip(tiles, outs):
            y_ref[rows, :] = _rmsnorm(x_ref[rows, :] + out, fg_ref[...])
        return carry

    _loop(tb // (UNROLL * ROW_TILE), back_body)

    @pl.when(first)
    def _():
        for copy in handoff:
            copy.wait()


def _weight_specs(weights, layer, in_hbm=()):
    specs = [_full(w.shape) for w in weights]
    specs[5] = pl.BlockSpec((None,) + weights[5].shape[1:], lambda *_: (layer, 0, 0), pipeline_mode=pl.Buffered(1))
    for i in in_hbm:
        specs[i] = pl.BlockSpec(memory_space=pl.ANY)
    return specs


def _full(shape):
    nd = len(shape)
    return pl.BlockSpec(shape, lambda *_: (0,) * nd, pipeline_mode=pl.Buffered(1))


def _prompt_call(x, weights, tb, layer):
    n, t, _ = x.shape
    assert t % tb == 0 and tb % (UNROLL * ROW_TILE) == 0 and tb % (CHUNK_UNROLL * CHUNK) == 0
    masks = jnp.asarray(_level_masks((1, 2, 4, 8, 16, 32), SUBLANES, SUBLANES))
    assert n % SUBLANES == 0
    kern = functools.partial(_prompt_kernel, tb=tb, layer=layer)
    w_in, w_out = weights[1], weights[9]
    assert w_in.dtype == _F32 and w_out.dtype == _F32 and w_in.shape[0] <= tb and w_in.shape[1] % LANES == 0
    in_hbm = pl.BlockSpec(memory_space=pl.ANY)
    w_specs = _weight_specs(weights, layer, in_hbm=(1, 9))
    return pl.pallas_call(
        kern,
        grid=(n, t // tb),
        in_specs=[pl.BlockSpec((None, tb, D_MODEL), lambda i, j: (i, j, 0))] + w_specs + [_full(masks.shape)],
        out_specs=[
            pl.BlockSpec((None, tb, D_MODEL), lambda i, j: (i, j, 0)),
            pl.BlockSpec((None, A_HEADS, HEAD_DIM, HEAD_DIM), lambda i, j: (i, 0, 0, 0)),
            _full((CONV_WIDTH - 1, n, B_WIDTH)),
            in_hbm,
            in_hbm,
        ],
        out_shape=[
            jax.ShapeDtypeStruct((n, t, D_MODEL), _F32),
            jax.ShapeDtypeStruct((n, A_HEADS, HEAD_DIM, HEAD_DIM), _F32),
            jax.ShapeDtypeStruct((CONV_WIDTH - 1, n, B_WIDTH), _F32),
            jax.ShapeDtypeStruct(w_in.shape, _BF16),
            jax.ShapeDtypeStruct(w_out.shape, _BF16),
        ],
        scratch_shapes=[
            pltpu.VMEM((4 * A_HEADS, tb, LANES), _F32),
            pltpu.VMEM((HIST + tb, B_WIDTH), _F32),
            pltpu.VMEM((HIST + tb + 2 * SUBLANES, B_WIDTH), _BF16),
            pltpu.VMEM((HIST + tb + 2 * SUBLANES, B_WIDTH), _BF16),
            pltpu.VMEM((tb, B_WIDTH), _F32),
            pltpu.VMEM((CONV_BUFS, CHUNK, B_WIDTH), _F32),
            pltpu.VMEM((A_HEADS, tb, LANES), _F32),
            pltpu.VMEM((tb, A_WIDTH + B_WIDTH), _BF16),
            pltpu.VMEM((A_HEADS, HEAD_DIM, HEAD_DIM), _F32),
            pltpu.VMEM(w_in.shape, _BF16),
            pltpu.VMEM(w_out.shape, _BF16),
            pltpu.VMEM(w_out.shape, _F32),
            pltpu.VMEM((1, A_WIDTH), _F32),
            pltpu.VMEM((CONV_WIDTH, 2 * SUBLANES, B_WIDTH), _BF16),
            pltpu.SemaphoreType.DMA((4 * A_HEADS + 3,)),
        ],
        compiler_params=pltpu.CompilerParams(
            dimension_semantics=("arbitrary", "arbitrary"),
            vmem_limit_bytes=VMEM_LIMIT_BYTES,
        ),
        name="hymba_prompt",
    )(x, *weights, masks)


def _decode_kernel(x_ref, s0_hbm, c0_ref, normg_ref, win_hbm, lbl_ref, ag_ref, bglu_ref, convw_ref, convb_ref,
                   lng_ref, lnb_ref, wout_hbm, fg_ref, masks_ref,
                   y_ref, s_out_ref, cs_out_ref,
                   pa_ref, ub_ref, gb_ref, oa_ref, ob_ref, lb_ref, xt_ref, s0_ref, win_ref, wout_ref, sems, *, t_dec, layer, steps):
    g = pl.program_id(0)
    per_step = DEC_GROUPS * SUBLANES
    slot = g % STATE_BUFS

    def fetch(step, slot):
        return pltpu.make_async_copy(s0_hbm.at[pl.ds(step * per_step, per_step)], s0_ref.at[slot], sems.at[slot])
    n_seq = pa_ref.shape[1] // t_dec
    lane_tiles = D_MODEL // LANES
    grp = SUBLANES * t_dec
    n_ls = B_WIDTH // LANES
    tiles = [(pl.ds(t * n_seq + s0, ROW_TILE), pl.ds(s0, ROW_TILE), t)
             for t in range(t_dec) for s0 in range(0, n_seq, ROW_TILE)]

    w_fetch = [pltpu.make_async_copy(win_hbm, win_ref, sems.at[STATE_BUFS]),
               pltpu.make_async_copy(wout_hbm, wout_ref, sems.at[STATE_BUFS + 1])]

    @pl.when(g == 0)
    def _():
        w_fetch[0].start()
        for k in range(min(STATE_BUFS, steps)):
            fetch(k, k).start()
        w_fetch[1].start()
        lb_ref[...] = _forget_lower_bound(lbl_ref, layer)
        w_fetch[0].wait()
        for rows, seq_rows, t in tiles:
            for j in range(lane_tiles):
                start = (seq_rows.start * lane_tiles + j) * t_dec + t
                xt_ref[rows, j * LANES:(j + 1) * LANES] = x_ref[pl.ds(start, ROW_TILE, stride=lane_tiles * t_dec), :]
            h = _rmsnorm(xt_ref[rows, :], normg_ref[...]).astype(_BF16)
            for half in range(B_WIDTH // MXU_N):
                u, gb = _project_b(lambda: h, win_ref, bglu_ref, half)
                for c in range(MXU_N // LANES):
                    ls = half * (MXU_N // LANES) + c
                    ub_ref[ls, rows, :] = u[:, c * LANES:(c + 1) * LANES]
                    gb_ref[ls, rows, :] = gb[:, c * LANES:(c + 1) * LANES]

            def store_a(slab, val, rows=rows):
                pa_ref[slab, rows, :] = val
            for cg in range(2 * A_HEADS):
                _project_a(lambda: h, win_ref, lb_ref, store_a, cg)

    fetch(g, slot).wait()

    def group(sub):
        seqs = slice(sub * SUBLANES, (sub + 1) * SUBLANES)
        seq0 = pl.multiple_of((g * DEC_GROUPS + sub) * SUBLANES, SUBLANES)
        token_rows = lambda t: pl.ds(t * n_seq + seq0, SUBLANES)

        def token_slab(ref, idx, t):
            return ref[idx, token_rows(t), :]

        u_tok = [jnp.concatenate([token_slab(ub_ref, ls, t) for ls in range(n_ls)], axis=-1) for t in range(t_dec)]
        full = lambda kk: c0_ref[kk, seqs, :] if kk < CONV_WIDTH - 1 else u_tok[kk - (CONV_WIDTH - 1)]
        for kk in range(CONV_WIDTH - 1):
            cs_out_ref[kk, seqs, :] = full(kk + t_dec)
        for t in range(t_dec):
            cv = jnp.broadcast_to(convb_ref[...], (SUBLANES, B_WIDTH))
            for j in range(CONV_WIDTH):
                cv = cv + convw_ref[j:j + 1, :] * full(t + j)
            gate_b = jnp.concatenate([token_slab(gb_ref, ls, t) for ls in range(n_ls)], axis=-1)
            o_b = _group_b_out(cv, lng_ref, lnb_ref, gate_b)
            for ls in range(n_ls):
                ob_ref[ls, token_rows(t), :] = o_b[:, ls * LANES:(ls + 1) * LANES]

        row32 = lax.broadcasted_iota(jnp.int32, (grp, LANES), 0) % SUBLANES
        row64 = lax.broadcasted_iota(jnp.int32, (2 * grp, LANES), 0)
        for hd in range(A_HEADS):
            slabs = lambda arr: [token_slab(pa_ref, arr * A_HEADS + hd, t) for t in range(t_dec)]
            res = []
            for _ in _hgrn_stages(slabs(0), slabs(1), slabs(2), masks_ref, None, False, lambda *a: res.extend(a)):
                pass
            o, qc, kc, v_all, decay = res
            qf, kf, vf = qc.astype(_F32), kc.astype(_F32), v_all.astype(_F32)
            e_hi = decay.astype(_BF16).astype(_F32)
            e_mid = (decay - e_hi).astype(_BF16).astype(_F32)
            e_lo = (decay - e_hi - e_mid).astype(_BF16).astype(_F32)
            lhs = jnp.concatenate([kf, e_hi, e_mid, e_lo, jnp.zeros((SUBLANES, LANES), _F32)], axis=0).astype(_BF16)
            v_pad = jnp.concatenate([vf, jnp.zeros((grp, LANES), _F32)], axis=0)
            o_state = None
            for s in range(SUBLANES):
                s0 = s0_ref[slot, sub * SUBLANES + s, hd]
                term = _dot(jnp.where(row32 == s, qf, 0.0).astype(_BF16), s0.astype(_BF16))
                o_state = term if o_state is None else o_state + term
                mine = row64 % SUBLANES == s
                rhs = jnp.concatenate([
                    jnp.where(mine & (row64 < grp), v_pad, 0.0),
                    jnp.where(mine & (row64 >= grp) & (row64 < grp + 3 * SUBLANES), 1.0, 0.0)], axis=-1).astype(_BF16)
                upd = _dot_tn(lhs, rhs)
                s_out_ref[sub * SUBLANES + s, hd] = s0 * upd[:, HEAD_DIM:] + upd[:, :HEAD_DIM]
            gate = jnp.concatenate([token_slab(pa_ref, 3 * A_HEADS + hd, t) for t in range(t_dec)], axis=0)
            on = _head_out(o + o_state, ag_ref[...], gate)
            for t in range(t_dec):
                oa_ref[hd, token_rows(t), :] = on[t * SUBLANES:(t + 1) * SUBLANES, :]

    for sub in range(DEC_GROUPS):
        group(sub)

    @pl.when(g + STATE_BUFS < steps)
    def _():
        fetch(g + STATE_BUFS, slot).start()

    @pl.when(g == 0)
    def _():
        w_fetch[1].wait()

    step_rows = [pl.ds(pl.multiple_of(t * n_seq + g * per_step, per_step), per_step) for t in range(t_dec)]
    cat = jnp.concatenate(
        [jnp.concatenate([oa_ref[hd, rows, :] for hd in range(A_HEADS)] + [ob_ref[ls, rows, :] for ls in range(n_ls)],
                         axis=-1) for rows in step_rows], axis=0).astype(_BF16)
    out = _dot(cat, wout_ref[...])
    for t, rows in enumerate(step_rows):
        y_ref[:, t, :] = _rmsnorm(xt_ref[rows, :] + out[t * per_step:(t + 1) * per_step, :], fg_ref[...])


def _decode_call(x, s0, c0, weights, layer):
    n, t_dec, _ = x.shape
    assert n % ROW_TILE == 0 and t_dec == 4
    n_tok = n * t_dec
    x_rows = x.reshape(n, t_dec, D_MODEL // LANES, LANES).transpose(0, 2, 1, 3).reshape(n_tok * D_MODEL // LANES, LANES)
    masks = jnp.asarray(_level_masks((1, 2), t_dec, SUBLANES, same_group=t_dec))
    per_step = DEC_GROUPS * SUBLANES
    kern = functools.partial(_decode_kernel, t_dec=t_dec, layer=layer, steps=n // per_step)
    w_specs = _weight_specs(weights, layer, in_hbm=(1, 9))
    state_spec = pl.BlockSpec((per_step, A_HEADS, HEAD_DIM, HEAD_DIM), lambda i: (i, 0, 0, 0))
    conv_spec = pl.BlockSpec((CONV_WIDTH - 1, per_step, B_WIDTH), lambda i: (0, i, 0))
    y, s_new, c_new = pl.pallas_call(
        kern,
        grid=(n // per_step,),
        in_specs=[_full(x_rows.shape), pl.BlockSpec(memory_space=pl.ANY), conv_spec] + w_specs + [_full(masks.shape)],
        out_specs=[pl.BlockSpec((per_step, t_dec, D_MODEL), lambda i: (i, 0, 0)), state_spec, conv_spec],
        out_shape=[
            jax.ShapeDtypeStruct(x.shape, _F32),
            jax.ShapeDtypeStruct((n, A_HEADS, HEAD_DIM, HEAD_DIM), _F32),
            jax.ShapeDtypeStruct((CONV_WIDTH - 1, n, B_WIDTH), _F32),
        ],
        scratch_shapes=[
            pltpu.VMEM((4 * A_HEADS, n_tok, LANES), _F32),
            pltpu.VMEM((B_WIDTH // LANES, n_tok, LANES), _F32),
            pltpu.VMEM((B_WIDTH // LANES, n_tok, LANES), _F32),
            pltpu.VMEM((A_HEADS, n_tok, LANES), _F32),
            pltpu.VMEM((B_WIDTH // LANES, n_tok, LANES), _F32),
            pltpu.VMEM((1, A_WIDTH), _F32),
            pltpu.VMEM((n_tok, D_MODEL), _F32),
            pltpu.VMEM((STATE_BUFS, per_step, A_HEADS, HEAD_DIM, HEAD_DIM), _F32),
            pltpu.VMEM(weights[1].shape, _BF16),
            pltpu.VMEM(weights[9].shape, _BF16),
            pltpu.SemaphoreType.DMA((STATE_BUFS + 2,)),
        ],
        compiler_params=pltpu.CompilerParams(
            dimension_semantics=("arbitrary",),
            vmem_limit_bytes=VMEM_LIMIT_BYTES,
        ),
        name="hymba_decode",
    )(x_rows, s0, c0, *weights, masks)
    return y, s_new, c_new


def _prepare_weights(norm_in_g, w_in, lb_logits, hgrn_norm_g, b_glu, conv_w, conv_b, ln_g, ln_b, w_out, final_norm_g, layer):
    row = lambda a: a.astype(_F32).reshape(1, -1)
    return (
        row(norm_in_g[layer]),
        w_in[layer].astype(_F32),
        lb_logits.astype(_F32),
        row(hgrn_norm_g[layer]),
        row(b_glu[layer]),
        conv_w.astype(_F32),
        row(conv_b[layer]),
        row(ln_g[layer]),
        row(ln_b[layer]),
        w_out[layer].astype(_F32),
        row(final_norm_g),
    )


def kernel(x_prompt, x_sample, state_hgrn, state_conv, norm_in_g, w_in, lb_logits, hgrn_norm_g, b_glu, conv_w,
           conv_b, ln_g, ln_b, w_out, final_norm_g):
    depth = w_in.shape[0]
    assert depth == 1, "single mixer layer: the final norm is fused into the layer kernel"
    weights = _prepare_weights(norm_in_g, w_in, lb_logits, hgrn_norm_g, b_glu, conv_w, conv_b, ln_g, ln_b, w_out,
                               final_norm_g, 0)
    y_p, s_p, c_p, w_in_b, w_out_b = _prompt_call(x_prompt, weights, tb=1024, layer=0)
    weights = weights[:1] + (w_in_b,) + weights[2:9] + (w_out_b,) + weights[10:]
    y_s, s_s, c_s = _decode_call(x_sample, state_hgrn[0], jnp.transpose(state_conv[0], (1, 0, 2)), weights, layer=0)
    return (y_p, y_s, s_p[None], jnp.transpose(c_p, (1, 0, 2))[None], s_s[None], jnp.transpose(c_s, (1, 0, 2))[None])
```

```python
import functools
import math

import numpy as np
import jax
import jax.numpy as jnp
from jax import lax
from jax.experimental import pallas as pl
from jax.experimental.pallas import tpu as pltpu

D_MODEL = 1024
A_HEADS = 4
HEAD_DIM = 128
A_WIDTH = A_HEADS * HEAD_DIM
B_WIDTH = 512
CONV_WIDTH = 31
EPS = 1e-6
IN_WIDTH = 4 * A_WIDTH + 3 * B_WIDTH
GLU_OFF = 4 * A_WIDTH
ZB_OFF = GLU_OFF + 2 * B_WIDTH

LANES = 128
SUBLANES = 8
MXU_N = 256
CHUNK = SUBLANES * SUBLANES
ROW_TILE = 128
UNROLL = 8
CHUNK_UNROLL = 2
CHUNK_LAG = 4
CONV_BUFS = 3
DEC_GROUPS = 2
STATE_BUFS = 4
HIST = 32
FIRST = HIST - (CONV_WIDTH - 1)
VMEM_LIMIT_BYTES = 56 * 1024 * 1024
LOG2E = math.log2(math.e)

_F32 = jnp.float32
_BF16 = jnp.bfloat16


def _sigmoid(x):
    return 1.0 / (1.0 + jnp.exp(-x))


def _silu(x):
    return x * _sigmoid(x)


def _dot(a, b):
    return jnp.dot(a, b, preferred_element_type=_F32)


def _dot_nt(a, b):
    return lax.dot_general(a, b, (((1,), (1,)), ((), ())), preferred_element_type=_F32)


def _dot_tn(a, b):
    return lax.dot_general(a, b, (((0,), (0,)), ((), ())), preferred_element_type=_F32)


def _level_masks(levels, n_res, n_rows, same_group=None):
    r = np.arange(n_res * n_rows)
    t = n_res * (r % n_rows) + r // n_rows
    tt, ss = t[:, None], t[None, :]
    masks = [tt == ss]
    for b in levels:
        masks.append(((tt // b) == (ss // b) + 1) & ((ss // b) % 2 == 0))
    masks = np.stack(masks)
    want = (ss <= tt) if same_group is None else ((ss <= tt) & (tt // same_group == ss // same_group))
    assert (masks.sum(0) == want).all()
    return masks.astype(np.float32)


def _rows(x, r):
    return jnp.broadcast_to(x[r:r + 1, :], x.shape)


def _hgrn_stages(q, f, v, masks_ref, get_st, coarse, done):
    n_res = len(q)
    every = range(n_res)
    k = [1.0 - fi for fi in f]
    lf = [jnp.log(fi) * LOG2E for fi in f]
    g_in = [lf[0]]
    for i in range(1, n_res):
        g_in.append(g_in[-1] + lf[i])
    tot = g_in[-1]
    zero = jnp.zeros_like(tot)

    def cat(parts):
        return jnp.concatenate(parts, axis=0).astype(_BF16)

    def slab_rows(x, i):
        return x[i * SUBLANES:(i + 1) * SUBLANES, :]

    v_all = cat(v)
    s_rows = [None] * n_res
    pending = []

    def fold():
        level, rows, s_l = pending.pop(0)
        for n, i in enumerate(rows):
            term = masks_ref[level, i * SUBLANES:(i + 1) * SUBLANES, :] * slab_rows(s_l, n)
            s_rows[i] = term if s_rows[i] is None else s_rows[i] + term

    def step(rows, qt, kt):
        pending.append((step.level, rows, _dot_nt(cat(qt), cat(kt))))
        step.level += 1
    step.level = 0

    step(every, q, k)
    yield
    b = 1
    while b < n_res:
        odd = [i for i in every if (i // b) % 2 == 1]
        qt, kt = [], []
        for i in every:
            bs = (i // b) * b
            if i in odd:
                qt.append(q[i] * jnp.exp2(g_in[i] - g_in[bs - 1]))
                kt.append(zero)
            else:
                be = bs + b - 1
                kt.append(k[i] * jnp.exp2(g_in[be] - g_in[i]) if i != be else k[i])
        step(odd, qt, kt)
        yield
        fold()
        b *= 2
    qe = [q[i] * jnp.exp2(g_in[i]) for i in every]
    ke = [k[i] * jnp.exp2(tot - g_in[i]) for i in range(n_res - 1)] + [k[-1]]
    if coarse:
        row = lax.broadcasted_iota(jnp.int32, tot.shape, 0)
        pref = tot
        for sh in (1, 2, 4):
            pref = pref + jnp.where(row >= sh, pltpu.roll(pref, sh, 0), 0.0)
        before = pref - tot
        end_all = _rows(pref, SUBLANES - 1)
        qe_b, ke_b = cat(qe), cat(ke)
        tile = lambda x: jnp.concatenate([x] * n_res, axis=0).astype(_BF16)
        for gsz in (1, 2, 4):
            if gsz == 1:
                qt, kt = qe_b, ke_b
            else:
                if gsz == 2:
                    g_start = jnp.where(row % 2 == 0, before, pltpu.roll(before, 1, 0))
                    g_end = jnp.where(row % 2 == 1, pref, pltpu.roll(pref, SUBLANES - 1, 0))
                else:
                    g_start = jnp.where(row < 4, _rows(before, 0), _rows(before, 4))
                    g_end = jnp.where(row < 4, _rows(pref, 3), _rows(pref, 7))
                qt = qe_b * tile(jnp.exp2(before - g_start))
                kt = ke_b * tile(jnp.exp2(g_end - pref))
            pending.append((step.level, every, _dot_nt(qt, kt)))
            step.level += 1
            yield
            fold()
        qc = qe_b * tile(jnp.exp2(before))
        kc = ke_b * tile(jnp.exp2(end_all - pref))
        o_st = _dot_nt(qc, get_st().astype(_BF16))
        decay = jnp.exp2(pref[SUBLANES - 1:SUBLANES, :])
        yield
    else:
        qc, kc = cat(qe), cat(ke)
        o_st = None
        decay = jnp.exp2(tot)
    while pending:
        fold()
    o = _dot(cat(s_rows), v_all)
    yield
    done(o if o_st is None else o + o_st, qc, kc, v_all, decay)


def _round_robin(starts):
    waiting = list(starts)
    live = []
    rnd = 0
    while waiting or live:
        if waiting and rnd % CHUNK_LAG == 0:
            live += waiting.pop(0)
        for g in list(live):
            try:
                next(g)
            except StopIteration:
                live.remove(g)
        rnd += 1


def _aligned(x, m):
    return x if isinstance(x, int) else pl.multiple_of(x, m)


def _loop(trips, body):
    if trips == 1:
        body(0, 0)
    else:
        lax.fori_loop(0, trips, body, 0)


def _head_out(o, ag, gate):
    ms = jnp.mean(o * o, axis=-1, keepdims=True)
    return o * lax.rsqrt(ms + EPS) * ag * gate


def _forget_lower_bound(logits_ref, layer):
    rows = [logits_ref[i:i + 1, :] for i in range(logits_ref.shape[0])]
    top = functools.reduce(jnp.maximum, rows)
    e = [jnp.exp(r - top) for r in rows]
    total = functools.reduce(jnp.add, e)
    return functools.reduce(jnp.add, [v / total for v in e[:layer + 1]])


def _project_a(h, win_ref, lb_ref, store, cg):
    p = _dot(h(), win_ref[:, cg * MXU_N:(cg + 1) * MXU_N])
    arr = cg // 2
    if arr == 0 or arr == 3:
        p = _silu(p)
    elif arr == 1:
        lb = lb_ref[:, (cg % 2) * MXU_N:(cg % 2 + 1) * MXU_N]
        p = lb + (1.0 - lb) * _sigmoid(p)
    store(2 * cg, p[:, :LANES])
    store(2 * cg + 1, p[:, LANES:])


def _project_b(h, win_ref, bglu_ref, half):
    c0 = half * MXU_N
    glu_a = _dot(h(), win_ref[:, GLU_OFF + c0:GLU_OFF + c0 + MXU_N]) + bglu_ref[:, c0:c0 + MXU_N]
    glu_b = (_dot(h(), win_ref[:, GLU_OFF + B_WIDTH + c0:GLU_OFF + B_WIDTH + c0 + MXU_N])
             + bglu_ref[:, B_WIDTH + c0:B_WIDTH + c0 + MXU_N])
    zb = _dot(h(), win_ref[:, ZB_OFF + c0:ZB_OFF + c0 + MXU_N])
    return glu_a * _sigmoid(glu_b), _silu(zb)


def _rmsnorm(x, g):
    return x * lax.rsqrt(jnp.mean(x * x, axis=-1, keepdims=True) + EPS) * g


def _group_b_out(cv, lng_ref, lnb_ref, gate_b):
    mu = jnp.mean(cv, axis=-1, keepdims=True)
    d = cv - mu
    var = jnp.mean(d * d, axis=-1, keepdims=True)
    ln = d * lax.rsqrt(var + EPS) * lng_ref[...] + lnb_ref[...]
    return _silu(ln) * gate_b


def _cast_weight(src_hbm, dst_ref, slab_refs, sems):
    n = dst_ref.shape[1] // LANES
    slots = len(slab_refs)

    def copy(j):
        return pltpu.make_async_copy(src_hbm.at[:, pl.ds(j * LANES, LANES)], slab_refs[j % slots], sems.at[j % slots])

    for j in range(min(slots, n)):
        copy(j).start()
    for j in range(n):
        copy(j).wait()
        dst_ref[:, j * LANES:(j + 1) * LANES] = slab_refs[j % slots][...].astype(_BF16)
        if j + slots < n:
            copy(j + slots).start()


def _conv_stages(ua_ref, ub_ref, wb_ref, convb_ref, lng_ref, lnb_ref, gb_ref, cv_ref, cat_ref, base, n):
    reps = lambda w, rows: jnp.concatenate([w] * (rows // (2 * SUBLANES)), axis=0)
    for ls in range(B_WIDTH // LANES):
        lanes = slice(ls * LANES, (ls + 1) * LANES)
        acc = jnp.broadcast_to(convb_ref[:, lanes], (n, LANES))
        for off in range(SUBLANES):
            taps = [j for j in range(CONV_WIDTH) if (FIRST + j) % SUBLANES == off]
            rows = n + (2 * SUBLANES if off else 0)
            part = None
            for j in taps:
                rel = FIRST + j - off
                if rel % (2 * SUBLANES) == 0:
                    win = ua_ref[pl.ds(_aligned(base + rel, 2 * SUBLANES), rows), lanes]
                else:
                    win = ub_ref[pl.ds(_aligned(base + rel + SUBLANES, 2 * SUBLANES), rows), lanes]
                term = reps(wb_ref[j, :, lanes], rows) * win
                part = term if part is None else part + term
            part = part.astype(_F32)
            acc = acc + (part[off:off + n, :] if off else part)
            if off == SUBLANES // 2 - 1:
                yield
        cv_ref[:, lanes] = acc
        yield
    rows = pl.ds(_aligned(base, n), n)
    o_b = _group_b_out(cv_ref[...], lng_ref, lnb_ref, gb_ref[rows, :])
    cat_ref[rows, A_WIDTH:] = o_b.astype(_BF16)


def _prompt_kernel(x_ref, normg_ref, win_hbm, lbl_ref, ag_ref, bglu_ref, convw_ref, convb_ref,
                   lng_ref, lnb_ref, wout_hbm, fg_ref, masks_ref,
                   y_ref, st_out_ref, cs_out_ref, win_out_hbm, wout_out_hbm,
                   pa_ref, uh_ref, ua_ref, ub_ref, gb_ref, cv_ref, oa_ref, cat_ref, st_ref,
                   win_ref, wout_ref, wout_f32_ref, lb_ref, wb_ref, sems, *, tb, layer):
    seq = pl.program_id(0)
    t = pl.program_id(1)
    nt = pl.num_programs(1)

    first = (seq == 0) & (t == 0)
    n_slabs = pa_ref.shape[0]
    handoff = [pltpu.make_async_copy(win_ref, win_out_hbm, sems.at[n_slabs]),
               pltpu.make_async_copy(wout_ref, wout_out_hbm, sems.at[n_slabs + 1])]
    wout_fetch = pltpu.make_async_copy(wout_hbm, wout_f32_ref, sems.at[n_slabs + 2])

    @pl.when(first)
    def _():
        _cast_weight(win_hbm, win_ref, [pa_ref.at[s, pl.ds(0, win_ref.shape[0])] for s in range(n_slabs)], sems)
        wout_fetch.start()
        handoff[0].start()
        lb_ref[...] = _forget_lower_bound(lbl_ref, layer)
        for j in range(CONV_WIDTH):
            wb_ref[j] = jnp.broadcast_to(convw_ref[j:j + 1, :], (2 * SUBLANES, B_WIDTH)).astype(_BF16)
        cs_out_ref[...] = jnp.zeros_like(cs_out_ref)

    @pl.when(t == 0)
    def _():
        st_ref[...] = jnp.zeros_like(st_ref)
        uh_ref[...] = jnp.zeros_like(uh_ref)
        ua_ref[...] = jnp.zeros_like(ua_ref)
        ub_ref[...] = jnp.zeros_like(ub_ref)

    def front_tile(r0):
        rows = pl.ds(r0, ROW_TILE)
        h = _rmsnorm(x_ref[rows, :], normg_ref[...]).astype(_BF16)
        for half in range(B_WIDTH // MXU_N):
            u, gb = _project_b(lambda: h, win_ref, bglu_ref, half)
            cols = slice(half * MXU_N, (half + 1) * MXU_N)
            here = pl.ds(_aligned(HIST + r0, HIST), ROW_TILE)
            prev = uh_ref[pl.ds(_aligned(HIST + r0 - SUBLANES, SUBLANES), SUBLANES), cols]
            uh_ref[here, cols] = u
            ua_ref[here, cols] = u.astype(_BF16)
            ub_ref[here, cols] = jnp.concatenate([prev, u[:ROW_TILE - SUBLANES, :]], axis=0).astype(_BF16)
            gb_ref[rows, cols] = gb

        def store_a(slab, val):
            pa_ref[slab, rows, :] = val
        for cg in range(2 * A_HEADS):
            _project_a(lambda: h, win_ref, lb_ref, store_a, cg)

    def front_body(p, carry):
        for n in range(UNROLL):
            front_tile(_aligned((UNROLL * p + n) * ROW_TILE, ROW_TILE))
        return carry

    _loop(tb // (UNROLL * ROW_TILE), front_body)
    ub_ref[HIST + tb:HIST + tb + 2 * SUBLANES, :] = jnp.concatenate(
        [uh_ref[HIST + tb - SUBLANES:HIST + tb, :], jnp.zeros((SUBLANES, B_WIDTH), _F32)], axis=0).astype(_BF16)

    def chunk(base, slot):
        def head(hd):
            def slabs(arr):
                return [pa_ref[arr * A_HEADS + hd, pl.ds(base + i, SUBLANES, stride=SUBLANES), :]
                        for i in range(SUBLANES)]

            def done(o, qc, kc, v_all, decay):
                st_ref[hd] = st_ref[hd] * decay + _dot_tn(v_all, kc)
                for i in range(SUBLANES):
                    oa_ref[hd, pl.ds(base + i, SUBLANES, stride=SUBLANES), :] = o[i * SUBLANES:(i + 1) * SUBLANES, :]
            return _hgrn_stages(slabs(0), slabs(1), slabs(2), masks_ref, lambda: st_ref[hd], True, done)
        conv = _conv_stages(ua_ref, ub_ref, wb_ref, convb_ref, lng_ref, lnb_ref, gb_ref, cv_ref.at[slot], cat_ref, base, CHUNK)
        return [head(hd) for hd in range(A_HEADS)] + [conv]

    def chunk_body(p, carry):
        _round_robin([chunk(_aligned((CHUNK_UNROLL * p + n) * CHUNK, CHUNK), n % CONV_BUFS) for n in range(CHUNK_UNROLL)])
        return carry

    _loop(tb // (CHUNK_UNROLL * CHUNK), chunk_body)

    @pl.when(t == nt - 1)
    def _():
        grp = pl.ds(pl.multiple_of((seq // SUBLANES) * SUBLANES, SUBLANES), SUBLANES)
        mine = lax.broadcasted_iota(jnp.int32, (SUBLANES, B_WIDTH), 0) == seq % SUBLANES
        for kk in range(CONV_WIDTH - 1):
            row = uh_ref[HIST + tb - (CONV_WIDTH - 1) + kk:HIST + tb - (CONV_WIDTH - 1) + kk + 1, :]
            cs_out_ref[kk, grp, :] = jnp.where(mine, jnp.broadcast_to(row, (SUBLANES, B_WIDTH)), cs_out_ref[kk, grp, :])
        for hd in range(A_HEADS):
            st_out_ref[hd] = st_ref[hd].T

    uh_ref[0:HIST, :] = uh_ref[tb:tb + HIST, :]
    ua_ref[0:HIST, :] = ua_ref[tb:tb + HIST, :]
    ub_ref[0:HIST, :] = ub_ref[tb:tb + HIST, :]

    @pl.when(first)
    def _():
        wout_fetch.wait()
        for r0 in range(0, wout_ref.shape[0], ROW_TILE):
            wout_ref[r0:r0 + ROW_TILE, :] = wout_f32_ref[r0:r0 + ROW_TILE, :].astype(_BF16)
        handoff[1].start()

    def back_body(p, carry):
        tiles = [pl.ds(_aligned((UNROLL * p + n) * ROW_TILE, ROW_TILE), ROW_TILE) for n in range(UNROLL)]
        for rows in tiles:
            for hd in range(A_HEADS):
                on = _head_out(oa_ref[hd, rows, :], ag_ref[...], pa_ref[3 * A_HEADS + hd, rows, :])
                cat_ref[rows, hd * HEAD_DIM:(hd + 1) * HEAD_DIM] = on.astype(_BF16)
        outs = [_dot(cat_ref[rows, :], wout_ref[...]) for rows in tiles]
        for rows, out in zip(tiles, outs):
            y_ref[rows, :] = _rmsnorm(x_ref[rows, :] + out, fg_ref[...])
        return carry

    _loop(tb // (UNROLL * ROW_TILE), back_body)

    @pl.when(first)
    def _():
        for copy in handoff:
            copy.wait()


def _weight_specs(weights, layer, in_hbm=()):
    specs = [_full(w.shape) for w in weights]
    taps, _, width = weights[5].shape
    specs[5] = pl.BlockSpec((taps, None, width), lambda *_: (0, layer, 0), pipeline_mode=pl.Buffered(1))
    for i in in_hbm:
        specs[i] = pl.BlockSpec(memory_space=pl.ANY)
    return specs


def _full(shape):
    nd = len(shape)
    return pl.BlockSpec(shape, lambda *_: (0,) * nd, pipeline_mode=pl.Buffered(1))


def _prompt_call(x, weights, tb, layer):
    n, t, _ = x.shape
    assert t % tb == 0 and tb % (UNROLL * ROW_TILE) == 0 and tb % (CHUNK_UNROLL * CHUNK) == 0
    masks = jnp.asarray(_level_masks((1, 2, 4, 8, 16, 32), SUBLANES, SUBLANES))
    assert n % SUBLANES == 0
    kern = functools.partial(_prompt_kernel, tb=tb, layer=layer)
    w_in, w_out = weights[1], weights[9]
    assert w_in.dtype == _F32 and w_out.dtype == _F32 and w_in.shape[0] <= tb and w_in.shape[1] % LANES == 0
    in_hbm = pl.BlockSpec(memory_space=pl.ANY)
    w_specs = _weight_specs(weights, layer, in_hbm=(1, 9))
    return pl.pallas_call(
        kern,
        grid=(n, t // tb),
        in_specs=[pl.BlockSpec((None, tb, D_MODEL), lambda i, j: (i, j, 0))] + w_specs + [_full(masks.shape)],
        out_specs=[
            pl.BlockSpec((None, tb, D_MODEL), lambda i, j: (i, j, 0)),
            pl.BlockSpec((None, A_HEADS, HEAD_DIM, HEAD_DIM), lambda i, j: (i, 0, 0, 0)),
            _full((CONV_WIDTH - 1, n, B_WIDTH)),
            in_hbm,
            in_hbm,
        ],
        out_shape=[
            jax.ShapeDtypeStruct((n, t, D_MODEL), _F32),
            jax.ShapeDtypeStruct((n, A_HEADS, HEAD_DIM, HEAD_DIM), _F32),
            jax.ShapeDtypeStruct((CONV_WIDTH - 1, n, B_WIDTH), _F32),
            jax.ShapeDtypeStruct(w_in.shape, _BF16),
            jax.ShapeDtypeStruct(w_out.shape, _BF16),
        ],
        scratch_shapes=[
            pltpu.VMEM((4 * A_HEADS, tb, LANES), _F32),
            pltpu.VMEM((HIST + tb, B_WIDTH), _F32),
            pltpu.VMEM((HIST + tb + 2 * SUBLANES, B_WIDTH), _BF16),
            pltpu.VMEM((HIST + tb + 2 * SUBLANES, B_WIDTH), _BF16),
            pltpu.VMEM((tb, B_WIDTH), _F32),
            pltpu.VMEM((CONV_BUFS, CHUNK, B_WIDTH), _F32),
            pltpu.VMEM((A_HEADS, tb, LANES), _F32),
            pltpu.VMEM((tb, A_WIDTH + B_WIDTH), _BF16),
            pltpu.VMEM((A_HEADS, HEAD_DIM, HEAD_DIM), _F32),
            pltpu.VMEM(w_in.shape, _BF16),
            pltpu.VMEM(w_out.shape, _BF16),
            pltpu.VMEM(w_out.shape, _F32),
            pltpu.VMEM((1, A_WIDTH), _F32),
            pltpu.VMEM((CONV_WIDTH, 2 * SUBLANES, B_WIDTH), _BF16),
            pltpu.SemaphoreType.DMA((4 * A_HEADS + 3,)),
        ],
        compiler_params=pltpu.CompilerParams(
            dimension_semantics=("arbitrary", "arbitrary"),
            vmem_limit_bytes=VMEM_LIMIT_BYTES,
        ),
        name="hymba_prompt",
    )(x, *weights, masks)


def _decode_kernel(x_ref, s0_hbm, c0_ref, normg_ref, win_hbm, lbl_ref, ag_ref, bglu_ref, convw_ref, convb_ref,
                   lng_ref, lnb_ref, wout_hbm, fg_ref, masks_ref,
                   y_ref, s_out_ref, cs_out_ref,
                   pa_ref, ub_ref, gb_ref, oa_ref, ob_ref, lb_ref, xt_ref, s0_ref, win_ref, wout_ref, sems, *, t_dec, layer, steps):
    g = pl.program_id(0)
    per_step = DEC_GROUPS * SUBLANES
    slot = g % STATE_BUFS

    def fetch(step, slot):
        return pltpu.make_async_copy(s0_hbm.at[pl.ds(step * per_step, per_step)], s0_ref.at[slot], sems.at[slot])
    n_seq = pa_ref.shape[1] // t_dec
    lane_tiles = D_MODEL // LANES
    grp = SUBLANES * t_dec
    n_ls = B_WIDTH // LANES
    tiles = [(pl.ds(t * n_seq + s0, ROW_TILE), pl.ds(s0, ROW_TILE), t)
             for t in range(t_dec) for s0 in range(0, n_seq, ROW_TILE)]

    w_fetch = [pltpu.make_async_copy(win_hbm, win_ref, sems.at[STATE_BUFS]),
               pltpu.make_async_copy(wout_hbm, wout_ref, sems.at[STATE_BUFS + 1])]

    @pl.when(g == 0)
    def _():
        w_fetch[0].start()
        for k in range(min(STATE_BUFS, steps)):
            fetch(k, k).start()
        w_fetch[1].start()
        lb_ref[...] = _forget_lower_bound(lbl_ref, layer)
        w_fetch[0].wait()
        for rows, seq_rows, t in tiles:
            for j in range(lane_tiles):
                start = (seq_rows.start * lane_tiles + j) * t_dec + t
                xt_ref[rows, j * LANES:(j + 1) * LANES] = x_ref[pl.ds(start, ROW_TILE, stride=lane_tiles * t_dec), :]
            h = _rmsnorm(xt_ref[rows, :], normg_ref[...]).astype(_BF16)
            for half in range(B_WIDTH // MXU_N):
                u, gb = _project_b(lambda: h, win_ref, bglu_ref, half)
                for c in range(MXU_N // LANES):
                    ls = half * (MXU_N // LANES) + c
                    ub_ref[ls, rows, :] = u[:, c * LANES:(c + 1) * LANES]
                    gb_ref[ls, rows, :] = gb[:, c * LANES:(c + 1) * LANES]

            def store_a(slab, val, rows=rows):
                pa_ref[slab, rows, :] = val
            for cg in range(2 * A_HEADS):
                _project_a(lambda: h, win_ref, lb_ref, store_a, cg)

    fetch(g, slot).wait()

    def group(sub):
        seqs = slice(sub * SUBLANES, (sub + 1) * SUBLANES)
        seq0 = pl.multiple_of((g * DEC_GROUPS + sub) * SUBLANES, SUBLANES)
        token_rows = lambda t: pl.ds(t * n_seq + seq0, SUBLANES)

        def token_slab(ref, idx, t):
            return ref[idx, token_rows(t), :]

        u_tok = [jnp.concatenate([token_slab(ub_ref, ls, t) for ls in range(n_ls)], axis=-1) for t in range(t_dec)]
        full = lambda kk: c0_ref[kk, seqs, :] if kk < CONV_WIDTH - 1 else u_tok[kk - (CONV_WIDTH - 1)]
        for kk in range(CONV_WIDTH - 1):
            cs_out_ref[kk, seqs, :] = full(kk + t_dec)
        for t in range(t_dec):
            cv = jnp.broadcast_to(convb_ref[...], (SUBLANES, B_WIDTH))
            for j in range(CONV_WIDTH):
                cv = cv + convw_ref[j:j + 1, :] * full(t + j)
            gate_b = jnp.concatenate([token_slab(gb_ref, ls, t) for ls in range(n_ls)], axis=-1)
            o_b = _group_b_out(cv, lng_ref, lnb_ref, gate_b)
            for ls in range(n_ls):
                ob_ref[ls, token_rows(t), :] = o_b[:, ls * LANES:(ls + 1) * LANES]

        row32 = lax.broadcasted_iota(jnp.int32, (grp, LANES), 0) % SUBLANES
        row64 = lax.broadcasted_iota(jnp.int32, (2 * grp, LANES), 0)
        for hd in range(A_HEADS):
            slabs = lambda arr: [token_slab(pa_ref, arr * A_HEADS + hd, t) for t in range(t_dec)]
            res = []
            for _ in _hgrn_stages(slabs(0), slabs(1), slabs(2), masks_ref, None, False, lambda *a: res.extend(a)):
                pass
            o, qc, kc, v_all, decay = res
            qf, kf, vf = qc.astype(_F32), kc.astype(_F32), v_all.astype(_F32)
            e_hi = decay.astype(_BF16).astype(_F32)
            e_mid = (decay - e_hi).astype(_BF16).astype(_F32)
            e_lo = (decay - e_hi - e_mid).astype(_BF16).astype(_F32)
            lhs = jnp.concatenate([kf, e_hi, e_mid, e_lo, jnp.zeros((SUBLANES, LANES), _F32)], axis=0).astype(_BF16)
            v_pad = jnp.concatenate([vf, jnp.zeros((grp, LANES), _F32)], axis=0)
            o_state = None
            for s in range(SUBLANES):
                s0 = s0_ref[slot, sub * SUBLANES + s, hd]
                term = _dot(jnp.where(row32 == s, qf, 0.0).astype(_BF16), s0.astype(_BF16))
                o_state = term if o_state is None else o_state + term
                mine = row64 % SUBLANES == s
                rhs = jnp.concatenate([
                    jnp.where(mine & (row64 < grp), v_pad, 0.0),
                    jnp.where(mine & (row64 >= grp) & (row64 < grp + 3 * SUBLANES), 1.0, 0.0)], axis=-1).astype(_BF16)
                upd = _dot_tn(lhs, rhs)
                s_out_ref[sub * SUBLANES + s, hd] = s0 * upd[:, HEAD_DIM:] + upd[:, :HEAD_DIM]
            gate = jnp.concatenate([token_slab(pa_ref, 3 * A_HEADS + hd, t) for t in range(t_dec)], axis=0)
            on = _head_out(o + o_state, ag_ref[...], gate)
            for t in range(t_dec):
                oa_ref[hd, token_rows(t), :] = on[t * SUBLANES:(t + 1) * SUBLANES, :]

    for sub in range(DEC_GROUPS):
        group(sub)

    @pl.when(g + STATE_BUFS < steps)
    def _():
        fetch(g + STATE_BUFS, slot).start()

    @pl.when(g == 0)
    def _():
        w_fetch[1].wait()

    step_rows = [pl.ds(pl.multiple_of(t * n_seq + g * per_step, per_step), per_step) for t in range(t_dec)]
    cat = jnp.concatenate(
        [jnp.concatenate([oa_ref[hd, rows, :] for hd in range(A_HEADS)] + [ob_ref[ls, rows, :] for ls in range(n_ls)],
                         axis=-1) for rows in step_rows], axis=0).astype(_BF16)
    out = _dot(cat, wout_ref[...])
    for t, rows in enumerate(step_rows):
        y_ref[:, t, :] = _rmsnorm(xt_ref[rows, :] + out[t * per_step:(t + 1) * per_step, :], fg_ref[...])


def _decode_call(x, s0, c0, weights, layer):
    n, t_dec, _ = x.shape
    assert n % ROW_TILE == 0 and t_dec == 4
    n_tok = n * t_dec
    x_rows = x.reshape(n, t_dec, D_MODEL // LANES, LANES).transpose(0, 2, 1, 3).reshape(n_tok * D_MODEL // LANES, LANES)
    masks = jnp.asarray(_level_masks((1, 2), t_dec, SUBLANES, same_group=t_dec))
    per_step = DEC_GROUPS * SUBLANES
    kern = functools.partial(_decode_kernel, t_dec=t_dec, layer=layer, steps=n // per_step)
    w_specs = _weight_specs(weights, layer, in_hbm=(1, 9))
    state_spec = pl.BlockSpec((per_step, A_HEADS, HEAD_DIM, HEAD_DIM), lambda i: (i, 0, 0, 0))
    conv_spec = pl.BlockSpec((CONV_WIDTH - 1, per_step, B_WIDTH), lambda i: (0, i, 0))
    y, s_new, c_new = pl.pallas_call(
        kern,
        grid=(n // per_step,),
        in_specs=[_full(x_rows.shape), pl.BlockSpec(memory_space=pl.ANY), conv_spec] + w_specs + [_full(masks.shape)],
        out_specs=[pl.BlockSpec((per_step, t_dec, D_MODEL), lambda i: (i, 0, 0)), state_spec, conv_spec],
        out_shape=[
            jax.ShapeDtypeStruct(x.shape, _F32),
            jax.ShapeDtypeStruct((n, A_HEADS, HEAD_DIM, HEAD_DIM), _F32),
            jax.ShapeDtypeStruct((CONV_WIDTH - 1, n, B_WIDTH), _F32),
        ],
        scratch_shapes=[
            pltpu.VMEM((4 * A_HEADS, n_tok, LANES), _F32),
            pltpu.VMEM((B_WIDTH // LANES, n_tok, LANES), _F32),
            pltpu.VMEM((B_WIDTH // LANES, n_tok, LANES), _F32),
            pltpu.VMEM((A_HEADS, n_tok, LANES), _F32),
            pltpu.VMEM((B_WIDTH // LANES, n_tok, LANES), _F32),
            pltpu.VMEM((1, A_WIDTH), _F32),
            pltpu.VMEM((n_tok, D_MODEL), _F32),
            pltpu.VMEM((STATE_BUFS, per_step, A_HEADS, HEAD_DIM, HEAD_DIM), _F32),
            pltpu.VMEM(weights[1].shape, _BF16),
            pltpu.VMEM(weights[9].shape, _BF16),
            pltpu.SemaphoreType.DMA((STATE_BUFS + 2,)),
        ],
        compiler_params=pltpu.CompilerParams(
            dimension_semantics=("arbitrary",),
            vmem_limit_bytes=VMEM_LIMIT_BYTES,
        ),
        name="hymba_decode",
    )(x_rows, s0, c0, *weights, masks)
    return y, s_new, c_new


def _prepare_weights(norm_in_g, w_in, lb_logits, hgrn_norm_g, b_glu, conv_w, conv_b, ln_g, ln_b, w_out, final_norm_g, layer):
    row = lambda a: a.astype(_F32).reshape(1, -1)
    return (
        row(norm_in_g[layer]),
        w_in[layer].astype(_F32),
        lb_logits.astype(_F32),
        row(hgrn_norm_g[layer]),
        row(b_glu[layer]),
        jnp.transpose(conv_w.astype(_F32), (1, 0, 2)),
        row(conv_b[layer]),
        row(ln_g[layer]),
        row(ln_b[layer]),
        w_out[layer].astype(_F32),
        row(final_norm_g),
    )


def kernel(x_prompt, x_sample, state_hgrn, state_conv, norm_in_g, w_in, lb_logits, hgrn_norm_g, b_glu, conv_w,
           conv_b, ln_g, ln_b, w_out, final_norm_g):
    depth = w_in.shape[0]
    assert depth == 1, "single mixer layer: the final norm is fused into the layer kernel"
    weights = _prepare_weights(norm_in_g, w_in, lb_logits, hgrn_norm_g, b_glu, conv_w, conv_b, ln_g, ln_b, w_out,
                               final_norm_g, 0)
    y_p, s_p, c_p, w_in_b, w_out_b = _prompt_call(x_prompt, weights, tb=1024, layer=0)
    weights = weights[:1] + (w_in_b,) + weights[2:9] + (w_out_b,) + weights[10:]
    y_s, s_s, c_s = _decode_call(x_sample, state_hgrn[0], jnp.transpose(state_conv[0], (1, 0, 2)), weights, layer=0)
    return (y_p, y_s, s_p[None], jnp.transpose(c_p, (1, 0, 2))[None], s_s[None], jnp.transpose(c_s, (1, 0, 2))[None])
```
